```python
import math
import jax, jax.numpy as jnp
from jax import lax
import numpy as np

D_MODEL = 2048
BATCH = 1
SEQ = 16384
DEPTH = 1
DEC_BATCH = 128
DEC_SEQ = 8
PAST_LEN = 16384
PAGE_SIZE = 128

GLA_HEADS = 4
GLA_KEY = D_MODEL // 2
GLA_VAL = D_MODEL
GLA_DK = GLA_KEY // GLA_HEADS
GLA_DV = GLA_VAL // GLA_HEADS
GLA_RANK = 16
GLA_TAU = 16.0
GLA_CHUNK = 64
SWA_HEAD_DIM = 64
SWA_HEADS = D_MODEL // SWA_HEAD_DIM
SWA_KV_HEADS = 4
SWA_GROUP = SWA_HEADS // SWA_KV_HEADS
SWA_Q = SWA_HEADS * SWA_HEAD_DIM
SWA_KV = SWA_KV_HEADS * SWA_HEAD_DIM
WINDOW = 128
SWA_BLOCK = WINDOW
ROPE_THETA = 10000.0
D_FF = -(-8 * D_MODEL // (3 * 256)) * 256
EPS = 1e-6
NEG_INF = -1e30
IN_SIZES = (GLA_KEY, GLA_KEY, GLA_VAL, GLA_VAL, GLA_RANK, SWA_Q, SWA_KV, SWA_KV, 2 * D_MODEL)
IN_COLS = sum(IN_SIZES)
IN_SPLITS = [int(s) for s in np.cumsum(IN_SIZES)[:-1]]

kernel_name = "hybrid_gla_swa_sinks_adaln_step"


def rmsnorm(x, g):
    xf = x.astype(jnp.float32)
    y = xf * lax.rsqrt(jnp.mean(xf * xf, axis=-1, keepdims=True) + EPS)
    return (y * g.astype(jnp.float32)).astype(x.dtype)


def rope(x, pos):
    half = SWA_HEAD_DIM // 2
    inv_freq = ROPE_THETA ** (-(jnp.arange(half, dtype=jnp.float32) * 2.0) / SWA_HEAD_DIM)
    ang = pos.astype(jnp.float32)[:, None] * inv_freq[None, :]
    cos = jnp.cos(ang)[None, :, None, :]
    sin = jnp.sin(ang)[None, :, None, :]
    xf = x.astype(jnp.float32)
    x1, x2 = xf[..., :half], xf[..., half:]
    return jnp.concatenate([x1 * cos - x2 * sin, x2 * cos + x1 * sin], axis=-1).astype(x.dtype)


def gla_chunked(q, k, v, log_a, S0):
    B, T, H, _ = q.shape
    C = math.gcd(T, GLA_CHUNK)
    N = T // C
    f32 = jnp.float32

    def to_chunks(a):
        return jnp.moveaxis(a.astype(f32).reshape(B, N, C, H, a.shape[-1]), 1, 0)

    causal = jnp.tril(jnp.ones((C, C), dtype=bool))[None, :, :, None, None]

    def step(S, inp):
        qc, kc, vc, gc = inp
        b = jnp.cumsum(gc, axis=1)
        o_inter = jnp.einsum('bchk,bhkv->bchv', qc * jnp.exp(b), S)
        diff = b[:, :, None] - b[:, None, :]
        decay = jnp.where(causal, jnp.exp(jnp.where(causal, diff, 0.0)), 0.0)
        A = jnp.einsum('bihk,bjhk,bijhk->bhij', qc, kc, decay)
        o_intra = jnp.einsum('bhij,bjhv->bihv', A, vc)
        bC = b[:, -1]
        k_dec = kc * jnp.exp(bC[:, None] - b)
        S_new = jnp.exp(bC)[..., None] * S + jnp.einsum('bchk,bchv->bhkv', k_dec, vc)
        return S_new, o_inter + o_intra

    S_fin, o = lax.scan(step, S0.astype(f32), (to_chunks(q), to_chunks(k), to_chunks(v), to_chunks(log_a)))
    o = jnp.moveaxis(o, 0, 1).reshape(B, T, H, v.shape[-1])
    return o, S_fin


def attn_sinks(q, k, v, valid, sinks):
    s = jnp.einsum('bqhgd,bkhd->bhgqk', q, k).astype(jnp.float32) * (SWA_HEAD_DIM ** -0.5)
    s = jnp.where(valid[None, None, None], s, NEG_INF)
    sink = sinks.astype(jnp.float32).reshape(1, SWA_KV_HEADS, SWA_GROUP, 1, 1)
    m = jnp.maximum(jnp.max(s, axis=-1, keepdims=True), sink)
    p = jnp.exp(s - m)
    denom = jnp.sum(p, axis=-1, keepdims=True) + jnp.exp(sink - m)
    out = jnp.einsum('bhgqk,bkhd->bqhgd', p / denom, v.astype(jnp.float32))
    return out.astype(q.dtype)


def swa_prompt(q, k, v, sinks):
    B, S = q.shape[0], q.shape[1]
    NB = S // SWA_BLOCK
    qb = jnp.moveaxis(q.reshape(B, NB, SWA_BLOCK, SWA_KV_HEADS, SWA_GROUP, SWA_HEAD_DIM), 1, 0)

    def band(a):
        ab = a.reshape(B, NB, SWA_BLOCK, SWA_KV_HEADS, SWA_HEAD_DIM)
        prev = jnp.concatenate([jnp.zeros_like(ab[:, :1]), ab[:, :-1]], axis=1)
        return jnp.moveaxis(jnp.concatenate([prev, ab], axis=2), 1, 0)

    qi = jnp.arange(SWA_BLOCK)[:, None]
    kj = jnp.arange(2 * SWA_BLOCK)[None, :]
    dist = SWA_BLOCK + qi - kj

    def blk(args):
        qn, kn, vn, n = args
        valid = (dist >= 0) & (dist <= WINDOW) & ((n - 1) * SWA_BLOCK + kj >= 0)
        return attn_sinks(qn, kn, vn, valid, sinks)

    out = lax.map(blk, (qb, band(k), band(v), jnp.arange(NB, dtype=jnp.int32)))
    return jnp.moveaxis(out, 0, 1).reshape(B, S, SWA_Q)


def swa_sample(q, k_new, v_new, buf_k, buf_v, sinks):
    B, T = q.shape[0], q.shape[1]
    k_all = jnp.concatenate([buf_k.astype(k_new.dtype), k_new], axis=1)
    v_all = jnp.concatenate([buf_v.astype(v_new.dtype), v_new], axis=1)
    qi = jnp.arange(T)[:, None]
    kj = jnp.arange(WINDOW + T)[None, :]
    dist = WINDOW + qi - kj
    valid = (dist >= 0) & (dist <= WINDOW)
    out = attn_sinks(q, k_all, v_all, valid, sinks).reshape(B, T, SWA_Q)
    return out, k_all[:, -WINDOW:], v_all[:, -WINDOW:]


def mixer(h, pos, S0, buf_k, buf_v, w_in, w_gk2, b_gk2, g_qnorm, g_knorm, sinks, g_gla_out, w_pa, w_pb, w_o):
    B, T, _ = h.shape
    proj = h @ w_in
    gq, gk, gv, gr, glr, sq, sk, sv, bg = jnp.split(proj, IN_SPLITS, axis=-1)
    log_a = jax.nn.log_sigmoid((glr @ w_gk2 + b_gk2).astype(jnp.float32)) / GLA_TAU
    qa = gq.reshape(B, T, GLA_HEADS, GLA_DK) * (GLA_DK ** -0.5)
    ka = gk.reshape(B, T, GLA_HEADS, GLA_DK)
    va = gv.reshape(B, T, GLA_HEADS, GLA_DV)
    if S0 is None:
        S0 = jnp.zeros((B, GLA_HEADS, GLA_DK, GLA_DV), dtype=h.dtype)
    oa, S_new = gla_chunked(qa, ka, va, log_a.reshape(B, T, GLA_HEADS, GLA_DK), S0)
    oa = rmsnorm(oa.astype(h.dtype), g_gla_out).reshape(B, T, GLA_VAL) * jax.nn.silu(gr)
    qb = rope(rmsnorm(sq.reshape(B, T, SWA_HEADS, SWA_HEAD_DIM), g_qnorm), pos)
    qb = qb.reshape(B, T, SWA_KV_HEADS, SWA_GROUP, SWA_HEAD_DIM)
    kb = rope(rmsnorm(sk.reshape(B, T, SWA_KV_HEADS, SWA_HEAD_DIM), g_knorm), pos)
    vb = sv.reshape(B, T, SWA_KV_HEADS, SWA_HEAD_DIM)
    if buf_k is None:
        ob = swa_prompt(qb, kb, vb, sinks)
        new_k, new_v = kb[:, -WINDOW:], vb[:, -WINDOW:]
    else:
        ob, new_k, new_v = swa_sample(qb, kb, vb, buf_k, buf_v, sinks)
    ga, gb = jnp.split(jax.nn.sigmoid(bg), 2, axis=-1)
    merged = ga * (oa @ w_pa) + gb * (ob @ w_pb)
    return merged @ w_o, S_new.astype(S0.dtype), new_k.astype(h.dtype), new_v.astype(h.dtype)


def layer(x, c, pos, S0, buf_k, buf_v, w_ada, b_ada, g_norm_mix, w_in, w_gk2, b_gk2, g_qnorm, g_knorm,
          sinks, g_gla_out, w_pa, w_pb, w_o, g_norm_ffn, w_ffn_gate, w_ffn_up, w_ffn_down):
    B = x.shape[0]
    ada = (jax.nn.silu(c) @ w_ada + b_ada).reshape(B, 6, 1, D_MODEL)
    shift_m, scale_m, gate_m, shift_f, scale_f, gate_f = (ada[:, i] for i in range(6))
    h = rmsnorm(x, g_norm_mix) * (1.0 + scale_m) + shift_m
    mix, S_new, new_k, new_v = mixer(h, pos, S0, buf_k, buf_v, w_in, w_gk2, b_gk2, g_qnorm, g_knorm,
                                     sinks, g_gla_out, w_pa, w_pb, w_o)
    x = x + gate_m * mix
    h2 = rmsnorm(x, g_norm_ffn) * (1.0 + scale_f) + shift_f
    ffn = (jax.nn.silu(h2 @ w_ffn_gate) * (h2 @ w_ffn_up)) @ w_ffn_down
    x = x + gate_f * ffn
    return x, S_new, new_k, new_v


def setup_inputs(seed: int = 0) -> dict:
    key = jax.random.key(seed)
    ks = jax.random.split(key, 24)
    f32 = jnp.float32

    def nrm(k, shape, scale=1.0):
        return jax.random.normal(k, shape, dtype=f32) * scale

    L, D = DEPTH, D_MODEL
    return {
        "x_prompt": nrm(ks[0], (BATCH, SEQ, D)),
        "x_sample": nrm(ks[1], (DEC_BATCH, DEC_SEQ, D)),
        "c_prompt": nrm(ks[2], (BATCH, D)),
        "c_sample": nrm(ks[3], (DEC_BATCH, D)),
        "state_gla": nrm(ks[4], (L, DEC_BATCH, GLA_HEADS, GLA_DK, GLA_DV), 0.5),
        "cache_swa_k": nrm(ks[5], (L, DEC_BATCH, WINDOW, SWA_KV_HEADS, SWA_HEAD_DIM)),
        "cache_swa_v": nrm(ks[6], (L, DEC_BATCH, WINDOW, SWA_KV_HEADS, SWA_HEAD_DIM)),
        "w_ada": nrm(ks[7], (L, D, 6 * D), 0.3 * D ** -0.5),
        "b_ada": nrm(ks[8], (L, 6 * D), 0.02),
        "g_norm_mix": 1.0 + nrm(ks[9], (L, D), 0.02),
        "w_in": nrm(ks[10], (L, D, IN_COLS), D ** -0.5),
        "w_gk2": nrm(ks[11], (L, GLA_RANK, GLA_KEY), GLA_RANK ** -0.5),
        "b_gk2": nrm(ks[12], (L, GLA_KEY), 0.1),
        "g_qnorm": 1.0 + nrm(ks[13], (L, SWA_HEAD_DIM), 0.02),
        "g_knorm": 1.0 + nrm(ks[14], (L, SWA_HEAD_DIM), 0.02),
        "sinks": nrm(ks[15], (L, SWA_HEADS)),
        "g_gla_out": 1.0 + nrm(ks[16], (L, GLA_DV), 0.02),
        "w_pa": nrm(ks[17], (L, GLA_VAL, D), GLA_VAL ** -0.5),
        "w_pb": nrm(ks[18], (L, SWA_Q, D), SWA_Q ** -0.5),
        "w_o": nrm(ks[19], (L, D, D), D ** -0.5),
        "g_norm_ffn": 1.0 + nrm(ks[20], (L, D), 0.02),
        "w_ffn_gate": nrm(ks[21], (L, D, D_FF), D ** -0.5),
        "w_ffn_up": nrm(ks[22], (L, D, D_FF), D ** -0.5),
        "w_ffn_down": nrm(ks[23], (L, D_FF, D), D_FF ** -0.5),
    }


def reference(x_prompt, x_sample, c_prompt, c_sample, state_gla, cache_swa_k, cache_swa_v,
              w_ada, b_ada, g_norm_mix, w_in, w_gk2, b_gk2, g_qnorm, g_knorm, sinks, g_gla_out,
              w_pa, w_pb, w_o, g_norm_ffn, w_ffn_gate, w_ffn_up, w_ffn_down):
    pos_p = jnp.arange(SEQ, dtype=jnp.int32)
    pos_s = PAST_LEN + jnp.arange(DEC_SEQ, dtype=jnp.int32)
    yp, ys = x_prompt, x_sample
    sp_list, kp_list, vp_list, ss_list, ks_list, vs_list = [], [], [], [], [], []
    for l in range(DEPTH):
        w = (w_ada[l], b_ada[l], g_norm_mix[l], w_in[l], w_gk2[l], b_gk2[l], g_qnorm[l], g_knorm[l],
             sinks[l], g_gla_out[l], w_pa[l], w_pb[l], w_o[l], g_norm_ffn[l], w_ffn_gate[l],
             w_ffn_up[l], w_ffn_down[l])
        yp, sp, kp, vp = layer(yp, c_prompt, pos_p, None, None, None, *w)
        ys, ss, kss, vss = layer(ys, c_sample, pos_s, state_gla[l], cache_swa_k[l], cache_swa_v[l], *w)
        sp_list.append(sp); kp_list.append(kp); vp_list.append(vp)
        ss_list.append(ss); ks_list.append(kss); vs_list.append(vss)
    state_gla_prompt = jnp.stack(sp_list)
    cache_swa_k_prompt = jnp.stack(kp_list)
    cache_swa_v_prompt = jnp.stack(vp_list)
    state_gla_sample = jnp.stack(ss_list)
    cache_swa_k_sample = jnp.stack(ks_list)
    cache_swa_v_sample = jnp.stack(vs_list)
    return (yp, ys, state_gla_prompt, cache_swa_k_prompt, cache_swa_v_prompt,
            state_gla_sample, cache_swa_k_sample, cache_swa_v_sample)
```

```python
import functools

import numpy as np
import jax
import jax.numpy as jnp
from jax import lax
from jax.experimental import pallas as pl
from jax.experimental.pallas import tpu as pltpu

F32 = jnp.float32
BF16 = jnp.bfloat16

D_MODEL = 2048
SEQ = 16384
DEC_BATCH = 128
DEC_SEQ = 8
PAST_LEN = 16384
GLA_HEADS = 4
GLA_DK = 256
GLA_DV = 512
GLA_KEY = GLA_HEADS * GLA_DK
GLA_VAL = GLA_HEADS * GLA_DV
GLA_RANK = 16
GLA_TAU = 16.0
SWA_HEAD_DIM = 64
SWA_HEADS = 32
SWA_KV_HEADS = 4
SWA_GROUP = SWA_HEADS // SWA_KV_HEADS
SWA_Q = SWA_HEADS * SWA_HEAD_DIM
SWA_KV = SWA_KV_HEADS * SWA_HEAD_DIM
WINDOW = 128
ROPE_THETA = 10000.0
D_FF = 5632
EPS = 1e-6
NEG_INF = -1e30

SUBLANES = 8
LANES = 128
MXU_DIM = 256
VMEM_LIMIT_BYTES = 56 * 1024 * 1024

COL_GQ = 0
COL_GK = COL_GQ + GLA_KEY
COL_GV = COL_GK + GLA_KEY
COL_GR = COL_GV + GLA_VAL
COL_SQ = COL_GR + GLA_VAL
COL_SK = COL_SQ + SWA_Q
COL_SV = COL_SK + SWA_KV
COL_BG = COL_SV + SWA_KV
PROJ_COLS = COL_BG + 2 * D_MODEL
W_IN_GLR = COL_GR + GLA_VAL

GLA_CHUNK = 256
GLA_SAMPLE_BATCH = 8
SWA_SAMPLE_BATCH = 4
ROW_CHUNK = 128


def _dot(a, b):
    return jnp.dot(a, b, preferred_element_type=F32)


def _dot_nt(a, b):
    return lax.dot_general(a, b, (((1,), (1,)), ((), ())), preferred_element_type=F32)


def _dot_tn(a, b):
    return lax.dot_general(a, b, (((0,), (0,)), ((), ())), preferred_element_type=F32)


def _sigmoid(x):
    return 1.0 / (1.0 + jnp.exp(-x))


def _silu(x):
    return x * _sigmoid(x)


def _split3(x):
    hi = x.astype(BF16)
    r1 = x - hi.astype(F32)
    mid = r1.astype(BF16)
    lo = (r1 - mid.astype(F32)).astype(BF16)
    return hi, mid, lo


def _params(*sem):
    return pltpu.CompilerParams(dimension_semantics=sem, vmem_limit_bytes=VMEM_LIMIT_BYTES)


def _ada_body(c_ref, w_ref, b_ref, o_ref):
    a = _silu(c_ref[...]).astype(BF16)
    o_ref[...] = _dot(a, w_ref[...].astype(BF16)) + b_ref[...]


def _ada(c_all, w_ada, b_ada, tn=512):
    m, d = c_all.shape
    n = w_ada.shape[1]
    return pl.pallas_call(
        _ada_body,
        grid=(n // tn,),
        in_specs=[pl.BlockSpec((m, d), lambda j: (0, 0)),
                  pl.BlockSpec((d, tn), lambda j: (0, j)),
                  pl.BlockSpec((1, tn), lambda j: (0, j))],
        out_specs=pl.BlockSpec((m, tn), lambda j: (0, j)),
        out_shape=jax.ShapeDtypeStruct((m, n), F32),
        compiler_params=_params("arbitrary"),
        name="ada",
    )(c_all, w_ada, b_ada)


def _mod_spec(mod, tm, width, col):
    if mod.shape[0] == 1:
        return pl.BlockSpec((1, width), lambda i, j: (0, col(j)))
    return pl.BlockSpec((tm // DEC_SEQ, width), lambda i, j: (i, col(j)))


def _mod_rows(ref, row0, nrows):
    if ref.shape[0] == 1:
        return ref[...]
    seq0 = row0 // DEC_SEQ
    return jnp.concatenate(
        [jnp.broadcast_to(ref[pl.ds(seq0 + s, 1), :], (DEC_SEQ, ref.shape[1])) for s in range(nrows // DEC_SEQ)],
        axis=0)


def _modulated_norm_rows(x_ref, gn_ref, sc_ref, sh_ref, row0, nrows):
    x = x_ref[pl.ds(row0, nrows), :]
    ms = jnp.mean(x * x, axis=-1, keepdims=True)
    y = (x * lax.rsqrt(ms + EPS)) * gn_ref[...]
    return y * (1.0 + _mod_rows(sc_ref, row0, nrows)) + _mod_rows(sh_ref, row0, nrows)


def _row_loop(total_rows, body):
    def step(r, carry):
        body(pl.multiple_of(r * ROW_CHUNK, ROW_CHUNK))
        return carry
    lax.fori_loop(0, total_rows // ROW_CHUNK, step, 0)


def _inproj_body(x_ref, gn_ref, sc_ref, sh_ref, w_ref, wlr_ref, wgk_ref, bgk_ref,
                 proj_ref, glog_ref, h_scr):
    @pl.when(pl.program_id(1) == 0)
    def _():
        def rows(row0):
            hb = _modulated_norm_rows(x_ref, gn_ref, sc_ref, sh_ref, row0, ROW_CHUNK).astype(BF16)
            h_scr[pl.ds(row0, ROW_CHUNK), :] = hb
            glr = _dot(hb, wlr_ref[...])
            z = _dot(glr.astype(BF16), wgk_ref[...]) + bgk_ref[...]
            log_sig = jnp.minimum(z, 0.0) - jnp.log1p(jnp.exp(-jnp.abs(z)))
            glog_ref[pl.ds(row0, ROW_CHUNK), :] = log_sig * (1.0 / GLA_TAU)
        _row_loop(x_ref.shape[0], rows)

    proj_ref[...] = _dot(h_scr[...], w_ref[...]).astype(BF16)


def _inproj(x, gn, scale, shift, w_main, w_lr, w_gk, b_gk, tm=1024, tn=512):
    t, d = x.shape
    n = w_main.shape[1]
    zero = lambda j: 0
    return pl.pallas_call(
        _inproj_body,
        grid=(t // tm, n // tn),
        in_specs=[pl.BlockSpec((tm, d), lambda i, j: (i, 0)),
                  pl.BlockSpec((1, d), lambda i, j: (0, 0)),
                  _mod_spec(scale, tm, d, zero),
                  _mod_spec(shift, tm, d, zero),
                  pl.BlockSpec((d, tn), lambda i, j: (0, j)),
                  pl.BlockSpec(w_lr.shape, lambda i, j: (0, 0)),
                  pl.BlockSpec(w_gk.shape, lambda i, j: (0, 0)),
                  pl.BlockSpec(b_gk.shape, lambda i, j: (0, 0))],
        out_specs=[pl.BlockSpec((tm, tn), lambda i, j: (i, j)),
                   pl.BlockSpec((tm, GLA_KEY), lambda i, j: (i, 0))],
        out_shape=[jax.ShapeDtypeStruct((t, n), BF16),
                   jax.ShapeDtypeStruct((t, GLA_KEY), F32)],
        scratch_shapes=[pltpu.VMEM((tm, d), BF16)],
        compiler_params=_params("arbitrary", "arbitrary"),
        name="inproj",
    )(x, gn, scale, shift, w_main, w_lr, w_gk, b_gk)


def _gla_level_table(c, group):
    n_levels = int(np.log2(group))
    i = np.arange(c)[:, None]
    j = np.arange(c)[None, :]
    x = np.bitwise_xor(i, j)
    lvl = np.floor(np.log2(np.maximum(x, 1))).astype(np.int32)
    lvl = np.where(i == j, n_levels, lvl)
    valid = (i >= j) & (i // group == j // group)
    return np.where(valid, lvl, -1).astype(np.int32), n_levels


def _gla_tril(c, group):
    i = np.arange(c)[:, None]
    j = np.arange(c)[None, :]
    return ((i >= j) & (i // group == j // group)).astype(np.float32)


def _level_reference(b_scr, rows, level):
    s = 2 ** (level + 1)
    width = b_scr.shape[1]
    pieces = []
    if s >= SUBLANES:
        for blk in range(rows // s):
            mid = blk * s + s // 2
            pieces.append(jnp.broadcast_to(b_scr[mid:mid + 1, :], (s, width)))
    else:
        p = lax.broadcasted_iota(jnp.int32, (SUBLANES, width), 0)
        for tile in range(rows // SUBLANES):
            base = tile * SUBLANES
            mids = [base + q * s + s // 2 for q in range(SUBLANES // s)]
            r = jnp.broadcast_to(b_scr[mids[-1]:mids[-1] + 1, :], (SUBLANES, width))
            for q in range(SUBLANES // s - 2, -1, -1):
                row = jnp.broadcast_to(b_scr[mids[q]:mids[q] + 1, :], (SUBLANES, width))
                r = jnp.where(p < (q + 1) * s, row, r)
            pieces.append(r)
    return jnp.concatenate(pieces, axis=0) if len(pieces) > 1 else pieces[0]


def _gla_intra(q, k, k_bf, b, b_scr, lvl, n_levels):
    rows = q.shape[0]
    a = jnp.where(lvl == n_levels, _dot_nt(q.astype(BF16), k_bf), 0.0)
    for level in range(n_levels):
        f = jnp.exp(-jnp.abs(b - _level_reference(b_scr, rows, level)))
        p = _dot_nt((q * f).astype(BF16), (k * f).astype(BF16))
        a = jnp.where(lvl == level, p, a)
    return a


def _gla_cumsum(g, tril_bf):
    hi, mid, lo = _split3(g)
    return _dot(tril_bf, hi) + _dot(tril_bf, mid) + _dot(tril_bf, lo)


def _gla_out_gate(o, gout_ref, gr_ref):
    ms = jnp.mean(o * o, axis=-1, keepdims=True)
    y = (o * lax.rsqrt(ms + EPS)) * gout_ref[...]
    return (y * _silu(gr_ref[...].astype(F32))).astype(BF16)


def _gla_prompt_body(q_ref, k_ref, v_ref, gr_ref, g_ref, tril_ref, lvl_ref, gout_ref,
                     oa_ref, st_ref, st_scr, b_scr, *, n_levels):
    c = pl.program_id(1)
    rows = q_ref.shape[0]

    @pl.when(c == 0)
    def _():
        st_scr[...] = jnp.zeros_like(st_scr)

    b = _gla_cumsum(g_ref[...], tril_ref[...])
    b_scr[...] = b
    b_end = b_scr[rows - 1:rows, :]
    q = q_ref[...].astype(F32) * (GLA_DK ** -0.5)
    k_bf = k_ref[...]
    k = k_bf.astype(F32)
    v = v_ref[...]
    st = st_scr[...]

    o = _dot_nt((q * jnp.exp(b)).astype(BF16), st.astype(BF16))
    a = _gla_intra(q, k, k_bf, b, b_scr, lvl_ref[...], n_levels)
    o = o + _dot(a.astype(BF16), v)
    k_dec = (k * jnp.exp(b_end - b)).astype(BF16)
    st_scr[...] = st * jnp.exp(b_end) + _dot_tn(v, k_dec)
    oa_ref[...] = _gla_out_gate(o, gout_ref, gr_ref)

    @pl.when(c == pl.num_programs(1) - 1)
    def _():
        st_ref[0] = st_scr[...].T


def _gla_prompt(proj, glog, g_out):
    t = proj.shape[0]
    c = GLA_CHUNK
    lvl, n_levels = _gla_level_table(c, c)
    tril = jnp.asarray(_gla_tril(c, c), BF16)
    lvl = jnp.asarray(lvl)
    qb, kb, vb, rb = COL_GQ // GLA_DK, COL_GK // GLA_DK, COL_GV // GLA_DV, COL_GR // GLA_DV
    return pl.pallas_call(
        functools.partial(_gla_prompt_body, n_levels=n_levels),
        grid=(GLA_HEADS, t // c),
        in_specs=[pl.BlockSpec((c, GLA_DK), lambda h, i: (i, qb + h)),
                  pl.BlockSpec((c, GLA_DK), lambda h, i: (i, kb + h)),
                  pl.BlockSpec((c, GLA_DV), lambda h, i: (i, vb + h)),
                  pl.BlockSpec((c, GLA_DV), lambda h, i: (i, rb + h)),
                  pl.BlockSpec((c, GLA_DK), lambda h, i: (i, h)),
                  pl.BlockSpec((c, c), lambda h, i: (0, 0)),
                  pl.BlockSpec((c, c), lambda h, i: (0, 0)),
                  pl.BlockSpec((1, GLA_DV), lambda h, i: (0, 0))],
        out_specs=[pl.BlockSpec((c, GLA_DV), lambda h, i: (i, h)),
                   pl.BlockSpec((1, GLA_DK, GLA_DV), lambda h, i: (h, 0, 0))],
        out_shape=[jax.ShapeDtypeStruct((t, GLA_VAL), BF16),
                   jax.ShapeDtypeStruct((GLA_HEADS, GLA_DK, GLA_DV), F32)],
        scratch_shapes=[pltpu.VMEM((GLA_DV, GLA_DK), F32), pltpu.VMEM((c, GLA_DK), F32)],
        compiler_params=_params("arbitrary", "arbitrary"),
        name="gla_prompt",
    )(proj, proj, proj, proj, glog, tril, lvl, g_out)


def _gla_sample_body(q_ref, k_ref, v_ref, gr_ref, g_ref, s0_ref, tril_ref, lvl_ref, gout_ref,
                     oa_ref, s1_ref, b_scr, *, n_levels):
    rows = q_ref.shape[0]
    nseq = rows // DEC_SEQ
    b = _gla_cumsum(g_ref[...], tril_ref[...])
    b_scr[...] = b
    b_end = jnp.concatenate(
        [jnp.broadcast_to(b_scr[(s + 1) * DEC_SEQ - 1:(s + 1) * DEC_SEQ, :], (DEC_SEQ, GLA_DK))
         for s in range(nseq)], axis=0)
    q = q_ref[...].astype(F32) * (GLA_DK ** -0.5)
    k_bf = k_ref[...]
    k = k_bf.astype(F32)
    v = v_ref[...]
    qd = (q * jnp.exp(b)).astype(BF16)
    a = _gla_intra(q, k, k_bf, b, b_scr, lvl_ref[...], n_levels)
    o_intra = _dot(a.astype(BF16), v)

    stacked = jnp.concatenate([k * jnp.exp(b_end - b), jnp.exp(b_end)], axis=0)
    stacked_t = stacked.T
    kd_t = stacked_t[:, :rows]
    lane = lax.broadcasted_iota(jnp.int32, kd_t.shape, 1)
    outs = []
    for s in range(nseq):
        s0 = s0_ref[s, 0]
        outs.append(_dot(qd[s * DEC_SEQ:(s + 1) * DEC_SEQ, :], s0.astype(BF16)))
        in_seq = (lane >= s * DEC_SEQ) & (lane < (s + 1) * DEC_SEQ)
        kd_s = jnp.where(in_seq, kd_t, 0.0).astype(BF16)
        col = rows + s * DEC_SEQ
        decay = stacked_t[:, col:col + 1]
        s1_ref[s, 0] = s0 * decay + _dot(kd_s, v)
    o = jnp.concatenate(outs, axis=0) + o_intra
    oa_ref[...] = _gla_out_gate(o, gout_ref, gr_ref)


def _gla_sample(proj, glog, state, g_out):
    t = proj.shape[0]
    nseq = GLA_SAMPLE_BATCH
    rows = nseq * DEC_SEQ
    lvl, n_levels = _gla_level_table(rows, DEC_SEQ)
    tril = jnp.asarray(_gla_tril(rows, DEC_SEQ), BF16)
    lvl = jnp.asarray(lvl)
    qb, kb, vb, rb = COL_GQ // GLA_DK, COL_GK // GLA_DK, COL_GV // GLA_DV, COL_GR // GLA_DV
    return pl.pallas_call(
        functools.partial(_gla_sample_body, n_levels=n_levels),
        grid=(t // rows, GLA_HEADS),
        in_specs=[pl.BlockSpec((rows, GLA_DK), lambda i, h: (i, qb + h)),
                  pl.BlockSpec((rows, GLA_DK), lambda i, h: (i, kb + h)),
                  pl.BlockSpec((rows, GLA_DV), lambda i, h: (i, vb + h)),
                  pl.BlockSpec((rows, GLA_DV), lambda i, h: (i, rb + h)),
                  pl.BlockSpec((rows, GLA_DK), lambda i, h: (i, h)),
                  pl.BlockSpec((nseq, 1, GLA_DK, GLA_DV), lambda i, h: (i, h, 0, 0)),
                  pl.BlockSpec((rows, rows), lambda i, h: (0, 0)),
                  pl.BlockSpec((rows, rows), lambda i, h: (0, 0)),
                  pl.BlockSpec((1, GLA_DV), lambda i, h: (0, 0))],
        out_specs=[pl.BlockSpec((rows, GLA_DV), lambda i, h: (i, h)),
                   pl.BlockSpec((nseq, 1, GLA_DK, GLA_DV), lambda i, h: (i, h, 0, 0))],
        out_shape=[jax.ShapeDtypeStruct((t, GLA_VAL), BF16),
                   jax.ShapeDtypeStruct(state.shape, F32)],
        scratch_shapes=[pltpu.VMEM((rows, GLA_DK), F32)],
        compiler_params=_params("arbitrary", "arbitrary"),
        name="gla_sample",
    )(proj, proj, proj, proj, glog, state, tril, lvl, g_out)


def _rope_tables(pos):
    half = SWA_HEAD_DIM // 2
    inv_freq = ROPE_THETA ** (-(jnp.arange(half, dtype=F32) * 2.0) / SWA_HEAD_DIM)
    ang = pos.astype(F32)[:, None] * inv_freq[None, :]
    cos, sin = jnp.cos(ang), jnp.sin(ang)
    cos_t = jnp.concatenate([cos, cos, cos, cos], axis=-1)
    sin_t = jnp.concatenate([-sin, sin, -sin, sin], axis=-1)
    return cos_t, sin_t


def _head_mean_matrix():
    i = np.arange(MXU_DIM)[:, None] // SWA_HEAD_DIM
    j = np.arange(MXU_DIM)[None, :] // SWA_HEAD_DIM
    return (i == j).astype(np.float32) / SWA_HEAD_DIM


def _qk_norm_rope(x, gain, cos_t, sin_t, mean_mat):
    rows, width = x.shape
    lane = lax.broadcasted_iota(jnp.int32, (rows, LANES), 1)
    first_half = (lane % SWA_HEAD_DIM) < (SWA_HEAD_DIM // 2)
    shift = SWA_HEAD_DIM // 2
    sq = (x * x).astype(BF16)
    tiles = []
    for c in range(width // MXU_DIM):
        ms = _dot(sq[:, c * MXU_DIM:(c + 1) * MXU_DIM], mean_mat)
        for t in range(MXU_DIM // LANES):
            lo = c * MXU_DIM + t * LANES
            y = (x[:, lo:lo + LANES] * lax.rsqrt(ms[:, t * LANES:(t + 1) * LANES] + EPS)) * gain
            rot = jnp.where(first_half, pltpu.roll(y, LANES - shift, 1), pltpu.roll(y, shift, 1))
            tiles.append(y * cos_t + rot * sin_t)
    return tiles


def _attend(q_tiles, k_tile, v_tile, parity_of_kv, valid, sinks_ref, kv_head):
    rows = q_tiles[0].shape[0]
    lane = lax.broadcasted_iota(jnp.int32, k_tile.shape, 1)
    low = lane < SWA_HEAD_DIM

    def place(x, want_low):
        have_low = parity_of_kv == 0
        src = x if have_low == want_low else pltpu.roll(x, SWA_HEAD_DIM, 1)
        return jnp.where(low if want_low else ~low, src, 0.0).astype(BF16)

    q_all = jnp.concatenate(q_tiles, axis=0).astype(BF16)
    npairs = len(q_tiles)
    outs = [None] * npairs
    for parity in range(2):
        kk = place(k_tile, parity == 0)
        vv = place(v_tile, parity == 0)
        s_all = _dot_nt(q_all, kk)
        p_list, inv_list = [], []
        for pair in range(npairs):
            sink = sinks_ref[kv_head * SWA_GROUP + 2 * pair + parity]
            s = jnp.where(valid, s_all[pair * rows:(pair + 1) * rows, :], NEG_INF)
            m = jnp.maximum(jnp.max(s, axis=-1, keepdims=True), sink)
            p = jnp.exp(s - m)
            denom = jnp.sum(p, axis=-1, keepdims=True) + jnp.exp(sink - m)
            p_list.append(p.astype(BF16))
            inv_list.append(1.0 / denom)
        pv = _dot(jnp.concatenate(p_list, axis=0), vv)
        for pair in range(npairs):
            contrib = pv[pair * rows:(pair + 1) * rows, :] * inv_list[pair]
            outs[pair] = contrib if outs[pair] is None else outs[pair] + contrib
    return outs


def _attend_all_heads(q_tiles, k_all, v_all, valid, sinks_ref, store):
    pairs = SWA_GROUP // 2
    for g in range(SWA_KV_HEADS):
        lo = (g // 2) * LANES
        outs = _attend(q_tiles[g * pairs:(g + 1) * pairs], k_all[:, lo:lo + LANES], v_all[:, lo:lo + LANES],
                       g % 2, valid, sinks_ref, g)
        for p, o in enumerate(outs):
            store(g * pairs + p, o.astype(BF16))


def _swa_prompt_body(sinks_ref, q_ref, k_ref, v_ref, cos_ref, sin_ref, gq_ref, gk_ref, mm_ref,
                     ob_ref, knew_ref, vnew_ref, kprev_scr, vprev_scr):
    n = pl.program_id(0)
    blk = q_ref.shape[0]

    @pl.when(n == 0)
    def _():
        kprev_scr[...] = jnp.zeros_like(kprev_scr)
        vprev_scr[...] = jnp.zeros_like(vprev_scr)

    cos_t, sin_t, mean_mat = cos_ref[...], sin_ref[...], mm_ref[...]
    q_tiles = [t * (SWA_HEAD_DIM ** -0.5)
               for t in _qk_norm_rope(q_ref[...].astype(F32), gq_ref[...], cos_t, sin_t, mean_mat)]
    kn = jnp.concatenate(_qk_norm_rope(k_ref[...].astype(F32), gk_ref[...], cos_t, sin_t, mean_mat), axis=1)
    vn = v_ref[...].astype(F32)
    knew_ref[...] = kn
    vnew_ref[...] = vn
    k_all = jnp.concatenate([kprev_scr[...], kn], axis=0)
    v_all = jnp.concatenate([vprev_scr[...], vn], axis=0)

    qi = lax.broadcasted_iota(jnp.int32, (blk, 2 * blk), 0)
    kj = lax.broadcasted_iota(jnp.int32, (blk, 2 * blk), 1)
    dist = blk + qi - kj
    valid = (dist >= 0) & (dist <= WINDOW) & ((n - 1) * blk + kj >= 0)

    def store(tile, value):
        ob_ref[:, tile * LANES:(tile + 1) * LANES] = value
    _attend_all_heads(q_tiles, k_all, v_all, valid, sinks_ref, store)

    kprev_scr[...] = kn
    vprev_scr[...] = vn


def _swa_prompt(proj, sinks, cos_t, sin_t, g_q, g_k):
    t = proj.shape[0]
    blk = WINDOW
    mean_mat = jnp.asarray(_head_mean_matrix(), BF16)
    qb, kb, vb = COL_SQ // SWA_Q, COL_SK // SWA_KV, COL_SV // SWA_KV
    grid_spec = pltpu.PrefetchScalarGridSpec(
        num_scalar_prefetch=1,
        grid=(t // blk,),
        in_specs=[pl.BlockSpec((blk, SWA_Q), lambda n, s: (n, qb)),
                  pl.BlockSpec((blk, SWA_KV), lambda n, s: (n, kb)),
                  pl.BlockSpec((blk, SWA_KV), lambda n, s: (n, vb)),
                  pl.BlockSpec((blk, LANES), lambda n, s: (n, 0)),
                  pl.BlockSpec((blk, LANES), lambda n, s: (n, 0)),
                  pl.BlockSpec((1, LANES), lambda n, s: (0, 0)),
                  pl.BlockSpec((1, LANES), lambda n, s: (0, 0)),
                  pl.BlockSpec((MXU_DIM, MXU_DIM), lambda n, s: (0, 0))],
        out_specs=[pl.BlockSpec((blk, SWA_Q), lambda n, s: (n, 0)),
                   pl.BlockSpec((blk, SWA_KV), lambda n, s: (0, 0)),
                   pl.BlockSpec((blk, SWA_KV), lambda n, s: (0, 0))],
        scratch_shapes=[pltpu.VMEM((blk, SWA_KV), F32), pltpu.VMEM((blk, SWA_KV), F32)],
    )
    return pl.pallas_call(
        _swa_prompt_body,
        grid_spec=grid_spec,
        out_shape=[jax.ShapeDtypeStruct((t, SWA_Q), BF16),
                   jax.ShapeDtypeStruct((blk, SWA_KV), F32),
                   jax.ShapeDtypeStruct((blk, SWA_KV), F32)],
        compiler_params=_params("arbitrary"),
        name="swa_prompt",
    )(sinks, proj, proj, proj, cos_t, sin_t, g_q, g_k, mean_mat)


def _swa_sample_body(sinks_ref, q_ref, k_ref, v_ref, bk_ref, bv_ref, cos_ref, sin_ref, gq_ref, gk_ref, mm_ref,
                     ob_ref, kout_ref, vout_ref):
    nseq = bk_ref.shape[0]
    cos_t = jnp.concatenate([cos_ref[...]] * nseq, axis=0)
    sin_t = jnp.concatenate([sin_ref[...]] * nseq, axis=0)
    mean_mat = mm_ref[...]
    q_tiles = [t * (SWA_HEAD_DIM ** -0.5)
               for t in _qk_norm_rope(q_ref[...].astype(F32), gq_ref[...], cos_t, sin_t, mean_mat)]
    kn = jnp.concatenate(_qk_norm_rope(k_ref[...].astype(F32), gk_ref[...], cos_t, sin_t, mean_mat), axis=1)
    vn = v_ref[...].astype(F32)

    keys = 2 * WINDOW
    qi = lax.broadcasted_iota(jnp.int32, (DEC_SEQ, keys), 0)
    kj = lax.broadcasted_iota(jnp.int32, (DEC_SEQ, keys), 1)
    dist = WINDOW + qi - kj
    valid = (dist >= 0) & (dist <= WINDOW)
    pad = jnp.zeros((keys - WINDOW - DEC_SEQ, SWA_KV), F32)

    for s in range(nseq):
        rsl = slice(s * DEC_SEQ, (s + 1) * DEC_SEQ)
        bk, bv = bk_ref[s], bv_ref[s]
        kout_ref[s, :WINDOW - DEC_SEQ, :] = bk[DEC_SEQ:, :]
        kout_ref[s, WINDOW - DEC_SEQ:, :] = kn[rsl, :]
        vout_ref[s, :WINDOW - DEC_SEQ, :] = bv[DEC_SEQ:, :]
        vout_ref[s, WINDOW - DEC_SEQ:, :] = vn[rsl, :]
        k_all = jnp.concatenate([bk, kn[rsl, :], pad], axis=0)
        v_all = jnp.concatenate([bv, vn[rsl, :], pad], axis=0)

        def store(tile, value, rsl=rsl):
            ob_ref[rsl, tile * LANES:(tile + 1) * LANES] = value
        _attend_all_heads([t[rsl, :] for t in q_tiles], k_all, v_all, valid, sinks_ref, store)


def _swa_sample(proj, buf_k, buf_v, sinks, cos_t, sin_t, g_q, g_k):
    t = proj.shape[0]
    nseq = SWA_SAMPLE_BATCH
    rows = nseq * DEC_SEQ
    mean_mat = jnp.asarray(_head_mean_matrix(), BF16)
    qb, kb, vb = COL_SQ // SWA_Q, COL_SK // SWA_KV, COL_SV // SWA_KV
    grid_spec = pltpu.PrefetchScalarGridSpec(
        num_scalar_prefetch=1,
        grid=(t // rows,),
        in_specs=[pl.BlockSpec((rows, SWA_Q), lambda n, s: (n, qb)),
                  pl.BlockSpec((rows, SWA_KV), lambda n, s: (n, kb)),
                  pl.BlockSpec((rows, SWA_KV), lambda n, s: (n, vb)),
                  pl.BlockSpec((nseq, WINDOW, SWA_KV), lambda n, s: (n, 0, 0)),
                  pl.BlockSpec((nseq, WINDOW, SWA_KV), lambda n, s: (n, 0, 0)),
                  pl.BlockSpec((DEC_SEQ, LANES), lambda n, s: (0, 0)),
                  pl.BlockSpec((DEC_SEQ, LANES), lambda n, s: (0, 0)),
                  pl.BlockSpec((1, LANES), lambda n, s: (0, 0)),
                  pl.BlockSpec((1, LANES), lambda n, s: (0, 0)),
                  pl.BlockSpec((MXU_DIM, MXU_DIM), lambda n, s: (0, 0))],
        out_specs=[pl.BlockSpec((rows, SWA_Q), lambda n, s: (n, 0)),
                   pl.BlockSpec((nseq, WINDOW, SWA_KV), lambda n, s: (n, 0, 0)),
                   pl.BlockSpec((nseq, WINDOW, SWA_KV), lambda n, s: (n, 0, 0))],
    )
    return pl.pallas_call(
        _swa_sample_body,
        grid_spec=grid_spec,
        out_shape=[jax.ShapeDtypeStruct((t, SWA_Q), BF16),
                   jax.ShapeDtypeStruct(buf_k.shape, F32),
                   jax.ShapeDtypeStruct(buf_v.shape, F32)],
        compiler_params=_params("arbitrary"),
        name="swa_sample",
    )(sinks, proj, proj, proj, buf_k, buf_v, cos_t, sin_t, g_q, g_k, mean_mat)


def _merge_body(oa_ref, ob_ref, wpa_ref, wpb_ref, ga_ref, gb_ref, m_ref):
    ga = _sigmoid(ga_ref[...].astype(F32))
    gb = _sigmoid(gb_ref[...].astype(F32))
    m_ref[...] = (ga * _dot(oa_ref[...], wpa_ref[...]) + gb * _dot(ob_ref[...], wpb_ref[...])).astype(BF16)


def _merge(oa, ob, proj, w_pa, w_pb, tm=1024, tn=512):
    t, d = oa.shape
    ga_blk, gb_blk = COL_BG // tn, (COL_BG + D_MODEL) // tn
    return pl.pallas_call(
        _merge_body,
        grid=(t // tm, d // tn),
        in_specs=[pl.BlockSpec((tm, d), lambda i, j: (i, 0)),
                  pl.BlockSpec((tm, d), lambda i, j: (i, 0)),
                  pl.BlockSpec((d, tn), lambda i, j: (0, j)),
                  pl.BlockSpec((d, tn), lambda i, j: (0, j)),
                  pl.BlockSpec((tm, tn), lambda i, j: (i, ga_blk + j)),
                  pl.BlockSpec((tm, tn), lambda i, j: (i, gb_blk + j))],
        out_specs=pl.BlockSpec((tm, tn), lambda i, j: (i, j)),
        out_shape=jax.ShapeDtypeStruct((t, d), BF16),
        compiler_params=_params("arbitrary", "arbitrary"),
        name="merge",
    )(oa, ob, w_pa, w_pb, proj, proj)


def _oproj_body(m_ref, w_ref, x_ref, gate_ref, o_ref, y_scr):
    y_scr[...] = _dot(m_ref[...], w_ref[...])

    def rows(row0):
        sl = pl.ds(row0, ROW_CHUNK)
        o_ref[sl, :] = x_ref[sl, :] + _mod_rows(gate_ref, row0, ROW_CHUNK) * y_scr[sl, :]
    _row_loop(x_ref.shape[0], rows)


def _oproj(merged, w_o, x, gate, tm=1024, tn=512):
    t, d = x.shape
    return pl.pallas_call(
        _oproj_body,
        grid=(t // tm, d // tn),
        in_specs=[pl.BlockSpec((tm, d), lambda i, j: (i, 0)),
                  pl.BlockSpec((d, tn), lambda i, j: (0, j)),
                  pl.BlockSpec((tm, tn), lambda i, j: (i, j)),
                  _mod_spec(gate, tm, tn, lambda j: j)],
        out_specs=pl.BlockSpec((tm, tn), lambda i, j: (i, j)),
        out_shape=jax.ShapeDtypeStruct(x.shape, F32),
        scratch_shapes=[pltpu.VMEM((tm, tn), F32)],
        compiler_params=_params("arbitrary", "arbitrary"),
        name="oproj",
    )(merged, w_o, x, gate)


def _ffn_body(x_ref, gn_ref, sc_ref, sh_ref, gate_ref, wg_ref, wu_ref, wd_ref, o_ref, h_scr, acc_scr):
    f = pl.program_id(1)

    @pl.when(f == 0)
    def _():
        def rows(row0):
            h_scr[pl.ds(row0, ROW_CHUNK), :] = _modulated_norm_rows(
                x_ref, gn_ref, sc_ref, sh_ref, row0, ROW_CHUNK).astype(BF16)
        _row_loop(x_ref.shape[0], rows)

    h = h_scr[...]
    a = _silu(_dot(h, wg_ref[...])) * _dot(h, wu_ref[...])
    part = _dot(a.astype(BF16), wd_ref[...])

    @pl.when(f == 0)
    def _():
        acc_scr[...] = part

    @pl.when(f > 0)
    def _():
        acc_scr[...] += part

    @pl.when(f == pl.num_programs(1) - 1)
    def _():
        def rows(row0):
            sl = pl.ds(row0, ROW_CHUNK)
            o_ref[sl, :] = x_ref[sl, :] + _mod_rows(gate_ref, row0, ROW_CHUNK) * acc_scr[sl, :]
        _row_loop(x_ref.shape[0], rows)


def _ffn(x, gn, scale, shift, gate, w_gate, w_up, w_down, tm=512, tf=512):
    t, d = x.shape
    dff = w_gate.shape[1]
    zero = lambda j: 0
    return pl.pallas_call(
        _ffn_body,
        grid=(t // tm, dff // tf),
        in_specs=[pl.BlockSpec((tm, d), lambda i, f: (i, 0)),
                  pl.BlockSpec((1, d), lambda i, f: (0, 0)),
                  _mod_spec(scale, tm, d, zero),
                  _mod_spec(shift, tm, d, zero),
                  _mod_spec(gate, tm, d, zero),
                  pl.BlockSpec((d, tf), lambda i, f: (0, f)),
                  pl.BlockSpec((d, tf), lambda i, f: (0, f)),
                  pl.BlockSpec((tf, d), lambda i, f: (f, 0))],
        out_specs=pl.BlockSpec((tm, d), lambda i, f: (i, 0)),
        out_shape=jax.ShapeDtypeStruct(x.shape, F32),
        scratch_shapes=[pltpu.VMEM((tm, d), BF16), pltpu.VMEM((tm, d), F32)],
        compiler_params=_params("arbitrary", "arbitrary"),
        name="ffn",
    )(x, gn, scale, shift, gate, w_gate, w_up, w_down)


def _layer(x, ada, pos, state, buf_k, buf_v, wts):
    (gn_mix, w_main, w_lr, w_gk, b_gk, g_q, g_k, sinks, g_out, w_pa, w_pb, w_o, gn_ffn,
     w_fg, w_fu, w_fd) = wts
    bsz, tlen, d = x.shape
    x2 = x.reshape(bsz * tlen, d)
    shift_m, scale_m, gate_m, shift_f, scale_f, gate_f = (ada[:, i * d:(i + 1) * d] for i in range(6))

    proj, glog = _inproj(x2, gn_mix, scale_m, shift_m, w_main, w_lr, w_gk, b_gk)
    cos_t, sin_t = _rope_tables(pos)
    if state is None:
        oa, s_new = _gla_prompt(proj, glog, g_out)
        s_new = s_new[None]
        ob, k_new, v_new = _swa_prompt(proj, sinks, cos_t, sin_t, g_q, g_k)
    else:
        oa, s_new = _gla_sample(proj, glog, state, g_out)
        ob, k_new, v_new = _swa_sample(proj, buf_k.reshape(bsz, WINDOW, SWA_KV), buf_v.reshape(bsz, WINDOW, SWA_KV),
                                       sinks, cos_t, sin_t, g_q, g_k)
    merged = _merge(oa, ob, proj, w_pa, w_pb)
    x1 = _oproj(merged, w_o, x2, gate_m)
    y = _ffn(x1, gn_ffn, scale_f, shift_f, gate_f, w_fg, w_fu, w_fd)
    return (y.reshape(bsz, tlen, d), s_new,
            k_new.reshape(bsz, WINDOW, SWA_KV_HEADS, SWA_HEAD_DIM),
            v_new.reshape(bsz, WINDOW, SWA_KV_HEADS, SWA_HEAD_DIM))


def kernel(x_prompt, x_sample, c_prompt, c_sample, state_gla, cache_swa_k, cache_swa_v, w_ada, b_ada, g_norm_mix, w_in, w_gk2, b_gk2, g_qnorm, g_knorm, sinks, g_gla_out, w_pa, w_pb, w_o, g_norm_ffn, w_ffn_gate, w_ffn_up, w_ffn_down):
    assert w_in.shape[0] == 1, "single trunk layer"
    pos_p = jnp.arange(SEQ, dtype=jnp.int32)
    pos_s = PAST_LEN + jnp.arange(DEC_SEQ, dtype=jnp.int32)
    l = 0
    w = w_in[l]
    w_main = jnp.concatenate([w[:, :W_IN_GLR], w[:, W_IN_GLR + GLA_RANK:]], axis=1).astype(BF16)
    w_lr = jnp.pad(w[:, W_IN_GLR:W_IN_GLR + GLA_RANK], ((0, 0), (0, LANES - GLA_RANK))).astype(BF16)
    w_gk = jnp.pad(w_gk2[l], ((0, LANES - GLA_RANK), (0, 0))).astype(BF16)
    wts = (g_norm_mix[l].reshape(1, D_MODEL), w_main, w_lr, w_gk, b_gk2[l].reshape(1, GLA_KEY),
           jnp.tile(g_qnorm[l], 2).reshape(1, LANES), jnp.tile(g_knorm[l], 2).reshape(1, LANES),
           sinks[l], g_gla_out[l].reshape(1, GLA_DV),
           w_pa[l].astype(BF16), w_pb[l].astype(BF16), w_o[l].astype(BF16),
           g_norm_ffn[l].reshape(1, D_MODEL),
           w_ffn_gate[l].astype(BF16), w_ffn_up[l].astype(BF16), w_ffn_down[l].astype(BF16))

    n_c = 1 + DEC_BATCH
    pad = (-n_c) % (2 * SUBLANES)
    c_all = jnp.pad(jnp.concatenate([c_prompt, c_sample], axis=0), ((0, pad), (0, 0)))
    ada = _ada(c_all, w_ada[l], b_ada[l].reshape(1, -1))

    yp, sp, kp, vp = _layer(x_prompt, ada[0:1], pos_p, None, None, None, wts)
    ys, ss, ks, vs = _layer(x_sample, ada[1:n_c], pos_s, state_gla[l], cache_swa_k[l], cache_swa_v[l], wts)
    return (yp, ys, sp[None], kp[None], vp[None], ss[None], ks[None], vs[None])
```

```python
import functools

import numpy as np
import jax
import jax.numpy as jnp
from jax import lax
from jax.experimental import pallas as pl
from jax.experimental.pallas import tpu as pltpu

F32 = jnp.float32
BF16 = jnp.bfloat16

D_MODEL = 2048
SEQ = 16384
DEC_BATCH = 128
DEC_SEQ = 8
PAST_LEN = 16384
GLA_HEADS = 4
GLA_DK = 256
GLA_DV = 512
GLA_KEY = GLA_HEADS * GLA_DK
GLA_VAL = GLA_HEADS * GLA_DV
GLA_RANK = 16
GLA_TAU = 16.0
SWA_HEAD_DIM = 64
SWA_HEADS = 32
SWA_KV_HEADS = 4
SWA_GROUP = SWA_HEADS // SWA_KV_HEADS
SWA_Q = SWA_HEADS * SWA_HEAD_DIM
SWA_KV = SWA_KV_HEADS * SWA_HEAD_DIM
WINDOW = 128
ROPE_THETA = 10000.0
D_FF = 5632
EPS = 1e-6
NEG_INF = -1e30

SUBLANES = 8
LANES = 128
MXU_DIM = 256
VMEM_LIMIT_BYTES = 56 * 1024 * 1024

COL_GQ = 0
COL_GK = COL_GQ + GLA_KEY
COL_GV = COL_GK + GLA_KEY
COL_GR = COL_GV + GLA_VAL
COL_SQ = COL_GR + GLA_VAL
COL_SK = COL_SQ + SWA_Q
COL_SV = COL_SK + SWA_KV
COL_BG = COL_SV + SWA_KV
PROJ_COLS = COL_BG + 2 * D_MODEL
W_IN_GLR = COL_GR + GLA_VAL

GLA_CHUNK = 256
GLA_ONE_SIDED_MAX_DECAY = 80.0
GLA_SAMPLE_BATCH = 8
SWA_SAMPLE_BATCH = 4
ROW_CHUNK = 128


def _dot(a, b):
    return jnp.dot(a, b, preferred_element_type=F32)


def _dot_nt(a, b):
    return lax.dot_general(a, b, (((1,), (1,)), ((), ())), preferred_element_type=F32)


def _dot_tn(a, b):
    return lax.dot_general(a, b, (((0,), (0,)), ((), ())), preferred_element_type=F32)


def _sigmoid(x):
    return 1.0 / (1.0 + jnp.exp(-x))


def _silu(x):
    return x * _sigmoid(x)


def _split3(x):
    hi = x.astype(BF16)
    r1 = x - hi.astype(F32)
    mid = r1.astype(BF16)
    lo = (r1 - mid.astype(F32)).astype(BF16)
    return hi, mid, lo


def _params(*sem):
    return pltpu.CompilerParams(dimension_semantics=sem, vmem_limit_bytes=VMEM_LIMIT_BYTES)


def _ada_body(c_ref, w_ref, b_ref, o_ref):
    a = _silu(c_ref[...]).astype(BF16)
    o_ref[...] = _dot(a, w_ref[...].astype(BF16)) + b_ref[...]


def _ada(c_all, w_ada, b_ada, tn=512):
    m, d = c_all.shape
    n = w_ada.shape[1]
    return pl.pallas_call(
        _ada_body,
        grid=(n // tn,),
        in_specs=[pl.BlockSpec((m, d), lambda j: (0, 0)),
                  pl.BlockSpec((d, tn), lambda j: (0, j)),
                  pl.BlockSpec((1, tn), lambda j: (0, j))],
        out_specs=pl.BlockSpec((m, tn), lambda j: (0, j)),
        out_shape=jax.ShapeDtypeStruct((m, n), F32),
        compiler_params=_params("arbitrary"),
        name="ada",
    )(c_all, w_ada, b_ada)


def _mod_spec(mod, tm, width, col):
    if mod.shape[0] == 1:
        return pl.BlockSpec((1, width), lambda i, j: (0, col(j)))
    return pl.BlockSpec((tm // DEC_SEQ, width), lambda i, j: (i, col(j)))


def _mod_rows(ref, row0, nrows):
    if ref.shape[0] == 1:
        return ref[...]
    seq0 = row0 // DEC_SEQ
    return jnp.concatenate(
        [jnp.broadcast_to(ref[pl.ds(seq0 + s, 1), :], (DEC_SEQ, ref.shape[1])) for s in range(nrows // DEC_SEQ)],
        axis=0)


def _modulated_norm_rows(x_ref, gn_ref, sc_ref, sh_ref, row0, nrows):
    x = x_ref[pl.ds(row0, nrows), :]
    ms = jnp.mean(x * x, axis=-1, keepdims=True)
    y = (x * lax.rsqrt(ms + EPS)) * gn_ref[...]
    return y * (1.0 + _mod_rows(sc_ref, row0, nrows)) + _mod_rows(sh_ref, row0, nrows)


def _row_loop(total_rows, body):
    def step(r, carry):
        body(pl.multiple_of(r * ROW_CHUNK, ROW_CHUNK))
        return carry
    lax.fori_loop(0, total_rows // ROW_CHUNK, step, 0)


def _inproj_body(x_ref, gn_ref, sc_ref, sh_ref, wa_ref, wb_ref, wlr_ref, wgk_ref, bgk_ref,
                 proj_ref, glog_ref, h_scr, *, n_first):
    j = pl.program_id(1)

    @pl.when(j == 0)
    def _():
        def rows(row0):
            hb = _modulated_norm_rows(x_ref, gn_ref, sc_ref, sh_ref, row0, ROW_CHUNK).astype(BF16)
            h_scr[pl.ds(row0, ROW_CHUNK), :] = hb
            glr = _dot(hb, wlr_ref[...])
            z = _dot(glr.astype(BF16), wgk_ref[...]) + bgk_ref[...]
            log_sig = jnp.minimum(z, 0.0) - jnp.log1p(jnp.exp(-jnp.abs(z)))
            glog_ref[pl.ds(row0, ROW_CHUNK), :] = log_sig * (1.0 / GLA_TAU)
        _row_loop(x_ref.shape[0], rows)

    @pl.when(j < n_first)
    def _():
        proj_ref[...] = _dot(h_scr[...], wa_ref[...].astype(BF16)).astype(BF16)

    @pl.when(j >= n_first)
    def _():
        proj_ref[...] = _dot(h_scr[...], wb_ref[...]).astype(BF16)


def _inproj(x, gn, scale, shift, w_full, w_tail, w_lr, w_gk, b_gk, tm=1024, tn=512):
    t, d = x.shape
    n_first = W_IN_GLR // tn
    n = W_IN_GLR + w_tail.shape[1]
    zero = lambda j: 0
    return pl.pallas_call(
        functools.partial(_inproj_body, n_first=n_first),
        grid=(t // tm, n // tn),
        in_specs=[pl.BlockSpec((tm, d), lambda i, j: (i, 0)),
                  pl.BlockSpec((1, d), lambda i, j: (0, 0)),
                  _mod_spec(scale, tm, d, zero),
                  _mod_spec(shift, tm, d, zero),
                  pl.BlockSpec((d, tn), lambda i, j: (0, jnp.minimum(j, n_first - 1))),
                  pl.BlockSpec((d, tn), lambda i, j: (0, jnp.maximum(j - n_first, 0))),
                  pl.BlockSpec(w_lr.shape, lambda i, j: (0, 0)),
                  pl.BlockSpec(w_gk.shape, lambda i, j: (0, 0)),
                  pl.BlockSpec(b_gk.shape, lambda i, j: (0, 0))],
        out_specs=[pl.BlockSpec((tm, tn), lambda i, j: (i, j)),
                   pl.BlockSpec((tm, GLA_KEY), lambda i, j: (i, 0))],
        out_shape=[jax.ShapeDtypeStruct((t, n), BF16),
                   jax.ShapeDtypeStruct((t, GLA_KEY), F32)],
        scratch_shapes=[pltpu.VMEM((tm, d), BF16)],
        compiler_params=_params("arbitrary", "arbitrary"),
        name="inproj",
    )(x, gn, scale, shift, w_full, w_tail, w_lr, w_gk, b_gk)


def _gla_level_table(c, group):
    n_levels = int(np.log2(group))
    i = np.arange(c)[:, None]
    j = np.arange(c)[None, :]
    x = np.bitwise_xor(i, j)
    lvl = np.floor(np.log2(np.maximum(x, 1))).astype(np.int32)
    lvl = np.where(i == j, n_levels, lvl)
    valid = (i >= j) & (i // group == j // group)
    return np.where(valid, lvl, -1).astype(np.int32), n_levels


def _gla_tril(c, group):
    i = np.arange(c)[:, None]
    j = np.arange(c)[None, :]
    return ((i >= j) & (i // group == j // group)).astype(np.float32)


def _level_reference(b_scr, rows, level):
    s = 2 ** (level + 1)
    width = b_scr.shape[1]
    pieces = []
    if s >= SUBLANES:
        for blk in range(rows // s):
            mid = blk * s + s // 2
            pieces.append(jnp.broadcast_to(b_scr[mid:mid + 1, :], (s, width)))
    else:
        p = lax.broadcasted_iota(jnp.int32, (SUBLANES, width), 0)
        for tile in range(rows // SUBLANES):
            base = tile * SUBLANES
            mids = [base + q * s + s // 2 for q in range(SUBLANES // s)]
            r = jnp.broadcast_to(b_scr[mids[-1]:mids[-1] + 1, :], (SUBLANES, width))
            for q in range(SUBLANES // s - 2, -1, -1):
                row = jnp.broadcast_to(b_scr[mids[q]:mids[q] + 1, :], (SUBLANES, width))
                r = jnp.where(p < (q + 1) * s, row, r)
            pieces.append(r)
    return jnp.concatenate(pieces, axis=0) if len(pieces) > 1 else pieces[0]


def _gla_intra(q, k, k_bf, b, b_scr, lvl, n_levels):
    rows = q.shape[0]
    a = jnp.where(lvl == n_levels, _dot_nt(q.astype(BF16), k_bf), 0.0)
    for level in range(n_levels):
        f = jnp.exp(-jnp.abs(b - _level_reference(b_scr, rows, level)))
        p = _dot_nt((q * f).astype(BF16), (k * f).astype(BF16))
        a = jnp.where(lvl == level, p, a)
    return a


def _gla_cumsum(g, tril_bf):
    hi, mid, lo = _split3(g)
    return _dot(tril_bf, hi) + _dot(tril_bf, mid) + _dot(tril_bf, lo)


def _gla_out_gate(o, gout_ref, gr_ref):
    ms = jnp.mean(o * o, axis=-1, keepdims=True)
    y = (o * lax.rsqrt(ms + EPS)) * gout_ref[...]
    return (y * _silu(gr_ref[...].astype(F32))).astype(BF16)


def _gla_prompt_body(q_ref, k_ref, v_ref, gr_ref, g_ref, tril_ref, lvl_ref, gout_ref,
                     oa_ref, st_ref, st_scr, b_scr, a_scr, *, n_levels):
    c = pl.program_id(1)
    rows = q_ref.shape[0]

    @pl.when(c == 0)
    def _():
        st_scr[...] = jnp.zeros_like(st_scr)

    b = _gla_cumsum(g_ref[...], tril_ref[...])
    b_scr[...] = b
    b_end = b_scr[rows - 1:rows, :]
    q = q_ref[...].astype(F32) * (GLA_DK ** -0.5)
    k_bf = k_ref[...]
    k = k_bf.astype(F32)
    v = v_ref[...]
    st = st_scr[...]
    qd = (q * jnp.exp(b)).astype(BF16)
    o = _dot_nt(qd, st.astype(BF16))

    mild = jnp.min(b_end) > -GLA_ONE_SIDED_MAX_DECAY

    @pl.when(mild)
    def _():
        a_scr[...] = jnp.where(lvl_ref[...] >= 0, _dot_nt(qd, (k * jnp.exp(-b)).astype(BF16)), 0.0)

    @pl.when(jnp.logical_not(mild))
    def _():
        a_scr[...] = _gla_intra(q, k, k_bf, b, b_scr, lvl_ref[...], n_levels)

    o = o + _dot(a_scr[...].astype(BF16), v)
    k_dec = (k * jnp.exp(b_end - b)).astype(BF16)
    st_scr[...] = st * jnp.exp(b_end) + _dot_tn(v, k_dec)
    oa_ref[...] = _gla_out_gate(o, gout_ref, gr_ref)

    @pl.when(c == pl.num_programs(1) - 1)
    def _():
        st_ref[0] = st_scr[...].T


def _gla_prompt(proj, glog, g_out):
    t = proj.shape[0]
    c = GLA_CHUNK
    lvl, n_levels = _gla_level_table(c, c)
    tril = jnp.asarray(_gla_tril(c, c), BF16)
    lvl = jnp.asarray(lvl)
    qb, kb, vb, rb = COL_GQ // GLA_DK, COL_GK // GLA_DK, COL_GV // GLA_DV, COL_GR // GLA_DV
    return pl.pallas_call(
        functools.partial(_gla_prompt_body, n_levels=n_levels),
        grid=(GLA_HEADS, t // c),
        in_specs=[pl.BlockSpec((c, GLA_DK), lambda h, i: (i, qb + h)),
                  pl.BlockSpec((c, GLA_DK), lambda h, i: (i, kb + h)),
                  pl.BlockSpec((c, GLA_DV), lambda h, i: (i, vb + h)),
                  pl.BlockSpec((c, GLA_DV), lambda h, i: (i, rb + h)),
                  pl.BlockSpec((c, GLA_DK), lambda h, i: (i, h)),
                  pl.BlockSpec((c, c), lambda h, i: (0, 0)),
                  pl.BlockSpec((c, c), lambda h, i: (0, 0)),
                  pl.BlockSpec((1, GLA_DV), lambda h, i: (0, 0))],
        out_specs=[pl.BlockSpec((c, GLA_DV), lambda h, i: (i, h)),
                   pl.BlockSpec((1, GLA_DK, GLA_DV), lambda h, i: (h, 0, 0))],
        out_shape=[jax.ShapeDtypeStruct((t, GLA_VAL), BF16),
                   jax.ShapeDtypeStruct((GLA_HEADS, GLA_DK, GLA_DV), F32)],
        scratch_shapes=[pltpu.VMEM((GLA_DV, GLA_DK), F32), pltpu.VMEM((c, GLA_DK), F32),
                        pltpu.VMEM((c, c), F32)],
        compiler_params=_params("arbitrary", "arbitrary"),
        name="gla_prompt",
    )(proj, proj, proj, proj, glog, tril, lvl, g_out)


def _gla_sample_body(q_ref, k_ref, v_ref, gr_ref, g_ref, s0_ref, tril_ref, lvl_ref, gout_ref,
                     oa_ref, s1_ref, b_scr, *, n_levels):
    rows = q_ref.shape[0]
    nseq = rows // DEC_SEQ
    b = _gla_cumsum(g_ref[...], tril_ref[...])
    b_scr[...] = b
    b_end = jnp.concatenate(
        [jnp.broadcast_to(b_scr[(s + 1) * DEC_SEQ - 1:(s + 1) * DEC_SEQ, :], (DEC_SEQ, GLA_DK))
         for s in range(nseq)], axis=0)
    q = q_ref[...].astype(F32) * (GLA_DK ** -0.5)
    k_bf = k_ref[...]
    k = k_bf.astype(F32)
    v = v_ref[...]
    qd = (q * jnp.exp(b)).astype(BF16)
    a = _gla_intra(q, k, k_bf, b, b_scr, lvl_ref[...], n_levels)
    o_intra = _dot(a.astype(BF16), v)

    stacked = jnp.concatenate([k * jnp.exp(b_end - b), jnp.exp(b_end)], axis=0)
    stacked_t = stacked.T
    kd_t = stacked_t[:, :rows]
    lane = lax.broadcasted_iota(jnp.int32, kd_t.shape, 1)
    outs = []
    for s in range(nseq):
        s0 = s0_ref[s, 0]
        outs.append(_dot(qd[s * DEC_SEQ:(s + 1) * DEC_SEQ, :], s0.astype(BF16)))
        in_seq = (lane >= s * DEC_SEQ) & (lane < (s + 1) * DEC_SEQ)
        kd_s = jnp.where(in_seq, kd_t, 0.0).astype(BF16)
        col = rows + s * DEC_SEQ
        decay = stacked_t[:, col:col + 1]
        s1_ref[s, 0] = s0 * decay + _dot(kd_s, v)
    o = jnp.concatenate(outs, axis=0) + o_intra
    oa_ref[...] = _gla_out_gate(o, gout_ref, gr_ref)


def _gla_sample(proj, glog, state, g_out):
    t = proj.shape[0]
    nseq = GLA_SAMPLE_BATCH
    rows = nseq * DEC_SEQ
    lvl, n_levels = _gla_level_table(rows, DEC_SEQ)
    tril = jnp.asarray(_gla_tril(rows, DEC_SEQ), BF16)
    lvl = jnp.asarray(lvl)
    qb, kb, vb, rb = COL_GQ // GLA_DK, COL_GK // GLA_DK, COL_GV // GLA_DV, COL_GR // GLA_DV
    return pl.pallas_call(
        functools.partial(_gla_sample_body, n_levels=n_levels),
        grid=(t // rows, GLA_HEADS),
        in_specs=[pl.BlockSpec((rows, GLA_DK), lambda i, h: (i, qb + h)),
                  pl.BlockSpec((rows, GLA_DK), lambda i, h: (i, kb + h)),
                  pl.BlockSpec((rows, GLA_DV), lambda i, h: (i, vb + h)),
                  pl.BlockSpec((rows, GLA_DV), lambda i, h: (i, rb + h)),
                  pl.BlockSpec((rows, GLA_DK), lambda i, h: (i, h)),
                  pl.BlockSpec((nseq, 1, GLA_DK, GLA_DV), lambda i, h: (i, h, 0, 0)),
                  pl.BlockSpec((rows, rows), lambda i, h: (0, 0)),
                  pl.BlockSpec((rows, rows), lambda i, h: (0, 0)),
                  pl.BlockSpec((1, GLA_DV), lambda i, h: (0, 0))],
        out_specs=[pl.BlockSpec((rows, GLA_DV), lambda i, h: (i, h)),
                   pl.BlockSpec((nseq, 1, GLA_DK, GLA_DV), lambda i, h: (i, h, 0, 0))],
        out_shape=[jax.ShapeDtypeStruct((t, GLA_VAL), BF16),
                   jax.ShapeDtypeStruct(state.shape, F32)],
        scratch_shapes=[pltpu.VMEM((rows, GLA_DK), F32)],
        compiler_params=_params("arbitrary", "arbitrary"),
        name="gla_sample",
    )(proj, proj, proj, proj, glog, state, tril, lvl, g_out)


def _rope_tables(pos):
    half = SWA_HEAD_DIM // 2
    inv_freq = ROPE_THETA ** (-(jnp.arange(half, dtype=F32) * 2.0) / SWA_HEAD_DIM)
    ang = pos.astype(F32)[:, None] * inv_freq[None, :]
    cos, sin = jnp.cos(ang), jnp.sin(ang)
    cos_t = jnp.concatenate([cos, cos, cos, cos], axis=-1)
    sin_t = jnp.concatenate([-sin, sin, -sin, sin], axis=-1)
    return cos_t, sin_t


def _head_mean_matrix():
    i = np.arange(MXU_DIM)[:, None] // SWA_HEAD_DIM
    j = np.arange(MXU_DIM)[None, :] // SWA_HEAD_DIM
    return (i == j).astype(np.float32) / SWA_HEAD_DIM


def _qk_norm_rope(x, gain, cos_t, sin_t, mean_mat):
    rows, width = x.shape
    lane = lax.broadcasted_iota(jnp.int32, (rows, LANES), 1)
    first_half = (lane % SWA_HEAD_DIM) < (SWA_HEAD_DIM // 2)
    shift = SWA_HEAD_DIM // 2
    sq = (x * x).astype(BF16)
    tiles = []
    for c in range(width // MXU_DIM):
        ms = _dot(sq[:, c * MXU_DIM:(c + 1) * MXU_DIM], mean_mat)
        for t in range(MXU_DIM // LANES):
            lo = c * MXU_DIM + t * LANES
            y = (x[:, lo:lo + LANES] * lax.rsqrt(ms[:, t * LANES:(t + 1) * LANES] + EPS)) * gain
            rot = jnp.where(first_half, pltpu.roll(y, LANES - shift, 1), pltpu.roll(y, shift, 1))
            tiles.append(y * cos_t + rot * sin_t)
    return tiles


def _attend(q_tiles, k_tile, v_tile, parity_of_kv, valid, sinks_ref, kv_head):
    rows = q_tiles[0].shape[0]
    lane = lax.broadcasted_iota(jnp.int32, k_tile.shape, 1)
    low = lane < SWA_HEAD_DIM

    def place(x, want_low):
        have_low = parity_of_kv == 0
        src = x if have_low == want_low else pltpu.roll(x, SWA_HEAD_DIM, 1)
        return jnp.where(low if want_low else ~low, src, 0.0).astype(BF16)

    q_all = jnp.concatenate(q_tiles, axis=0).astype(BF16)
    npairs = len(q_tiles)
    outs = [None] * npairs
    for parity in range(2):
        kk = place(k_tile, parity == 0)
        vv = place(v_tile, parity == 0)
        s_all = _dot_nt(q_all, kk)
        p_list, inv_list = [], []
        for pair in range(npairs):
            sink = sinks_ref[kv_head * SWA_GROUP + 2 * pair + parity]
            s = jnp.where(valid, s_all[pair * rows:(pair + 1) * rows, :], NEG_INF)
            m = jnp.maximum(jnp.max(s, axis=-1, keepdims=True), sink)
            p = jnp.exp(s - m)
            denom = jnp.sum(p, axis=-1, keepdims=True) + jnp.exp(sink - m)
            p_list.append(p.astype(BF16))
            inv_list.append(1.0 / denom)
        pv = _dot(jnp.concatenate(p_list, axis=0), vv)
        for pair in range(npairs):
            contrib = pv[pair * rows:(pair + 1) * rows, :] * inv_list[pair]
            outs[pair] = contrib if outs[pair] is None else outs[pair] + contrib
    return outs


def _attend_all_heads(q_tiles, k_all, v_all, valid, sinks_ref, store):
    pairs = SWA_GROUP // 2
    for g in range(SWA_KV_HEADS):
        lo = (g // 2) * LANES
        outs = _attend(q_tiles[g * pairs:(g + 1) * pairs], k_all[:, lo:lo + LANES], v_all[:, lo:lo + LANES],
                       g % 2, valid, sinks_ref, g)
        for p, o in enumerate(outs):
            store(g * pairs + p, o.astype(BF16))


def _swa_prompt_body(sinks_ref, q_ref, k_ref, v_ref, cb_ref, sb_ref, cr_ref, sr_ref, crs_ref, srs_ref,
                     gq_ref, gk_ref, mm_ref, ob_ref, knew_ref, vnew_ref, kprev_scr, vprev_scr):
    n = pl.program_id(0)
    blk = q_ref.shape[0]

    @pl.when(n == 0)
    def _():
        kprev_scr[...] = jnp.zeros_like(kprev_scr)
        vprev_scr[...] = jnp.zeros_like(vprev_scr)

    cb, sb = cb_ref[pl.ds(n, 1), :], sb_ref[pl.ds(n, 1), :]
    cos_t = cb * cr_ref[...] - sb * sr_ref[...]
    sin_t = sb * crs_ref[...] + cb * srs_ref[...]
    mean_mat = mm_ref[...]
    q_tiles = [t * (SWA_HEAD_DIM ** -0.5)
               for t in _qk_norm_rope(q_ref[...].astype(F32), gq_ref[...], cos_t, sin_t, mean_mat)]
    kn = jnp.concatenate(_qk_norm_rope(k_ref[...].astype(F32), gk_ref[...], cos_t, sin_t, mean_mat), axis=1)
    vn = v_ref[...].astype(F32)
    knew_ref[...] = kn
    vnew_ref[...] = vn
    k_all = jnp.concatenate([kprev_scr[...], kn], axis=0)
    v_all = jnp.concatenate([vprev_scr[...], vn], axis=0)

    qi = lax.broadcasted_iota(jnp.int32, (blk, 2 * blk), 0)
    kj = lax.broadcasted_iota(jnp.int32, (blk, 2 * blk), 1)
    dist = blk + qi - kj
    valid = (dist >= 0) & (dist <= WINDOW) & ((n - 1) * blk + kj >= 0)

    def store(tile, value):
        ob_ref[:, tile * LANES:(tile + 1) * LANES] = value
    _attend_all_heads(q_tiles, k_all, v_all, valid, sinks_ref, store)

    kprev_scr[...] = kn
    vprev_scr[...] = vn


def _rope_block_tables(nblocks, blk):
    half = SWA_HEAD_DIM // 2
    inv_freq = ROPE_THETA ** (-(jnp.arange(half, dtype=F32) * 2.0) / SWA_HEAD_DIM)
    lane_freq = jnp.tile(inv_freq, LANES // half)[None, :]
    sign = jnp.tile(jnp.concatenate([-jnp.ones((half,), F32), jnp.ones((half,), F32)]), LANES // SWA_HEAD_DIM)[None, :]
    ang_b = (jnp.arange(nblocks, dtype=jnp.int32) * blk).astype(F32)[:, None] * lane_freq
    ang_r = jnp.arange(blk, dtype=jnp.int32).astype(F32)[:, None] * lane_freq
    cr, sr = jnp.cos(ang_r), jnp.sin(ang_r)
    return jnp.cos(ang_b), jnp.sin(ang_b), cr, sr, sign * cr, sign * sr


def _swa_prompt(proj, sinks, g_q, g_k):
    t = proj.shape[0]
    blk = WINDOW
    nblocks = t // blk
    mean_mat = jnp.asarray(_head_mean_matrix(), BF16)
    tables = _rope_block_tables(nblocks, blk)
    qb, kb, vb = COL_SQ // SWA_Q, COL_SK // SWA_KV, COL_SV // SWA_KV
    whole = lambda n, s: (0, 0)
    grid_spec = pltpu.PrefetchScalarGridSpec(
        num_scalar_prefetch=1,
        grid=(nblocks,),
        in_specs=[pl.BlockSpec((blk, SWA_Q), lambda n, s: (n, qb)),
                  pl.BlockSpec((blk, SWA_KV), lambda n, s: (n, kb)),
                  pl.BlockSpec((blk, SWA_KV), lambda n, s: (n, vb)),
                  pl.BlockSpec((nblocks, LANES), whole),
                  pl.BlockSpec((nblocks, LANES), whole),
                  pl.BlockSpec((blk, LANES), whole),
                  pl.BlockSpec((blk, LANES), whole),
                  pl.BlockSpec((blk, LANES), whole),
                  pl.BlockSpec((blk, LANES), whole),
                  pl.BlockSpec((1, LANES), lambda n, s: (0, 0)),
                  pl.BlockSpec((1, LANES), lambda n, s: (0, 0)),
                  pl.BlockSpec((MXU_DIM, MXU_DIM), lambda n, s: (0, 0))],
        out_specs=[pl.BlockSpec((blk, SWA_Q), lambda n, s: (n, 0)),
                   pl.BlockSpec((blk, SWA_KV), lambda n, s: (0, 0)),
                   pl.BlockSpec((blk, SWA_KV), lambda n, s: (0, 0))],
        scratch_shapes=[pltpu.VMEM((blk, SWA_KV), F32), pltpu.VMEM((blk, SWA_KV), F32)],
    )
    return pl.pallas_call(
        _swa_prompt_body,
        grid_spec=grid_spec,
        out_shape=[jax.ShapeDtypeStruct((t, SWA_Q), BF16),
                   jax.ShapeDtypeStruct((blk, SWA_KV), F32),
                   jax.ShapeDtypeStruct((blk, SWA_KV), F32)],
        compiler_params=_params("arbitrary"),
        name="swa_prompt",
    )(sinks, proj, proj, proj, *tables, g_q, g_k, mean_mat)


def _swa_sample_body(sinks_ref, q_ref, k_ref, v_ref, bk_ref, bv_ref, cos_ref, sin_ref, gq_ref, gk_ref, mm_ref,
                     ob_ref, kout_ref, vout_ref):
    nseq = bk_ref.shape[0]
    cos_t = jnp.concatenate([cos_ref[...]] * nseq, axis=0)
    sin_t = jnp.concatenate([sin_ref[...]] * nseq, axis=0)
    mean_mat = mm_ref[...]
    q_tiles = [t * (SWA_HEAD_DIM ** -0.5)
               for t in _qk_norm_rope(q_ref[...].astype(F32), gq_ref[...], cos_t, sin_t, mean_mat)]
    kn = jnp.concatenate(_qk_norm_rope(k_ref[...].astype(F32), gk_ref[...], cos_t, sin_t, mean_mat), axis=1)
    vn = v_ref[...].astype(F32)

    keys = 2 * WINDOW
    qi = lax.broadcasted_iota(jnp.int32, (DEC_SEQ, keys), 0)
    kj = lax.broadcasted_iota(jnp.int32, (DEC_SEQ, keys), 1)
    dist = WINDOW + qi - kj
    valid = (dist >= 0) & (dist <= WINDOW)
    valid = jnp.concatenate([valid] * nseq, axis=0)
    pad = jnp.zeros((keys - WINDOW - DEC_SEQ, SWA_KV), F32)
    rows = nseq * DEC_SEQ

    k_parts, v_parts = [], []
    for s in range(nseq):
        rsl = slice(s * DEC_SEQ, (s + 1) * DEC_SEQ)
        bk, bv = bk_ref[s], bv_ref[s]
        kout_ref[s, :WINDOW - DEC_SEQ, :] = bk[DEC_SEQ:, :]
        kout_ref[s, WINDOW - DEC_SEQ:, :] = kn[rsl, :]
        vout_ref[s, :WINDOW - DEC_SEQ, :] = bv[DEC_SEQ:, :]
        vout_ref[s, WINDOW - DEC_SEQ:, :] = vn[rsl, :]
        k_parts += [bk, kn[rsl, :], pad]
        v_parts += [bv, vn[rsl, :], pad]
    k_stack = jnp.concatenate(k_parts, axis=0)
    v_stack = jnp.concatenate(v_parts, axis=0)

    low = lax.broadcasted_iota(jnp.int32, (nseq * keys, LANES), 1) < SWA_HEAD_DIM
    pairs = SWA_GROUP // 2
    zero_blk = jnp.zeros((DEC_SEQ, keys), F32)
    for g in range(SWA_KV_HEADS):
        lo = (g // 2) * LANES
        kv_low = g % 2 == 0
        keep = low if kv_low else jnp.logical_not(low)
        kk = jnp.where(keep, k_stack[:, lo:lo + LANES], 0.0).astype(BF16)
        vv = jnp.where(keep, v_stack[:, lo:lo + LANES], 0.0).astype(BF16)
        q_g = jnp.concatenate(q_tiles[g * pairs:(g + 1) * pairs], axis=0)
        out = None
        for parity in range(2):
            aligned = (parity == 0) == kv_low
            q_use = q_g if aligned else pltpu.roll(q_g, SWA_HEAD_DIM, 1)
            s_all = _dot_nt(q_use.astype(BF16), kk)
            p_rows, inv = [], []
            for pair in range(pairs):
                sink = sinks_ref[g * SWA_GROUP + 2 * pair + parity]
                sc = jnp.concatenate(
                    [s_all[pair * rows + s * DEC_SEQ: pair * rows + (s + 1) * DEC_SEQ, s * keys:(s + 1) * keys]
                     for s in range(nseq)], axis=0)
                sc = jnp.where(valid, sc, NEG_INF)
                m = jnp.maximum(jnp.max(sc, axis=-1, keepdims=True), sink)
                p = jnp.exp(sc - m)
                inv.append(1.0 / (jnp.sum(p, axis=-1, keepdims=True) + jnp.exp(sink - m)))
                for s in range(nseq):
                    blocks = [p[s * DEC_SEQ:(s + 1) * DEC_SEQ, :] if s2 == s else zero_blk for s2 in range(nseq)]
                    p_rows.append(jnp.concatenate(blocks, axis=1))
            pv = _dot(jnp.concatenate(p_rows, axis=0).astype(BF16), vv)
            pv = jnp.concatenate([pv[pair * rows:(pair + 1) * rows, :] * inv[pair] for pair in range(pairs)], axis=0)
            if not aligned:
                pv = pltpu.roll(pv, SWA_HEAD_DIM, 1)
            out = pv if out is None else out + pv
        for pair in range(pairs):
            tile = g * pairs + pair
            ob_ref[:, tile * LANES:(tile + 1) * LANES] = out[pair * rows:(pair + 1) * rows, :].astype(BF16)


def _swa_sample(proj, buf_k, buf_v, sinks, cos_t, sin_t, g_q, g_k):
    t = proj.shape[0]
    nseq = SWA_SAMPLE_BATCH
    rows = nseq * DEC_SEQ
    mean_mat = jnp.asarray(_head_mean_matrix(), BF16)
    qb, kb, vb = COL_SQ // SWA_Q, COL_SK // SWA_KV, COL_SV // SWA_KV
    grid_spec = pltpu.PrefetchScalarGridSpec(
        num_scalar_prefetch=1,
        grid=(t // rows,),
        in_specs=[pl.BlockSpec((rows, SWA_Q), lambda n, s: (n, qb)),
                  pl.BlockSpec((rows, SWA_KV), lambda n, s: (n, kb)),
                  pl.BlockSpec((rows, SWA_KV), lambda n, s: (n, vb)),
                  pl.BlockSpec((nseq, WINDOW, SWA_KV), lambda n, s: (n, 0, 0)),
                  pl.BlockSpec((nseq, WINDOW, SWA_KV), lambda n, s: (n, 0, 0)),
                  pl.BlockSpec((DEC_SEQ, LANES), lambda n, s: (0, 0)),
                  pl.BlockSpec((DEC_SEQ, LANES), lambda n, s: (0, 0)),
                  pl.BlockSpec((1, LANES), lambda n, s: (0, 0)),
                  pl.BlockSpec((1, LANES), lambda n, s: (0, 0)),
                  pl.BlockSpec((MXU_DIM, MXU_DIM), lambda n, s: (0, 0))],
        out_specs=[pl.BlockSpec((rows, SWA_Q), lambda n, s: (n, 0)),
                   pl.BlockSpec((nseq, WINDOW, SWA_KV), lambda n, s: (n, 0, 0)),
                   pl.BlockSpec((nseq, WINDOW, SWA_KV), lambda n, s: (n, 0, 0))],
    )
    return pl.pallas_call(
        _swa_sample_body,
        grid_spec=grid_spec,
        out_shape=[jax.ShapeDtypeStruct((t, SWA_Q), BF16),
                   jax.ShapeDtypeStruct(buf_k.shape, F32),
                   jax.ShapeDtypeStruct(buf_v.shape, F32)],
        compiler_params=_params("arbitrary"),
        name="swa_sample",
    )(sinks, proj, proj, proj, buf_k, buf_v, cos_t, sin_t, g_q, g_k, mean_mat)


def _merge_body(oa_ref, ob_ref, wpa_ref, wpb_ref, ga_ref, gb_ref, m_ref):
    ga = _sigmoid(ga_ref[...].astype(F32))
    gb = _sigmoid(gb_ref[...].astype(F32))
    m_ref[...] = (ga * _dot(oa_ref[...], wpa_ref[...].astype(BF16))
                  + gb * _dot(ob_ref[...], wpb_ref[...].astype(BF16))).astype(BF16)


def _merge(oa, ob, proj, w_pa, w_pb, tm=1024, tn=512):
    t, d = oa.shape
    ga_blk, gb_blk = COL_BG // tn, (COL_BG + D_MODEL) // tn
    return pl.pallas_call(
        _merge_body,
        grid=(t // tm, d // tn),
        in_specs=[pl.BlockSpec((tm, d), lambda i, j: (i, 0)),
                  pl.BlockSpec((tm, d), lambda i, j: (i, 0)),
                  pl.BlockSpec((d, tn), lambda i, j: (0, j)),
                  pl.BlockSpec((d, tn), lambda i, j: (0, j)),
                  pl.BlockSpec((tm, tn), lambda i, j: (i, ga_blk + j)),
                  pl.BlockSpec((tm, tn), lambda i, j: (i, gb_blk + j))],
        out_specs=pl.BlockSpec((tm, tn), lambda i, j: (i, j)),
        out_shape=jax.ShapeDtypeStruct((t, d), BF16),
        compiler_params=_params("arbitrary", "arbitrary"),
        name="merge",
    )(oa, ob, w_pa, w_pb, proj, proj)


def _oproj_body(m_ref, w_ref, x_ref, gate_ref, o_ref, y_scr):
    y_scr[...] = _dot(m_ref[...], w_ref[...].astype(BF16))

    def rows(row0):
        sl = pl.ds(row0, ROW_CHUNK)
        o_ref[sl, :] = x_ref[sl, :] + _mod_rows(gate_ref, row0, ROW_CHUNK) * y_scr[sl, :]
    _row_loop(x_ref.shape[0], rows)


def _oproj(merged, w_o, x, gate, tm=1024, tn=512):
    t, d = x.shape
    return pl.pallas_call(
        _oproj_body,
        grid=(t // tm, d // tn),
        in_specs=[pl.BlockSpec((tm, d), lambda i, j: (i, 0)),
                  pl.BlockSpec((d, tn), lambda i, j: (0, j)),
                  pl.BlockSpec((tm, tn), lambda i, j: (i, j)),
                  _mod_spec(gate, tm, tn, lambda j: j)],
        out_specs=pl.BlockSpec((tm, tn), lambda i, j: (i, j)),
        out_shape=jax.ShapeDtypeStruct(x.shape, F32),
        scratch_shapes=[pltpu.VMEM((tm, tn), F32)],
        compiler_params=_params("arbitrary", "arbitrary"),
        name="oproj",
    )(merged, w_o, x, gate)


def _ffn_body(x_ref, gn_ref, sc_ref, sh_ref, gate_ref, wg_ref, wu_ref, wd_ref, o_ref, h_scr, acc_scr):
    f = pl.program_id(1)

    @pl.when(f == 0)
    def _():
        def rows(row0):
            h_scr[pl.ds(row0, ROW_CHUNK), :] = _modulated_norm_rows(
                x_ref, gn_ref, sc_ref, sh_ref, row0, ROW_CHUNK).astype(BF16)
            acc_scr[pl.ds(row0, ROW_CHUNK), :] = jnp.zeros((ROW_CHUNK, acc_scr.shape[1]), F32)
        _row_loop(x_ref.shape[0], rows)

    h = h_scr[...]
    a = _silu(_dot(h, wg_ref[...])) * _dot(h, wu_ref[...])
    acc_scr[...] += _dot(a.astype(BF16), wd_ref[...])

    @pl.when(f == pl.num_programs(1) - 1)
    def _():
        def rows(row0):
            sl = pl.ds(row0, ROW_CHUNK)
            o_ref[sl, :] = x_ref[sl, :] + _mod_rows(gate_ref, row0, ROW_CHUNK) * acc_scr[sl, :]
        _row_loop(x_ref.shape[0], rows)


def _ffn(x, gn, scale, shift, gate, w_gate, w_up, w_down, tm=512, tf=512):
    t, d = x.shape
    dff = w_gate.shape[1]
    zero = lambda j: 0
    return pl.pallas_call(
        _ffn_body,
        grid=(t // tm, dff // tf),
        in_specs=[pl.BlockSpec((tm, d), lambda i, f: (i, 0)),
                  pl.BlockSpec((1, d), lambda i, f: (0, 0)),
                  _mod_spec(scale, tm, d, zero),
                  _mod_spec(shift, tm, d, zero),
                  _mod_spec(gate, tm, d, zero),
                  pl.BlockSpec((d, tf), lambda i, f: (0, f)),
                  pl.BlockSpec((d, tf), lambda i, f: (0, f)),
                  pl.BlockSpec((tf, d), lambda i, f: (f, 0))],
        out_specs=pl.BlockSpec((tm, d), lambda i, f: (i, 0)),
        out_shape=jax.ShapeDtypeStruct(x.shape, F32),
        scratch_shapes=[pltpu.VMEM((tm, d), BF16), pltpu.VMEM((tm, d), F32)],
        compiler_params=_params("arbitrary", "arbitrary"),
        name="ffn",
    )(x, gn, scale, shift, gate, w_gate, w_up, w_down)


def _layer(x, ada, pos, state, buf_k, buf_v, wts):
    (gn_mix, w_full, w_tail, w_lr, w_gk, b_gk, g_q, g_k, sinks, g_out, w_pa, w_pb, w_o, gn_ffn,
     w_fg, w_fu, w_fd) = wts
    bsz, tlen, d = x.shape
    x2 = x.reshape(bsz * tlen, d)
    shift_m, scale_m, gate_m, shift_f, scale_f, gate_f = (ada[:, i * d:(i + 1) * d] for i in range(6))

    proj, glog = _inproj(x2, gn_mix, scale_m, shift_m, w_full, w_tail, w_lr, w_gk, b_gk)
    if state is None:
        oa, s_new = _gla_prompt(proj, glog, g_out)
        s_new = s_new[None]
        ob, k_new, v_new = _swa_prompt(proj, sinks, g_q, g_k)
    else:
        cos_t, sin_t = _rope_tables(pos)
        oa, s_new = _gla_sample(proj, glog, state, g_out)
        ob, k_new, v_new = _swa_sample(proj, buf_k.reshape(bsz, WINDOW, SWA_KV), buf_v.reshape(bsz, WINDOW, SWA_KV),
                                       sinks, cos_t, sin_t, g_q, g_k)
    merged = _merge(oa, ob, proj, w_pa, w_pb)
    x1 = _oproj(merged, w_o, x2, gate_m)
    y = _ffn(x1, gn_ffn, scale_f, shift_f, gate_f, w_fg, w_fu, w_fd)
    return (y.reshape(bsz, tlen, d), s_new,
            k_new.reshape(bsz, WINDOW, SWA_KV_HEADS, SWA_HEAD_DIM),
            v_new.reshape(bsz, WINDOW, SWA_KV_HEADS, SWA_HEAD_DIM))


def kernel(x_prompt, x_sample, c_prompt, c_sample, state_gla, cache_swa_k, cache_swa_v, w_ada, b_ada, g_norm_mix, w_in, w_gk2, b_gk2, g_qnorm, g_knorm, sinks, g_gla_out, w_pa, w_pb, w_o, g_norm_ffn, w_ffn_gate, w_ffn_up, w_ffn_down):
    assert w_in.shape[0] == 1, "single trunk layer"
    pos_p = jnp.arange(SEQ, dtype=jnp.int32)
    pos_s = PAST_LEN + jnp.arange(DEC_SEQ, dtype=jnp.int32)
    l = 0
    w = w_in[l]
    w_tail = w[:, W_IN_GLR + GLA_RANK:].astype(BF16)
    w_lr = jnp.pad(w[:, W_IN_GLR:W_IN_GLR + GLA_RANK], ((0, 0), (0, LANES - GLA_RANK))).astype(BF16)
    w_gk = jnp.pad(w_gk2[l], ((0, LANES - GLA_RANK), (0, 0))).astype(BF16)
    wts = (g_norm_mix[l].reshape(1, D_MODEL), w, w_tail, w_lr, w_gk, b_gk2[l].reshape(1, GLA_KEY),
           jnp.tile(g_qnorm[l], 2).reshape(1, LANES), jnp.tile(g_knorm[l], 2).reshape(1, LANES),
           sinks[l], g_gla_out[l].reshape(1, GLA_DV),
           w_pa[l], w_pb[l], w_o[l],
           g_norm_ffn[l].reshape(1, D_MODEL),
           w_ffn_gate[l].astype(BF16), w_ffn_up[l].astype(BF16), w_ffn_down[l].astype(BF16))

    n_c = 1 + DEC_BATCH
    pad = (-n_c) % (2 * SUBLANES)
    c_all = jnp.pad(jnp.concatenate([c_prompt, c_sample], axis=0), ((0, pad), (0, 0)))
    ada = _ada(c_all, w_ada[l], b_ada[l].reshape(1, -1))

    yp, sp, kp, vp = _layer(x_prompt, ada[0:1], pos_p, None, None, None, wts)
    ys, ss, ks, vs = _layer(x_sample, ada[1:n_c], pos_s, state_gla[l], cache_swa_k[l], cache_swa_v[l], wts)
    return (yp, ys, sp[None], kp[None], vp[None], ss[None], ks[None], vs[None])
```

```python
import functools

import numpy as np
import jax
import jax.numpy as jnp
from jax import lax
from jax.experimental import pallas as pl
from jax.experimental.pallas import tpu as pltpu

F32 = jnp.float32
BF16 = jnp.bfloat16

D_MODEL = 2048
SEQ = 16384
DEC_BATCH = 128
DEC_SEQ = 8
PAST_LEN = 16384
GLA_HEADS = 4
GLA_DK = 256
GLA_DV = 512
GLA_KEY = GLA_HEADS * GLA_DK
GLA_VAL = GLA_HEADS * GLA_DV
GLA_RANK = 16
GLA_TAU = 16.0
SWA_HEAD_DIM = 64
SWA_HEADS = 32
SWA_KV_HEADS = 4
SWA_GROUP = SWA_HEADS // SWA_KV_HEADS
SWA_Q = SWA_HEADS * SWA_HEAD_DIM
SWA_KV = SWA_KV_HEADS * SWA_HEAD_DIM
WINDOW = 128
ROPE_THETA = 10000.0
D_FF = 5632
EPS = 1e-6
NEG_INF = -1e30
LOG2_E = 1.4426950408889634
SWA_SCORE_SCALE = (SWA_HEAD_DIM ** -0.5) * LOG2_E

SUBLANES = 8
LANES = 128
MXU_DIM = 256
VMEM_LIMIT_BYTES = 56 * 1024 * 1024

COL_GQ = 0
COL_GK = COL_GQ + GLA_KEY
COL_GV = COL_GK + GLA_KEY
COL_GR = COL_GV + GLA_VAL
COL_SQ = COL_GR + GLA_VAL
COL_SK = COL_SQ + SWA_Q
COL_SV = COL_SK + SWA_KV
COL_BG = COL_SV + SWA_KV
PROJ_COLS = COL_BG + 2 * D_MODEL
W_IN_GLR = COL_GR + GLA_VAL

GLA_CHUNK = 256
GLA_ONE_SIDED_MAX_DECAY = 80.0
GLA_PROMPT_HEADS_PER_STEP = 4
GLA_SAMPLE_BATCH = 8
SWA_SAMPLE_BATCH = 4
ROW_CHUNK = 128


def _dot(a, b):
    return jnp.dot(a, b, preferred_element_type=F32)


def _dot_nt(a, b):
    return lax.dot_general(a, b, (((1,), (1,)), ((), ())), preferred_element_type=F32)


def _dot_tn(a, b):
    return lax.dot_general(a, b, (((0,), (0,)), ((), ())), preferred_element_type=F32)


def _sigmoid(x):
    return 1.0 / (1.0 + jnp.exp(-x))


def _silu(x):
    return x * _sigmoid(x)


def _split3(x):
    hi = x.astype(BF16)
    r1 = x - hi.astype(F32)
    mid = r1.astype(BF16)
    lo = (r1 - mid.astype(F32)).astype(BF16)
    return hi, mid, lo


def _params(*sem):
    return pltpu.CompilerParams(dimension_semantics=sem, vmem_limit_bytes=VMEM_LIMIT_BYTES)


def _ada_body(c_ref, w_ref, b_ref, o_ref):
    a = _silu(c_ref[...]).astype(BF16)
    o_ref[...] = _dot(a, w_ref[...].astype(BF16)) + b_ref[...]


def _ada(c_all, w_ada, b_ada, tn=512):
    m, d = c_all.shape
    n = w_ada.shape[1]
    return pl.pallas_call(
        _ada_body,
        grid=(n // tn,),
        in_specs=[pl.BlockSpec((m, d), lambda j: (0, 0)),
                  pl.BlockSpec((d, tn), lambda j: (0, j)),
                  pl.BlockSpec((1, tn), lambda j: (0, j))],
        out_specs=pl.BlockSpec((m, tn), lambda j: (0, j)),
        out_shape=jax.ShapeDtypeStruct((m, n), F32),
        compiler_params=_params("arbitrary"),
        name="ada",
    )(c_all, w_ada, b_ada)


def _mod_spec(mod, tm, width, col):
    if mod.shape[0] == 1:
        return pl.BlockSpec((1, width), lambda i, j: (0, col(j)))
    return pl.BlockSpec((tm // DEC_SEQ, width), lambda i, j: (i, col(j)))


def _mod_rows(ref, row0, nrows):
    if ref.shape[0] == 1:
        return ref[...]
    seq0 = row0 // DEC_SEQ
    return jnp.concatenate(
        [jnp.broadcast_to(ref[pl.ds(seq0 + s, 1), :], (DEC_SEQ, ref.shape[1])) for s in range(nrows // DEC_SEQ)],
        axis=0)


def _modulated_norm_rows(x_ref, gn_ref, sc_ref, sh_ref, row0, nrows):
    x = x_ref[pl.ds(row0, nrows), :]
    ms = jnp.mean(x * x, axis=-1, keepdims=True)
    y = (x * lax.rsqrt(ms + EPS)) * gn_ref[...]
    return y * (1.0 + _mod_rows(sc_ref, row0, nrows)) + _mod_rows(sh_ref, row0, nrows)


def _row_loop(total_rows, body):
    def step(r, carry):
        body(pl.multiple_of(r * ROW_CHUNK, ROW_CHUNK))
        return carry
    lax.fori_loop(0, total_rows // ROW_CHUNK, step, 0)


def _inproj_body(x_ref, gn_ref, sc_ref, sh_ref, wt_ref, wlr_ref, wgk_ref, bgk_ref,
                 proj_ref, glog_ref, h_scr):
    @pl.when(pl.program_id(1) == 0)
    def _():
        def rows(row0):
            hb = _modulated_norm_rows(x_ref, gn_ref, sc_ref, sh_ref, row0, ROW_CHUNK).astype(BF16)
            h_scr[pl.ds(row0, ROW_CHUNK), :] = hb
            glr = _dot_nt(hb, wlr_ref[...])
            z = _dot(glr.astype(BF16), wgk_ref[...]) + bgk_ref[...]
            log_sig = jnp.minimum(z, 0.0) - jnp.log1p(jnp.exp(-jnp.abs(z)))
            glog_ref[pl.ds(row0, ROW_CHUNK), :] = log_sig * (1.0 / GLA_TAU)
        _row_loop(x_ref.shape[0], rows)

    proj_ref[...] = _dot_nt(h_scr[...], wt_ref[...]).astype(BF16)


def _inproj(x, gn, scale, shift, wt, wt_lr, w_gk, b_gk, tm=1024, tn=512):
    t, d = x.shape
    n = wt.shape[0] - GLA_RANK
    n_first = W_IN_GLR // tn
    zero = lambda j: 0
    wt_rows = lambda i, j: (pl.multiple_of(j * tn + jnp.where(j >= n_first, GLA_RANK, 0), GLA_RANK), 0)
    return pl.pallas_call(
        _inproj_body,
        grid=(t // tm, n // tn),
        in_specs=[pl.BlockSpec((tm, d), lambda i, j: (i, 0)),
                  pl.BlockSpec((1, d), lambda i, j: (0, 0)),
                  _mod_spec(scale, tm, d, zero),
                  _mod_spec(shift, tm, d, zero),
                  pl.BlockSpec((pl.Element(tn), pl.Element(d)), wt_rows),
                  pl.BlockSpec(wt_lr.shape, lambda i, j: (0, 0)),
                  pl.BlockSpec(w_gk.shape, lambda i, j: (0, 0)),
                  pl.BlockSpec(b_gk.shape, lambda i, j: (0, 0))],
        out_specs=[pl.BlockSpec((tm, tn), lambda i, j: (i, j)),
                   pl.BlockSpec((tm, GLA_KEY), lambda i, j: (i, 0))],
        out_shape=[jax.ShapeDtypeStruct((t, n), BF16),
                   jax.ShapeDtypeStruct((t, GLA_KEY), F32)],
        scratch_shapes=[pltpu.VMEM((tm, d), BF16)],
        compiler_params=_params("arbitrary", "arbitrary"),
        name="inproj",
    )(x, gn, scale, shift, wt, wt_lr, w_gk, b_gk)


def _gla_level_table(c, group):
    n_levels = int(np.log2(group))
    i = np.arange(c)[:, None]
    j = np.arange(c)[None, :]
    x = np.bitwise_xor(i, j)
    lvl = np.floor(np.log2(np.maximum(x, 1))).astype(np.int32)
    lvl = np.where(i == j, n_levels, lvl)
    valid = (i >= j) & (i // group == j // group)
    return np.where(valid, lvl, -1).astype(np.int32), n_levels


def _gla_tril(c, group):
    i = np.arange(c)[:, None]
    j = np.arange(c)[None, :]
    return ((i >= j) & (i // group == j // group)).astype(np.float32)


def _level_reference(b_scr, rows, level):
    s = 2 ** (level + 1)
    width = b_scr.shape[1]
    pieces = []
    if s >= SUBLANES:
        for blk in range(rows // s):
            mid = blk * s + s // 2
            pieces.append(jnp.broadcast_to(b_scr[mid:mid + 1, :], (s, width)))
    else:
        p = lax.broadcasted_iota(jnp.int32, (SUBLANES, width), 0)
        for tile in range(rows // SUBLANES):
            base = tile * SUBLANES
            mids = [base + q * s + s // 2 for q in range(SUBLANES // s)]
            r = jnp.broadcast_to(b_scr[mids[-1]:mids[-1] + 1, :], (SUBLANES, width))
            for q in range(SUBLANES // s - 2, -1, -1):
                row = jnp.broadcast_to(b_scr[mids[q]:mids[q] + 1, :], (SUBLANES, width))
                r = jnp.where(p < (q + 1) * s, row, r)
            pieces.append(r)
    return jnp.concatenate(pieces, axis=0) if len(pieces) > 1 else pieces[0]


def _gla_intra(q, k, k_bf, b, b_scr, lvl, n_levels):
    rows = q.shape[0]
    a = jnp.where(lvl == n_levels, _dot_nt(q.astype(BF16), k_bf), 0.0)
    for level in range(n_levels):
        f = jnp.exp(-jnp.abs(b - _level_reference(b_scr, rows, level)))
        p = _dot_nt((q * f).astype(BF16), (k * f).astype(BF16))
        a = jnp.where(lvl == level, p, a)
    return a


def _gla_cumsum(g, tril_bf):
    hi, mid, lo = _split3(g)
    return _dot(tril_bf, hi) + _dot(tril_bf, mid) + _dot(tril_bf, lo)


def _gla_out_gate(o, gout, gr):
    ms = jnp.mean(o * o, axis=-1, keepdims=True)
    y = (o * lax.rsqrt(ms + EPS)) * gout
    return (y * _silu(gr.astype(F32))).astype(BF16)


def _gla_prompt_body(q_ref, k_ref, v_ref, gr_ref, g_ref, tril_ref, lvl_ref, gout_ref,
                     oa_ref, st_ref, st_scr, b_scr, a_scr, *, n_levels, heads):
    c = pl.program_id(1)
    rows = q_ref.shape[0]

    @pl.when(c == 0)
    def _():
        st_scr[...] = jnp.zeros_like(st_scr)

    tril, lvl = tril_ref[...], lvl_ref[...]
    per_head = []
    mild = None
    for h in range(heads):
        ksl = slice(h * GLA_DK, (h + 1) * GLA_DK)
        b = _gla_cumsum(g_ref[:, ksl], tril)
        b_scr[h] = b
        b_end = b_scr[h, rows - 1:rows, :]
        q = q_ref[:, ksl].astype(F32) * (GLA_DK ** -0.5)
        k_bf = k_ref[:, ksl]
        k = k_bf.astype(F32)
        st = st_scr[h]
        qd = (q * jnp.exp(b)).astype(BF16)
        o_inter = _dot_nt(qd, st.astype(BF16))
        head_mild = jnp.min(b_end) > -GLA_ONE_SIDED_MAX_DECAY
        mild = head_mild if mild is None else jnp.logical_and(mild, head_mild)
        per_head.append((b, b_end, q, k, k_bf, st, qd, o_inter))

    @pl.when(mild)
    def _():
        for h, (b, _, _, k, _, _, qd, _) in enumerate(per_head):
            a_scr[h] = jnp.where(lvl >= 0, _dot_nt(qd, (k * jnp.exp(-b)).astype(BF16)), 0.0)

    @pl.when(jnp.logical_not(mild))
    def _():
        for h, (b, _, q, k, k_bf, _, _, _) in enumerate(per_head):
            a_scr[h] = _gla_intra(q, k, k_bf, b, b_scr.at[h], lvl, n_levels)

    for h, (b, b_end, _, k, _, st, _, o_inter) in enumerate(per_head):
        vsl = slice(h * GLA_DV, (h + 1) * GLA_DV)
        v = v_ref[:, vsl]
        o = o_inter + _dot(a_scr[h].astype(BF16), v)
        k_dec = (k * jnp.exp(b_end - b)).astype(BF16)
        st_scr[h] = st * jnp.exp(b_end) + _dot_tn(v, k_dec)
        oa_ref[:, vsl] = _gla_out_gate(o, gout_ref[...], gr_ref[:, vsl])

    @pl.when(c == pl.num_programs(1) - 1)
    def _():
        for h in range(heads):
            st_ref[h] = st_scr[h].T


def _gla_prompt(proj, glog, g_out, heads=GLA_PROMPT_HEADS_PER_STEP):
    t = proj.shape[0]
    c = GLA_CHUNK
    lvl, n_levels = _gla_level_table(c, c)
    tril = jnp.asarray(_gla_tril(c, c), BF16)
    lvl = jnp.asarray(lvl)
    wk, wv = heads * GLA_DK, heads * GLA_DV
    qb, kb, vb, rb = COL_GQ // wk, COL_GK // wk, COL_GV // wv, COL_GR // wv
    return pl.pallas_call(
        functools.partial(_gla_prompt_body, n_levels=n_levels, heads=heads),
        grid=(GLA_HEADS // heads, t // c),
        in_specs=[pl.BlockSpec((c, wk), lambda h, i: (i, qb + h)),
                  pl.BlockSpec((c, wk), lambda h, i: (i, kb + h)),
                  pl.BlockSpec((c, wv), lambda h, i: (i, vb + h)),
                  pl.BlockSpec((c, wv), lambda h, i: (i, rb + h)),
                  pl.BlockSpec((c, wk), lambda h, i: (i, h)),
                  pl.BlockSpec((c, c), lambda h, i: (0, 0)),
                  pl.BlockSpec((c, c), lambda h, i: (0, 0)),
                  pl.BlockSpec((1, GLA_DV), lambda h, i: (0, 0))],
        out_specs=[pl.BlockSpec((c, wv), lambda h, i: (i, h)),
                   pl.BlockSpec((heads, GLA_DK, GLA_DV), lambda h, i: (h, 0, 0))],
        out_shape=[jax.ShapeDtypeStruct((t, GLA_VAL), BF16),
                   jax.ShapeDtypeStruct((GLA_HEADS, GLA_DK, GLA_DV), F32)],
        scratch_shapes=[pltpu.VMEM((heads, GLA_DV, GLA_DK), F32), pltpu.VMEM((heads, c, GLA_DK), F32),
                        pltpu.VMEM((heads, c, c), F32)],
        compiler_params=_params("arbitrary", "arbitrary"),
        name="gla_prompt",
    )(proj, proj, proj, proj, glog, tril, lvl, g_out)


def _gla_sample_body(q_ref, k_ref, v_ref, gr_ref, g_ref, s0_ref, tril_ref, lvl_ref, gout_ref,
                     oa_ref, s1_ref, b_scr, *, n_levels):
    rows = q_ref.shape[0]
    nseq = rows // DEC_SEQ
    b = _gla_cumsum(g_ref[...], tril_ref[...])
    b_scr[...] = b
    b_end = jnp.concatenate(
        [jnp.broadcast_to(b_scr[(s + 1) * DEC_SEQ - 1:(s + 1) * DEC_SEQ, :], (DEC_SEQ, GLA_DK))
         for s in range(nseq)], axis=0)
    q = q_ref[...].astype(F32) * (GLA_DK ** -0.5)
    k_bf = k_ref[...]
    k = k_bf.astype(F32)
    v = v_ref[...]
    qd = (q * jnp.exp(b)).astype(BF16)
    a = _gla_intra(q, k, k_bf, b, b_scr, lvl_ref[...], n_levels)
    o_intra = _dot(a.astype(BF16), v)

    stacked = jnp.concatenate([k * jnp.exp(b_end - b), jnp.exp(b_end)], axis=0)
    stacked_t = stacked.T
    kd_t = stacked_t[:, :rows]
    lane = lax.broadcasted_iota(jnp.int32, kd_t.shape, 1)
    outs = []
    for s in range(nseq):
        s0 = s0_ref[s, 0]
        outs.append(_dot(qd[s * DEC_SEQ:(s + 1) * DEC_SEQ, :], s0.astype(BF16)))
        in_seq = (lane >= s * DEC_SEQ) & (lane < (s + 1) * DEC_SEQ)
        kd_s = jnp.where(in_seq, kd_t, 0.0).astype(BF16)
        col = rows + s * DEC_SEQ
        decay = stacked_t[:, col:col + 1]
        s1_ref[s, 0] = s0 * decay + _dot(kd_s, v)
    o = jnp.concatenate(outs, axis=0) + o_intra
    oa_ref[...] = _gla_out_gate(o, gout_ref[...], gr_ref[...])


def _gla_sample(proj, glog, state, g_out):
    t = proj.shape[0]
    nseq = GLA_SAMPLE_BATCH
    rows = nseq * DEC_SEQ
    lvl, n_levels = _gla_level_table(rows, DEC_SEQ)
    tril = jnp.asarray(_gla_tril(rows, DEC_SEQ), BF16)
    lvl = jnp.asarray(lvl)
    qb, kb, vb, rb = COL_GQ // GLA_DK, COL_GK // GLA_DK, COL_GV // GLA_DV, COL_GR // GLA_DV
    return pl.pallas_call(
        functools.partial(_gla_sample_body, n_levels=n_levels),
        grid=(t // rows, GLA_HEADS),
        in_specs=[pl.BlockSpec((rows, GLA_DK), lambda i, h: (i, qb + h)),
                  pl.BlockSpec((rows, GLA_DK), lambda i, h: (i, kb + h)),
                  pl.BlockSpec((rows, GLA_DV), lambda i, h: (i, vb + h)),
                  pl.BlockSpec((rows, GLA_DV), lambda i, h: (i, rb + h)),
                  pl.BlockSpec((rows, GLA_DK), lambda i, h: (i, h)),
                  pl.BlockSpec((nseq, 1, GLA_DK, GLA_DV), lambda i, h: (i, h, 0, 0)),
                  pl.BlockSpec((rows, rows), lambda i, h: (0, 0)),
                  pl.BlockSpec((rows, rows), lambda i, h: (0, 0)),
                  pl.BlockSpec((1, GLA_DV), lambda i, h: (0, 0))],
        out_specs=[pl.BlockSpec((rows, GLA_DV), lambda i, h: (i, h)),
                   pl.BlockSpec((nseq, 1, GLA_DK, GLA_DV), lambda i, h: (i, h, 0, 0))],
        out_shape=[jax.ShapeDtypeStruct((t, GLA_VAL), BF16),
                   jax.ShapeDtypeStruct(state.shape, F32)],
        scratch_shapes=[pltpu.VMEM((rows, GLA_DK), F32)],
        compiler_params=_params("arbitrary", "arbitrary"),
        name="gla_sample",
    )(proj, proj, proj, proj, glog, state, tril, lvl, g_out)


def _rope_tables(pos):
    half = SWA_HEAD_DIM // 2
    inv_freq = ROPE_THETA ** (-(jnp.arange(half, dtype=F32) * 2.0) / SWA_HEAD_DIM)
    ang = pos.astype(F32)[:, None] * inv_freq[None, :]
    cos, sin = jnp.cos(ang), jnp.sin(ang)
    cos_t = jnp.concatenate([cos, cos, cos, cos], axis=-1)
    sin_t = jnp.concatenate([-sin, sin, -sin, sin], axis=-1)
    return cos_t, sin_t


def _head_mean_matrix():
    i = np.arange(MXU_DIM)[:, None] // SWA_HEAD_DIM
    j = np.arange(MXU_DIM)[None, :] // SWA_HEAD_DIM
    return (i == j).astype(np.float32) / SWA_HEAD_DIM


def _qk_norm_rope(x, gain, cos_t, sin_t, mean_mat):
    rows, width = x.shape
    lane = lax.broadcasted_iota(jnp.int32, (rows, LANES), 1)
    first_half = (lane % SWA_HEAD_DIM) < (SWA_HEAD_DIM // 2)
    shift = SWA_HEAD_DIM // 2
    sq = (x * x).astype(BF16)
    tiles = []
    for c in range(width // MXU_DIM):
        ms = _dot(sq[:, c * MXU_DIM:(c + 1) * MXU_DIM], mean_mat)
        for t in range(MXU_DIM // LANES):
            lo = c * MXU_DIM + t * LANES
            y = (x[:, lo:lo + LANES] * lax.rsqrt(ms[:, t * LANES:(t + 1) * LANES] + EPS)) * gain
            rot = jnp.where(first_half, pltpu.roll(y, LANES - shift, 1), pltpu.roll(y, shift, 1))
            tiles.append(y * cos_t + rot * sin_t)
    return tiles


def _attend(q_tiles, k_tile, v_tile, parity_of_kv, valid, sinks_ref, kv_head):
    rows = q_tiles[0].shape[0]
    lane = lax.broadcasted_iota(jnp.int32, k_tile.shape, 1)
    low = lane < SWA_HEAD_DIM

    def place(x, want_low):
        have_low = parity_of_kv == 0
        src = x if have_low == want_low else pltpu.roll(x, SWA_HEAD_DIM, 1)
        return jnp.where(low if want_low else ~low, src, 0.0).astype(BF16)

    q_all = jnp.concatenate(q_tiles, axis=0).astype(BF16)
    npairs = len(q_tiles)
    outs = [None] * npairs
    for parity in range(2):
        kk = place(k_tile, parity == 0)
        vv = place(v_tile, parity == 0)
        s_all = _dot_nt(q_all, kk)
        p_list, inv_list = [], []
        for pair in range(npairs):
            sink = sinks_ref[kv_head * SWA_GROUP + 2 * pair + parity] * LOG2_E
            s = jnp.where(valid, s_all[pair * rows:(pair + 1) * rows, :], NEG_INF)
            m = jnp.maximum(jnp.max(s, axis=-1, keepdims=True), sink)
            p = jnp.exp2(s - m)
            denom = jnp.sum(p, axis=-1, keepdims=True) + jnp.exp2(sink - m)
            p_list.append(p.astype(BF16))
            inv_list.append(1.0 / denom)
        pv = _dot(jnp.concatenate(p_list, axis=0), vv)
        for pair in range(npairs):
            contrib = pv[pair * rows:(pair + 1) * rows, :] * inv_list[pair]
            outs[pair] = contrib if outs[pair] is None else outs[pair] + contrib
    return outs


def _attend_all_heads(q_tiles, k_all, v_all, valid, sinks_ref, store):
    pairs = SWA_GROUP // 2
    for g in range(SWA_KV_HEADS):
        lo = (g // 2) * LANES
        outs = _attend(q_tiles[g * pairs:(g + 1) * pairs], k_all[:, lo:lo + LANES], v_all[:, lo:lo + LANES],
                       g % 2, valid, sinks_ref, g)
        for p, o in enumerate(outs):
            store(g * pairs + p, o.astype(BF16))


def _swa_prompt_body(sinks_ref, q_ref, k_ref, v_ref, cb_ref, sb_ref, cr_ref, sr_ref, crs_ref, srs_ref,
                     gq_ref, gk_ref, mm_ref, ob_ref, knew_ref, vnew_ref, kprev_scr, vprev_scr):
    n = pl.program_id(0)
    blk = q_ref.shape[0]

    @pl.when(n == 0)
    def _():
        kprev_scr[...] = jnp.zeros_like(kprev_scr)
        vprev_scr[...] = jnp.zeros_like(vprev_scr)

    cb, sb = cb_ref[pl.ds(n, 1), :], sb_ref[pl.ds(n, 1), :]
    cos_t = cb * cr_ref[...] - sb * sr_ref[...]
    sin_t = sb * crs_ref[...] + cb * srs_ref[...]
    mean_mat = mm_ref[...]
    q_tiles = [t * SWA_SCORE_SCALE
               for t in _qk_norm_rope(q_ref[...].astype(F32), gq_ref[...], cos_t, sin_t, mean_mat)]
    kn = jnp.concatenate(_qk_norm_rope(k_ref[...].astype(F32), gk_ref[...], cos_t, sin_t, mean_mat), axis=1)
    vn = v_ref[...].astype(F32)
    knew_ref[...] = kn
    vnew_ref[...] = vn
    k_all = jnp.concatenate([kprev_scr[...], kn], axis=0)
    v_all = jnp.concatenate([vprev_scr[...], vn], axis=0)

    qi = lax.broadcasted_iota(jnp.int32, (blk, 2 * blk), 0)
    kj = lax.broadcasted_iota(jnp.int32, (blk, 2 * blk), 1)
    dist = blk + qi - kj
    valid = (dist >= 0) & (dist <= WINDOW) & ((n - 1) * blk + kj >= 0)

    def store(tile, value):
        ob_ref[:, tile * LANES:(tile + 1) * LANES] = value
    _attend_all_heads(q_tiles, k_all, v_all, valid, sinks_ref, store)

    kprev_scr[...] = kn
    vprev_scr[...] = vn


def _rope_block_tables(nblocks, blk):
    half = SWA_HEAD_DIM // 2
    inv_freq = ROPE_THETA ** (-(jnp.arange(half, dtype=F32) * 2.0) / SWA_HEAD_DIM)
    lane_freq = jnp.tile(inv_freq, LANES // half)[None, :]
    sign = jnp.tile(jnp.concatenate([-jnp.ones((half,), F32), jnp.ones((half,), F32)]), LANES // SWA_HEAD_DIM)[None, :]
    ang_b = (jnp.arange(nblocks, dtype=jnp.int32) * blk).astype(F32)[:, None] * lane_freq
    ang_r = jnp.arange(blk, dtype=jnp.int32).astype(F32)[:, None] * lane_freq
    cr, sr = jnp.cos(ang_r), jnp.sin(ang_r)
    return jnp.cos(ang_b), jnp.sin(ang_b), cr, sr, sign * cr, sign * sr


def _swa_prompt(proj, sinks, g_q, g_k):
    t = proj.shape[0]
    blk = WINDOW
    nblocks = t // blk
    mean_mat = jnp.asarray(_head_mean_matrix(), BF16)
    tables = _rope_block_tables(nblocks, blk)
    qb, kb, vb = COL_SQ // SWA_Q, COL_SK // SWA_KV, COL_SV // SWA_KV
    whole = lambda n, s: (0, 0)
    grid_spec = pltpu.PrefetchScalarGridSpec(
        num_scalar_prefetch=1,
        grid=(nblocks,),
        in_specs=[pl.BlockSpec((blk, SWA_Q), lambda n, s: (n, qb)),
                  pl.BlockSpec((blk, SWA_KV), lambda n, s: (n, kb)),
                  pl.BlockSpec((blk, SWA_KV), lambda n, s: (n, vb)),
                  pl.BlockSpec((nblocks, LANES), whole),
                  pl.BlockSpec((nblocks, LANES), whole),
                  pl.BlockSpec((blk, LANES), whole),
                  pl.BlockSpec((blk, LANES), whole),
                  pl.BlockSpec((blk, LANES), whole),
                  pl.BlockSpec((blk, LANES), whole),
                  pl.BlockSpec((1, LANES), lambda n, s: (0, 0)),
                  pl.BlockSpec((1, LANES), lambda n, s: (0, 0)),
                  pl.BlockSpec((MXU_DIM, MXU_DIM), lambda n, s: (0, 0))],
        out_specs=[pl.BlockSpec((blk, SWA_Q), lambda n, s: (n, 0)),
                   pl.BlockSpec((blk, SWA_KV), lambda n, s: (0, 0)),
                   pl.BlockSpec((blk, SWA_KV), lambda n, s: (0, 0))],
        scratch_shapes=[pltpu.VMEM((blk, SWA_KV), F32), pltpu.VMEM((blk, SWA_KV), F32)],
    )
    return pl.pallas_call(
        _swa_prompt_body,
        grid_spec=grid_spec,
        out_shape=[jax.ShapeDtypeStruct((t, SWA_Q), BF16),
                   jax.ShapeDtypeStruct((blk, SWA_KV), F32),
                   jax.ShapeDtypeStruct((blk, SWA_KV), F32)],
        compiler_params=_params("arbitrary"),
        name="swa_prompt",
    )(sinks, proj, proj, proj, *tables, g_q, g_k, mean_mat)


def _swa_sample_body(sinks_ref, q_ref, k_ref, v_ref, bk_ref, bv_ref, cos_ref, sin_ref, gq_ref, gk_ref, mm_ref,
                     ob_ref, kout_ref, vout_ref):
    nseq = bk_ref.shape[0]
    cos_t = jnp.concatenate([cos_ref[...]] * nseq, axis=0)
    sin_t = jnp.concatenate([sin_ref[...]] * nseq, axis=0)
    mean_mat = mm_ref[...]
    q_tiles = [t * SWA_SCORE_SCALE
               for t in _qk_norm_rope(q_ref[...].astype(F32), gq_ref[...], cos_t, sin_t, mean_mat)]
    kn = jnp.concatenate(_qk_norm_rope(k_ref[...].astype(F32), gk_ref[...], cos_t, sin_t, mean_mat), axis=1)
    vn = v_ref[...].astype(F32)

    keys = 2 * WINDOW
    qi = lax.broadcasted_iota(jnp.int32, (DEC_SEQ, keys), 0)
    kj = lax.broadcasted_iota(jnp.int32, (DEC_SEQ, keys), 1)
    dist = WINDOW + qi - kj
    valid = (dist >= 0) & (dist <= WINDOW)
    valid = jnp.concatenate([valid] * nseq, axis=0)
    pad = jnp.zeros((keys - WINDOW - DEC_SEQ, SWA_KV), F32)
    rows = nseq * DEC_SEQ

    k_parts, v_parts = [], []
    for s in range(nseq):
        rsl = slice(s * DEC_SEQ, (s + 1) * DEC_SEQ)
        bk, bv = bk_ref[s], bv_ref[s]
        kout_ref[s, :WINDOW - DEC_SEQ, :] = bk[DEC_SEQ:, :]
        kout_ref[s, WINDOW - DEC_SEQ:, :] = kn[rsl, :]
        vout_ref[s, :WINDOW - DEC_SEQ, :] = bv[DEC_SEQ:, :]
        vout_ref[s, WINDOW - DEC_SEQ:, :] = vn[rsl, :]
        k_parts += [bk, kn[rsl, :], pad]
        v_parts += [bv, vn[rsl, :], pad]
    k_stack = jnp.concatenate(k_parts, axis=0)
    v_stack = jnp.concatenate(v_parts, axis=0)

    low = lax.broadcasted_iota(jnp.int32, (nseq * keys, LANES), 1) < SWA_HEAD_DIM
    pairs = SWA_GROUP // 2
    zero_blk = jnp.zeros((DEC_SEQ, keys), F32)
    for g in range(SWA_KV_HEADS):
        lo = (g // 2) * LANES
        kv_low = g % 2 == 0
        keep = low if kv_low else jnp.logical_not(low)
        kk = jnp.where(keep, k_stack[:, lo:lo + LANES], 0.0).astype(BF16)
        vv = jnp.where(keep, v_stack[:, lo:lo + LANES], 0.0).astype(BF16)
        q_g = jnp.concatenate(q_tiles[g * pairs:(g + 1) * pairs], axis=0)
        out = None
        for parity in range(2):
            aligned = (parity == 0) == kv_low
            q_use = q_g if aligned else pltpu.roll(q_g, SWA_HEAD_DIM, 1)
            s_all = _dot_nt(q_use.astype(BF16), kk)
            p_rows, inv = [], []
            for pair in range(pairs):
                sink = sinks_ref[g * SWA_GROUP + 2 * pair + parity] * LOG2_E
                sc = jnp.concatenate(
                    [s_all[pair * rows + s * DEC_SEQ: pair * rows + (s + 1) * DEC_SEQ, s * keys:(s + 1) * keys]
                     for s in range(nseq)], axis=0)
                sc = jnp.where(valid, sc, NEG_INF)
                m = jnp.maximum(jnp.max(sc, axis=-1, keepdims=True), sink)
                p = jnp.exp2(sc - m)
                inv.append(1.0 / (jnp.sum(p, axis=-1, keepdims=True) + jnp.exp2(sink - m)))
                for s in range(nseq):
                    blocks = [p[s * DEC_SEQ:(s + 1) * DEC_SEQ, :] if s2 == s else zero_blk for s2 in range(nseq)]
                    p_rows.append(jnp.concatenate(blocks, axis=1))
            pv = _dot(jnp.concatenate(p_rows, axis=0).astype(BF16), vv)
            pv = jnp.concatenate([pv[pair * rows:(pair + 1) * rows, :] * inv[pair] for pair in range(pairs)], axis=0)
            if not aligned:
                pv = pltpu.roll(pv, SWA_HEAD_DIM, 1)
            out = pv if out is None else out + pv
        for pair in range(pairs):
            tile = g * pairs + pair
            ob_ref[:, tile * LANES:(tile + 1) * LANES] = out[pair * rows:(pair + 1) * rows, :].astype(BF16)


def _swa_sample(proj, buf_k, buf_v, sinks, cos_t, sin_t, g_q, g_k):
    t = proj.shape[0]
    nseq = SWA_SAMPLE_BATCH
    rows = nseq * DEC_SEQ
    mean_mat = jnp.asarray(_head_mean_matrix(), BF16)
    qb, kb, vb = COL_SQ // SWA_Q, COL_SK // SWA_KV, COL_SV // SWA_KV
    grid_spec = pltpu.PrefetchScalarGridSpec(
        num_scalar_prefetch=1,
        grid=(t // rows,),
        in_specs=[pl.BlockSpec((rows, SWA_Q), lambda n, s: (n, qb)),
                  pl.BlockSpec((rows, SWA_KV), lambda n, s: (n, kb)),
                  pl.BlockSpec((rows, SWA_KV), lambda n, s: (n, vb)),
                  pl.BlockSpec((nseq, WINDOW, SWA_KV), lambda n, s: (n, 0, 0)),
                  pl.BlockSpec((nseq, WINDOW, SWA_KV), lambda n, s: (n, 0, 0)),
                  pl.BlockSpec((DEC_SEQ, LANES), lambda n, s: (0, 0)),
                  pl.BlockSpec((DEC_SEQ, LANES), lambda n, s: (0, 0)),
                  pl.BlockSpec((1, LANES), lambda n, s: (0, 0)),
                  pl.BlockSpec((1, LANES), lambda n, s: (0, 0)),
                  pl.BlockSpec((MXU_DIM, MXU_DIM), lambda n, s: (0, 0))],
        out_specs=[pl.BlockSpec((rows, SWA_Q), lambda n, s: (n, 0)),
                   pl.BlockSpec((nseq, WINDOW, SWA_KV), lambda n, s: (n, 0, 0)),
                   pl.BlockSpec((nseq, WINDOW, SWA_KV), lambda n, s: (n, 0, 0))],
    )
    return pl.pallas_call(
        _swa_sample_body,
        grid_spec=grid_spec,
        out_shape=[jax.ShapeDtypeStruct((t, SWA_Q), BF16),
                   jax.ShapeDtypeStruct(buf_k.shape, F32),
                   jax.ShapeDtypeStruct(buf_v.shape, F32)],
        compiler_params=_params("arbitrary"),
        name="swa_sample",
    )(sinks, proj, proj, proj, buf_k, buf_v, cos_t, sin_t, g_q, g_k, mean_mat)


def _merge_body(oa_ref, ob_ref, wpa_ref, wpb_ref, ga_ref, gb_ref, m_ref):
    ga = _sigmoid(ga_ref[...].astype(F32))
    gb = _sigmoid(gb_ref[...].astype(F32))
    m_ref[...] = (ga * _dot(oa_ref[...], wpa_ref[...]) + gb * _dot(ob_ref[...], wpb_ref[...])).astype(BF16)


def _merge(oa, ob, proj, w_pa, w_pb, tm=1024, tn=512):
    t, d = oa.shape
    ga_blk, gb_blk = COL_BG // tn, (COL_BG + D_MODEL) // tn
    return pl.pallas_call(
        _merge_body,
        grid=(t // tm, d // tn),
        in_specs=[pl.BlockSpec((tm, d), lambda i, j: (i, 0)),
                  pl.BlockSpec((tm, d), lambda i, j: (i, 0)),
                  pl.BlockSpec((d, tn), lambda i, j: (0, j)),
                  pl.BlockSpec((d, tn), lambda i, j: (0, j)),
                  pl.BlockSpec((tm, tn), lambda i, j: (i, ga_blk + j)),
                  pl.BlockSpec((tm, tn), lambda i, j: (i, gb_blk + j))],
        out_specs=pl.BlockSpec((tm, tn), lambda i, j: (i, j)),
        out_shape=jax.ShapeDtypeStruct((t, d), BF16),
        compiler_params=_params("arbitrary", "arbitrary"),
        name="merge",
    )(oa, ob, w_pa, w_pb, proj, proj)


def _oproj_body(m_ref, w_ref, x_ref, gate_ref, o_ref, y_scr):
    if gate_ref.shape[0] == 1:
        o_ref[...] = x_ref[...] + gate_ref[...] * _dot(m_ref[...], w_ref[...])
        return
    y_scr[...] = _dot(m_ref[...], w_ref[...])

    def rows(row0):
        sl = pl.ds(row0, ROW_CHUNK)
        o_ref[sl, :] = x_ref[sl, :] + _mod_rows(gate_ref, row0, ROW_CHUNK) * y_scr[sl, :]
    _row_loop(x_ref.shape[0], rows)


def _oproj(merged, w_o, x, gate, tm=1024, tn=512):
    t, d = x.shape
    return pl.pallas_call(
        _oproj_body,
        grid=(t // tm, d // tn),
        in_specs=[pl.BlockSpec((tm, d), lambda i, j: (i, 0)),
                  pl.BlockSpec((d, tn), lambda i, j: (0, j)),
                  pl.BlockSpec((tm, tn), lambda i, j: (i, j)),
                  _mod_spec(gate, tm, tn, lambda j: j)],
        out_specs=pl.BlockSpec((tm, tn), lambda i, j: (i, j)),
        out_shape=jax.ShapeDtypeStruct(x.shape, F32),
        scratch_shapes=[pltpu.VMEM((tm, tn), F32)],
        compiler_params=_params("arbitrary", "arbitrary"),
        name="oproj",
    )(merged, w_o, x, gate)


def _ffn_body(x_ref, gn_ref, sc_ref, sh_ref, gate_ref, wg_ref, wu_ref, wd_ref, o_ref, h_scr):
    f = pl.program_id(1)

    @pl.when(f == 0)
    def _():
        def rows(row0):
            h_scr[pl.ds(row0, ROW_CHUNK), :] = _modulated_norm_rows(
                x_ref, gn_ref, sc_ref, sh_ref, row0, ROW_CHUNK).astype(BF16)
            o_ref[pl.ds(row0, ROW_CHUNK), :] = jnp.zeros((ROW_CHUNK, o_ref.shape[1]), F32)
        _row_loop(x_ref.shape[0], rows)

    h = h_scr[...]
    a = _silu(_dot(h, wg_ref[...])) * _dot(h, wu_ref[...])
    o_ref[...] += _dot(a.astype(BF16), wd_ref[...])

    @pl.when(f == pl.num_programs(1) - 1)
    def _():
        def rows(row0):
            sl = pl.ds(row0, ROW_CHUNK)
            o_ref[sl, :] = x_ref[sl, :] + _mod_rows(gate_ref, row0, ROW_CHUNK) * o_ref[sl, :]
        _row_loop(x_ref.shape[0], rows)


def _ffn(x, gn, scale, shift, gate, w_gate, w_up, w_down, tm=1024, tf=512):
    t, d = x.shape
    dff = w_gate.shape[1]
    zero = lambda j: 0
    return pl.pallas_call(
        _ffn_body,
        grid=(t // tm, dff // tf),
        in_specs=[pl.BlockSpec((tm, d), lambda i, f: (i, 0)),
                  pl.BlockSpec((1, d), lambda i, f: (0, 0)),
                  _mod_spec(scale, tm, d, zero),
                  _mod_spec(shift, tm, d, zero),
                  _mod_spec(gate, tm, d, zero),
                  pl.BlockSpec((d, tf), lambda i, f: (0, f)),
                  pl.BlockSpec((d, tf), lambda i, f: (0, f)),
                  pl.BlockSpec((tf, d), lambda i, f: (f, 0))],
        out_specs=pl.BlockSpec((tm, d), lambda i, f: (i, 0)),
        out_shape=jax.ShapeDtypeStruct(x.shape, F32),
        scratch_shapes=[pltpu.VMEM((tm, d), BF16)],
        compiler_params=_params("arbitrary", "arbitrary"),
        name="ffn",
    )(x, gn, scale, shift, gate, w_gate, w_up, w_down)


def _layer(x, ada, pos, state, buf_k, buf_v, wts):
    (gn_mix, wt_main, wt_lr, w_gk, b_gk, g_q, g_k, sinks, g_out, w_pa, w_pb, w_o, gn_ffn,
     w_fg, w_fu, w_fd) = wts
    bsz, tlen, d = x.shape
    x2 = x.reshape(bsz * tlen, d)
    shift_m, scale_m, gate_m, shift_f, scale_f, gate_f = (ada[:, i * d:(i + 1) * d] for i in range(6))

    proj, glog = _inproj(x2, gn_mix, scale_m, shift_m, wt_main, wt_lr, w_gk, b_gk)
    if state is None:
        oa, s_new = _gla_prompt(proj, glog, g_out)
        s_new = s_new[None]
        ob, k_new, v_new = _swa_prompt(proj, sinks, g_q, g_k)
    else:
        cos_t, sin_t = _rope_tables(pos)
        oa, s_new = _gla_sample(proj, glog, state, g_out)
        ob, k_new, v_new = _swa_sample(proj, buf_k.reshape(bsz, WINDOW, SWA_KV), buf_v.reshape(bsz, WINDOW, SWA_KV),
                                       sinks, cos_t, sin_t, g_q, g_k)
    merged = _merge(oa, ob, proj, w_pa, w_pb)
    x1 = _oproj(merged, w_o, x2, gate_m)
    y = _ffn(x1, gn_ffn, scale_f, shift_f, gate_f, w_fg, w_fu, w_fd)
    return (y.reshape(bsz, tlen, d), s_new,
            k_new.reshape(bsz, WINDOW, SWA_KV_HEADS, SWA_HEAD_DIM),
            v_new.reshape(bsz, WINDOW, SWA_KV_HEADS, SWA_HEAD_DIM))


def kernel(x_prompt, x_sample, c_prompt, c_sample, state_gla, cache_swa_k, cache_swa_v, w_ada, b_ada, g_norm_mix, w_in, w_gk2, b_gk2, g_qnorm, g_knorm, sinks, g_gla_out, w_pa, w_pb, w_o, g_norm_ffn, w_ffn_gate, w_ffn_up, w_ffn_down):
    assert w_in.shape[0] == 1, "single trunk layer"
    pos_p = jnp.arange(SEQ, dtype=jnp.int32)
    pos_s = PAST_LEN + jnp.arange(DEC_SEQ, dtype=jnp.int32)
    l = 0
    wt = jnp.transpose(w_in[l]).astype(BF16)
    wt_lr = jnp.pad(wt[W_IN_GLR:W_IN_GLR + GLA_RANK], ((0, LANES - GLA_RANK), (0, 0)))
    w_gk = jnp.pad(w_gk2[l], ((0, LANES - GLA_RANK), (0, 0))).astype(BF16)
    wts = (g_norm_mix[l].reshape(1, D_MODEL), wt, wt_lr, w_gk, b_gk2[l].reshape(1, GLA_KEY),
           jnp.tile(g_qnorm[l], 2).reshape(1, LANES), jnp.tile(g_knorm[l], 2).reshape(1, LANES),
           sinks[l], g_gla_out[l].reshape(1, GLA_DV),
           w_pa[l].astype(BF16), w_pb[l].astype(BF16), w_o[l].astype(BF16),
           g_norm_ffn[l].reshape(1, D_MODEL),
           w_ffn_gate[l].astype(BF16), w_ffn_up[l].astype(BF16), w_ffn_down[l].astype(BF16))

    n_c = 1 + DEC_BATCH
    pad = (-n_c) % (2 * SUBLANES)
    c_all = jnp.pad(jnp.concatenate([c_prompt, c_sample], axis=0), ((0, pad), (0, 0)))
    ada = _ada(c_all, w_ada[l], b_ada[l].reshape(1, -1))

    yp, sp, kp, vp = _layer(x_prompt, ada[0:1], pos_p, None, None, None, wts)
    ys, ss, ks, vs = _layer(x_sample, ada[1:n_c], pos_s, state_gla[l], cache_swa_k[l], cache_swa_v[l], wts)
    return (yp, ys, sp[None], kp[None], vp[None], ss[None], ks[None], vs[None])
```

```python
import functools

import numpy as np
import jax
import jax.numpy as jnp
from jax import lax
from jax.experimental import pallas as pl
from jax.experimental.pallas import tpu as pltpu

F32 = jnp.float32
BF16 = jnp.bfloat16

D_MODEL = 2048
SEQ = 16384
DEC_BATCH = 128
DEC_SEQ = 8
PAST_LEN = 16384
GLA_HEADS = 4
GLA_DK = 256
GLA_DV = 512
GLA_KEY = GLA_HEADS * GLA_DK
GLA_VAL = GLA_HEADS * GLA_DV
GLA_RANK = 16
GLA_TAU = 16.0
SWA_HEAD_DIM = 64
SWA_HEADS = 32
SWA_KV_HEADS = 4
SWA_GROUP = SWA_HEADS // SWA_KV_HEADS
SWA_Q = SWA_HEADS * SWA_HEAD_DIM
SWA_KV = SWA_KV_HEADS * SWA_HEAD_DIM
WINDOW = 128
ROPE_THETA = 10000.0
D_FF = 5632
EPS = 1e-6
NEG_INF = -1e30
LOG2_E = 1.4426950408889634
SWA_SCORE_SCALE = (SWA_HEAD_DIM ** -0.5) * LOG2_E

SUBLANES = 8
LANES = 128
MXU_DIM = 256
VMEM_LIMIT_BYTES = 58 * 1024 * 1024

COL_GQ = 0
COL_GK = COL_GQ + GLA_KEY
COL_GV = COL_GK + GLA_KEY
COL_GR = COL_GV + GLA_VAL
COL_SQ = COL_GR + GLA_VAL
COL_SK = COL_SQ + SWA_Q
COL_SV = COL_SK + SWA_KV
COL_BG = COL_SV + SWA_KV
PROJ_COLS = COL_BG + 2 * D_MODEL
W_IN_GLR = COL_GR + GLA_VAL

GLA_CHUNK = 256
GLA_ONE_SIDED_MAX_DECAY = 80.0
GLA_PROMPT_HEADS_PER_STEP = 4
GLA_SAMPLE_BATCH = 8
SWA_SAMPLE_BATCH = 4
ROW_CHUNK = 128


def _dot(a, b):
    return jnp.dot(a, b, preferred_element_type=F32)


def _dot_nt(a, b):
    return lax.dot_general(a, b, (((1,), (1,)), ((), ())), preferred_element_type=F32)


def _dot_tn(a, b):
    return lax.dot_general(a, b, (((0,), (0,)), ((), ())), preferred_element_type=F32)


def _sigmoid(x):
    return 1.0 / (1.0 + jnp.exp(-x))


def _silu(x):
    return x * _sigmoid(x)


def _split3(x):
    hi = x.astype(BF16)
    r1 = x - hi.astype(F32)
    mid = r1.astype(BF16)
    lo = (r1 - mid.astype(F32)).astype(BF16)
    return hi, mid, lo


def _params(*sem):
    return pltpu.CompilerParams(dimension_semantics=sem, vmem_limit_bytes=VMEM_LIMIT_BYTES)


def _ada_body(c_ref, w_ref, b_ref, o_ref):
    a = _silu(c_ref[...]).astype(BF16)
    o_ref[...] = _dot(a, w_ref[...].astype(BF16)) + b_ref[...]


def _ada(c_all, w_ada, b_ada, tn=512):
    m, d = c_all.shape
    n = w_ada.shape[1]
    return pl.pallas_call(
        _ada_body,
        grid=(n // tn,),
        in_specs=[pl.BlockSpec((m, d), lambda j: (0, 0)),
                  pl.BlockSpec((d, tn), lambda j: (0, j)),
                  pl.BlockSpec((1, tn), lambda j: (0, j))],
        out_specs=pl.BlockSpec((m, tn), lambda j: (0, j)),
        out_shape=jax.ShapeDtypeStruct((m, n), F32),
        compiler_params=_params("arbitrary"),
        name="ada",
    )(c_all, w_ada, b_ada)


def _mod_spec(mod, tm, width, col):
    if mod.shape[0] == 1:
        return pl.BlockSpec((1, width), lambda i, j: (0, col(j)))
    return pl.BlockSpec((tm // DEC_SEQ, width), lambda i, j: (i, col(j)))


def _mod_rows(ref, row0, nrows):
    if ref.shape[0] == 1:
        return ref[...]
    seq0 = row0 // DEC_SEQ
    return jnp.concatenate(
        [jnp.broadcast_to(ref[pl.ds(seq0 + s, 1), :], (DEC_SEQ, ref.shape[1])) for s in range(nrows // DEC_SEQ)],
        axis=0)


def _modulated_norm_rows(x_ref, gn_ref, sc_ref, sh_ref, row0, nrows):
    x = x_ref[pl.ds(row0, nrows), :]
    ms = jnp.mean(x * x, axis=-1, keepdims=True)
    y = (x * lax.rsqrt(ms + EPS)) * gn_ref[...]
    return y * (1.0 + _mod_rows(sc_ref, row0, nrows)) + _mod_rows(sh_ref, row0, nrows)


def _row_loop(total_rows, body):
    def step(r, carry):
        body(pl.multiple_of(r * ROW_CHUNK, ROW_CHUNK))
        return carry
    lax.fori_loop(0, total_rows // ROW_CHUNK, step, 0)


def _inproj_body(x_ref, gn_ref, sc_ref, sh_ref, wt_ref, wlr_ref, wgk_ref, bgk_ref, *rest, n_casts):
    cast_src = rest[:n_casts]
    proj_ref, glog_ref = rest[n_casts:n_casts + 2]
    cast_dst = rest[n_casts + 2:2 * n_casts + 2]
    h_scr = rest[-1]
    for src, dst in zip(cast_src, cast_dst):
        dst[...] = src[...].astype(BF16)

    @pl.when(pl.program_id(1) == 0)
    def _():
        def rows(row0):
            hb = _modulated_norm_rows(x_ref, gn_ref, sc_ref, sh_ref, row0, ROW_CHUNK).astype(BF16)
            h_scr[pl.ds(row0, ROW_CHUNK), :] = hb
            glr = _dot_nt(hb, wlr_ref[...])
            z = _dot(glr.astype(BF16), wgk_ref[...]) + bgk_ref[...]
            log_sig = jnp.minimum(z, 0.0) - jnp.log1p(jnp.exp(-jnp.abs(z)))
            glog_ref[pl.ds(row0, ROW_CHUNK), :] = log_sig * (1.0 / GLA_TAU)
        _row_loop(x_ref.shape[0], rows)

    proj_ref[...] = _dot_nt(h_scr[...], wt_ref[...]).astype(BF16)


def _cast_block_spec(shape, row_tiles, col_steps):
    rows, cols = shape
    br = rows // row_tiles
    assert br * row_tiles == rows and br % (2 * SUBLANES) == 0
    ncb = max(c for c in range(1, col_steps + 1) if cols % c == 0 and (cols // c) % LANES == 0)
    return pl.BlockSpec((br, cols // ncb), lambda i, j: (i, jnp.minimum(j, ncb - 1)))


def _inproj(x, gn, scale, shift, wt, wt_lr, w_gk, b_gk, cast_weights=(), tm=2048, tn=512):
    t, d = x.shape
    tm = min(tm, t)
    n = wt.shape[0] - GLA_RANK
    n_first = W_IN_GLR // tn
    grid = (t // tm, n // tn)
    zero = lambda j: 0
    wt_rows = lambda i, j: (pl.multiple_of(j * tn + jnp.where(j >= n_first, GLA_RANK, 0), GLA_RANK), 0)
    cast_specs = [_cast_block_spec(w.shape, *grid) for w in cast_weights]
    return pl.pallas_call(
        functools.partial(_inproj_body, n_casts=len(cast_weights)),
        grid=grid,
        in_specs=[pl.BlockSpec((tm, d), lambda i, j: (i, 0), pipeline_mode=pl.Buffered(1)),
                  pl.BlockSpec((1, d), lambda i, j: (0, 0)),
                  _mod_spec(scale, tm, d, zero),
                  _mod_spec(shift, tm, d, zero),
                  pl.BlockSpec((pl.Element(tn), pl.Element(d)), wt_rows),
                  pl.BlockSpec(wt_lr.shape, lambda i, j: (0, 0)),
                  pl.BlockSpec(w_gk.shape, lambda i, j: (0, 0)),
                  pl.BlockSpec(b_gk.shape, lambda i, j: (0, 0))] + cast_specs,
        out_specs=[pl.BlockSpec((tm, tn), lambda i, j: (i, j)),
                   pl.BlockSpec((tm, GLA_KEY), lambda i, j: (i, 0))] + cast_specs,
        out_shape=[jax.ShapeDtypeStruct((t, n), BF16),
                   jax.ShapeDtypeStruct((t, GLA_KEY), F32)]
                  + [jax.ShapeDtypeStruct(w.shape, BF16) for w in cast_weights],
        scratch_shapes=[pltpu.VMEM((tm, d), BF16)],
        compiler_params=_params("arbitrary", "arbitrary"),
        name="inproj",
    )(x, gn, scale, shift, wt, wt_lr, w_gk, b_gk, *cast_weights)


def _gla_level_table(c, group):
    n_levels = int(np.log2(group))
    i = np.arange(c)[:, None]
    j = np.arange(c)[None, :]
    x = np.bitwise_xor(i, j)
    lvl = np.floor(np.log2(np.maximum(x, 1))).astype(np.int32)
    lvl = np.where(i == j, n_levels, lvl)
    valid = (i >= j) & (i // group == j // group)
    return np.where(valid, lvl, -1).astype(np.int32), n_levels


def _gla_tril(c, group):
    i = np.arange(c)[:, None]
    j = np.arange(c)[None, :]
    return ((i >= j) & (i // group == j // group)).astype(np.float32)


def _level_reference(b_scr, rows, level):
    s = 2 ** (level + 1)
    width = b_scr.shape[1]
    pieces = []
    if s >= SUBLANES:
        for blk in range(rows // s):
            mid = blk * s + s // 2
            pieces.append(jnp.broadcast_to(b_scr[mid:mid + 1, :], (s, width)))
    else:
        p = lax.broadcasted_iota(jnp.int32, (SUBLANES, width), 0)
        for tile in range(rows // SUBLANES):
            base = tile * SUBLANES
            mids = [base + q * s + s // 2 for q in range(SUBLANES // s)]
            r = jnp.broadcast_to(b_scr[mids[-1]:mids[-1] + 1, :], (SUBLANES, width))
            for q in range(SUBLANES // s - 2, -1, -1):
                row = jnp.broadcast_to(b_scr[mids[q]:mids[q] + 1, :], (SUBLANES, width))
                r = jnp.where(p < (q + 1) * s, row, r)
            pieces.append(r)
    return jnp.concatenate(pieces, axis=0) if len(pieces) > 1 else pieces[0]


def _gla_intra(q, k, k_bf, b, b_scr, lvl, n_levels):
    rows = q.shape[0]
    a = jnp.where(lvl == n_levels, _dot_nt(q.astype(BF16), k_bf), 0.0)
    for level in range(n_levels):
        f = jnp.exp(-jnp.abs(b - _level_reference(b_scr, rows, level)))
        p = _dot_nt((q * f).astype(BF16), (k * f).astype(BF16))
        a = jnp.where(lvl == level, p, a)
    return a


def _gla_cumsum(g, tril_bf):
    hi, mid, lo = _split3(g)
    return _dot(tril_bf, hi) + _dot(tril_bf, mid) + _dot(tril_bf, lo)


def _gla_out_gate(o, gout, gr):
    ms = jnp.mean(o * o, axis=-1, keepdims=True)
    y = (o * lax.rsqrt(ms + EPS)) * gout
    return (y * _silu(gr.astype(F32))).astype(BF16)


def _gla_prompt_body(q_ref, k_ref, v_ref, gr_ref, g_ref, tril_ref, lvl_ref, gout_ref,
                     oa_ref, st_ref, st_scr, b_scr, a_scr, *, n_levels, heads):
    c = pl.program_id(1)
    rows = q_ref.shape[0]

    @pl.when(c == 0)
    def _():
        st_scr[...] = jnp.zeros_like(st_scr)

    tril, lvl = tril_ref[...], lvl_ref[...]
    per_head = []
    mild = None
    for h in range(heads):
        ksl = slice(h * GLA_DK, (h + 1) * GLA_DK)
        b = _gla_cumsum(g_ref[:, ksl], tril)
        b_scr[h] = b
        b_end = b_scr[h, rows - 1:rows, :]
        q = q_ref[:, ksl].astype(F32) * (GLA_DK ** -0.5)
        k_bf = k_ref[:, ksl]
        k = k_bf.astype(F32)
        st = st_scr[h]
        qd = (q * jnp.exp(b)).astype(BF16)
        o_inter = _dot_nt(qd, st.astype(BF16))
        head_mild = jnp.min(b_end) > -GLA_ONE_SIDED_MAX_DECAY
        mild = head_mild if mild is None else jnp.logical_and(mild, head_mild)
        per_head.append((b, b_end, q, k, k_bf, st, qd, o_inter))

    @pl.when(mild)
    def _():
        for h, (b, _, _, k, _, _, qd, _) in enumerate(per_head):
            a_scr[h] = jnp.where(lvl >= 0, _dot_nt(qd, (k * jnp.exp(-b)).astype(BF16)), 0.0)

    @pl.when(jnp.logical_not(mild))
    def _():
        for h, (b, _, q, k, k_bf, _, _, _) in enumerate(per_head):
            a_scr[h] = _gla_intra(q, k, k_bf, b, b_scr.at[h], lvl, n_levels)

    for h, (b, b_end, _, k, _, st, _, o_inter) in enumerate(per_head):
        vsl = slice(h * GLA_DV, (h + 1) * GLA_DV)
        v = v_ref[:, vsl]
        o = o_inter + _dot(a_scr[h].astype(BF16), v)
        k_dec = (k * jnp.exp(b_end - b)).astype(BF16)
        st_scr[h] = st * jnp.exp(b_end) + _dot_tn(v, k_dec)
        oa_ref[:, vsl] = _gla_out_gate(o, gout_ref[...], gr_ref[:, vsl])

    @pl.when(c == pl.num_programs(1) - 1)
    def _():
        for h in range(heads):
            st_ref[h] = st_scr[h].T


def _gla_prompt(proj, glog, g_out, heads=GLA_PROMPT_HEADS_PER_STEP):
    t = proj.shape[0]
    c = GLA_CHUNK
    lvl, n_levels = _gla_level_table(c, c)
    tril = jnp.asarray(_gla_tril(c, c), BF16)
    lvl = jnp.asarray(lvl)
    wk, wv = heads * GLA_DK, heads * GLA_DV
    qb, kb, vb, rb = COL_GQ // wk, COL_GK // wk, COL_GV // wv, COL_GR // wv
    return pl.pallas_call(
        functools.partial(_gla_prompt_body, n_levels=n_levels, heads=heads),
        grid=(GLA_HEADS // heads, t // c),
        in_specs=[pl.BlockSpec((c, wk), lambda h, i: (i, qb + h)),
                  pl.BlockSpec((c, wk), lambda h, i: (i, kb + h)),
                  pl.BlockSpec((c, wv), lambda h, i: (i, vb + h)),
                  pl.BlockSpec((c, wv), lambda h, i: (i, rb + h)),
                  pl.BlockSpec((c, wk), lambda h, i: (i, h)),
                  pl.BlockSpec((c, c), lambda h, i: (0, 0)),
                  pl.BlockSpec((c, c), lambda h, i: (0, 0)),
                  pl.BlockSpec((1, GLA_DV), lambda h, i: (0, 0))],
        out_specs=[pl.BlockSpec((c, wv), lambda h, i: (i, h)),
                   pl.BlockSpec((heads, GLA_DK, GLA_DV), lambda h, i: (h, 0, 0))],
        out_shape=[jax.ShapeDtypeStruct((t, GLA_VAL), BF16),
                   jax.ShapeDtypeStruct((GLA_HEADS, GLA_DK, GLA_DV), F32)],
        scratch_shapes=[pltpu.VMEM((heads, GLA_DV, GLA_DK), F32), pltpu.VMEM((heads, c, GLA_DK), F32),
                        pltpu.VMEM((heads, c, c), F32)],
        compiler_params=_params("arbitrary", "arbitrary"),
        name="gla_prompt",
    )(proj, proj, proj, proj, glog, tril, lvl, g_out)


def _gla_sample_body(q_ref, k_ref, v_ref, gr_ref, g_ref, s0_ref, tril_ref, lvl_ref, gout_ref,
                     oa_ref, s1_ref, b_scr, *, n_levels):
    rows = q_ref.shape[0]
    nseq = rows // DEC_SEQ
    b = _gla_cumsum(g_ref[...], tril_ref[...])
    b_scr[...] = b
    b_end = jnp.concatenate(
        [jnp.broadcast_to(b_scr[(s + 1) * DEC_SEQ - 1:(s + 1) * DEC_SEQ, :], (DEC_SEQ, GLA_DK))
         for s in range(nseq)], axis=0)
    q = q_ref[...].astype(F32) * (GLA_DK ** -0.5)
    k_bf = k_ref[...]
    k = k_bf.astype(F32)
    v = v_ref[...]
    qd = (q * jnp.exp(b)).astype(BF16)
    a = _gla_intra(q, k, k_bf, b, b_scr, lvl_ref[...], n_levels)
    o_intra = _dot(a.astype(BF16), v)

    stacked = jnp.concatenate([k * jnp.exp(b_end - b), jnp.exp(b_end)], axis=0)
    stacked_t = stacked.T
    kd_t = stacked_t[:, :rows]
    lane = lax.broadcasted_iota(jnp.int32, kd_t.shape, 1)
    outs = []
    for s in range(nseq):
        s0 = s0_ref[s, 0]
        outs.append(_dot(qd[s * DEC_SEQ:(s + 1) * DEC_SEQ, :], s0.astype(BF16)))
        in_seq = (lane >= s * DEC_SEQ) & (lane < (s + 1) * DEC_SEQ)
        kd_s = jnp.where(in_seq, kd_t, 0.0).astype(BF16)
        col = rows + s * DEC_SEQ
        decay = stacked_t[:, col:col + 1]
        s1_ref[s, 0] = s0 * decay + _dot(kd_s, v)
    o = jnp.concatenate(outs, axis=0) + o_intra
    oa_ref[...] = _gla_out_gate(o, gout_ref[...], gr_ref[...])


def _gla_sample(proj, glog, state, g_out):
    t = proj.shape[0]
    nseq = GLA_SAMPLE_BATCH
    rows = nseq * DEC_SEQ
    lvl, n_levels = _gla_level_table(rows, DEC_SEQ)
    tril = jnp.asarray(_gla_tril(rows, DEC_SEQ), BF16)
    lvl = jnp.asarray(lvl)
    qb, kb, vb, rb = COL_GQ // GLA_DK, COL_GK // GLA_DK, COL_GV // GLA_DV, COL_GR // GLA_DV
    return pl.pallas_call(
        functools.partial(_gla_sample_body, n_levels=n_levels),
        grid=(t // rows, GLA_HEADS),
        in_specs=[pl.BlockSpec((rows, GLA_DK), lambda i, h: (i, qb + h)),
                  pl.BlockSpec((rows, GLA_DK), lambda i, h: (i, kb + h)),
                  pl.BlockSpec((rows, GLA_DV), lambda i, h: (i, vb + h)),
                  pl.BlockSpec((rows, GLA_DV), lambda i, h: (i, rb + h)),
                  pl.BlockSpec((rows, GLA_DK), lambda i, h: (i, h)),
                  pl.BlockSpec((nseq, 1, GLA_DK, GLA_DV), lambda i, h: (i, h, 0, 0)),
                  pl.BlockSpec((rows, rows), lambda i, h: (0, 0)),
                  pl.BlockSpec((rows, rows), lambda i, h: (0, 0)),
                  pl.BlockSpec((1, GLA_DV), lambda i, h: (0, 0))],
        out_specs=[pl.BlockSpec((rows, GLA_DV), lambda i, h: (i, h)),
                   pl.BlockSpec((nseq, 1, GLA_DK, GLA_DV), lambda i, h: (i, h, 0, 0))],
        out_shape=[jax.ShapeDtypeStruct((t, GLA_VAL), BF16),
                   jax.ShapeDtypeStruct(state.shape, F32)],
        scratch_shapes=[pltpu.VMEM((rows, GLA_DK), F32)],
        compiler_params=_params("arbitrary", "arbitrary"),
        name="gla_sample",
    )(proj, proj, proj, proj, glog, state, tril, lvl, g_out)


def _rope_tables(pos):
    half = SWA_HEAD_DIM // 2
    inv_freq = ROPE_THETA ** (-(jnp.arange(half, dtype=F32) * 2.0) / SWA_HEAD_DIM)
    ang = pos.astype(F32)[:, None] * inv_freq[None, :]
    cos, sin = jnp.cos(ang), jnp.sin(ang)
    cos_t = jnp.concatenate([cos, cos, cos, cos], axis=-1)
    sin_t = jnp.concatenate([-sin, sin, -sin, sin], axis=-1)
    return cos_t, sin_t


def _head_mean_matrix():
    i = np.arange(MXU_DIM)[:, None] // SWA_HEAD_DIM
    j = np.arange(MXU_DIM)[None, :] // SWA_HEAD_DIM
    return (i == j).astype(np.float32) / SWA_HEAD_DIM


def _qk_norm_rope(x, gain, cos_t, sin_t, mean_mat):
    rows, width = x.shape
    lane = lax.broadcasted_iota(jnp.int32, (rows, LANES), 1)
    first_half = (lane % SWA_HEAD_DIM) < (SWA_HEAD_DIM // 2)
    shift = SWA_HEAD_DIM // 2
    sq = (x * x).astype(BF16)
    tiles = []
    for c in range(width // MXU_DIM):
        ms = _dot(sq[:, c * MXU_DIM:(c + 1) * MXU_DIM], mean_mat)
        for t in range(MXU_DIM // LANES):
            lo = c * MXU_DIM + t * LANES
            y = (x[:, lo:lo + LANES] * lax.rsqrt(ms[:, t * LANES:(t + 1) * LANES] + EPS)) * gain
            rot = jnp.where(first_half, pltpu.roll(y, LANES - shift, 1), pltpu.roll(y, shift, 1))
            tiles.append(y * cos_t + rot * sin_t)
    return tiles


def _attend(q_tiles, k_tile, v_tile, parity_of_kv, valid, sinks_ref, kv_head):
    rows = q_tiles[0].shape[0]
    lane = lax.broadcasted_iota(jnp.int32, k_tile.shape, 1)
    low = lane < SWA_HEAD_DIM

    def place(x, want_low):
        have_low = parity_of_kv == 0
        src = x if have_low == want_low else pltpu.roll(x, SWA_HEAD_DIM, 1)
        return jnp.where(low if want_low else ~low, src, 0.0).astype(BF16)

    q_all = jnp.concatenate(q_tiles, axis=0).astype(BF16)
    npairs = len(q_tiles)
    outs = [None] * npairs
    for parity in range(2):
        kk = place(k_tile, parity == 0)
        vv = place(v_tile, parity == 0)
        s_all = _dot_nt(q_all, kk)
        p_list, inv_list = [], []
        for pair in range(npairs):
            sink = sinks_ref[kv_head * SWA_GROUP + 2 * pair + parity] * LOG2_E
            s = jnp.where(valid, s_all[pair * rows:(pair + 1) * rows, :], NEG_INF)
            m = jnp.maximum(jnp.max(s, axis=-1, keepdims=True), sink)
            p = jnp.exp2(s - m)
            denom = jnp.sum(p, axis=-1, keepdims=True) + jnp.exp2(sink - m)
            p_list.append(p.astype(BF16))
            inv_list.append(1.0 / denom)
        pv = _dot(jnp.concatenate(p_list, axis=0), vv)
        for pair in range(npairs):
            contrib = pv[pair * rows:(pair + 1) * rows, :] * inv_list[pair]
            outs[pair] = contrib if outs[pair] is None else outs[pair] + contrib
    return outs


def _attend_all_heads(q_tiles, k_all, v_all, valid, sinks_ref, store):
    pairs = SWA_GROUP // 2
    for g in range(SWA_KV_HEADS):
        lo = (g // 2) * LANES
        outs = _attend(q_tiles[g * pairs:(g + 1) * pairs], k_all[:, lo:lo + LANES], v_all[:, lo:lo + LANES],
                       g % 2, valid, sinks_ref, g)
        for p, o in enumerate(outs):
            store(g * pairs + p, o.astype(BF16))


def _swa_prompt_body(sinks_ref, q_ref, k_ref, v_ref, cb_ref, sb_ref, cr_ref, sr_ref, crs_ref, srs_ref,
                     gq_ref, gk_ref, mm_ref, ob_ref, knew_ref, vnew_ref, kprev_scr, vprev_scr):
    n = pl.program_id(0)
    blk = q_ref.shape[0]

    @pl.when(n == 0)
    def _():
        kprev_scr[...] = jnp.zeros_like(kprev_scr)
        vprev_scr[...] = jnp.zeros_like(vprev_scr)

    cb, sb = cb_ref[pl.ds(n, 1), :], sb_ref[pl.ds(n, 1), :]
    cos_t = cb * cr_ref[...] - sb * sr_ref[...]
    sin_t = sb * crs_ref[...] + cb * srs_ref[...]
    mean_mat = mm_ref[...]
    q_tiles = [t * SWA_SCORE_SCALE
               for t in _qk_norm_rope(q_ref[...].astype(F32), gq_ref[...], cos_t, sin_t, mean_mat)]
    kn = jnp.concatenate(_qk_norm_rope(k_ref[...].astype(F32), gk_ref[...], cos_t, sin_t, mean_mat), axis=1)
    vn = v_ref[...].astype(F32)
    knew_ref[...] = kn
    vnew_ref[...] = vn
    k_all = jnp.concatenate([kprev_scr[...], kn], axis=0)
    v_all = jnp.concatenate([vprev_scr[...], vn], axis=0)

    qi = lax.broadcasted_iota(jnp.int32, (blk, 2 * blk), 0)
    kj = lax.broadcasted_iota(jnp.int32, (blk, 2 * blk), 1)
    dist = blk + qi - kj
    valid = (dist >= 0) & (dist <= WINDOW) & ((n - 1) * blk + kj >= 0)

    def store(tile, value):
        ob_ref[:, tile * LANES:(tile + 1) * LANES] = value
    _attend_all_heads(q_tiles, k_all, v_all, valid, sinks_ref, store)

    kprev_scr[...] = kn
    vprev_scr[...] = vn


def _rope_block_tables(nblocks, blk):
    half = SWA_HEAD_DIM // 2
    inv_freq = ROPE_THETA ** (-(jnp.arange(half, dtype=F32) * 2.0) / SWA_HEAD_DIM)
    lane_freq = jnp.tile(inv_freq, LANES // half)[None, :]
    sign = jnp.tile(jnp.concatenate([-jnp.ones((half,), F32), jnp.ones((half,), F32)]), LANES // SWA_HEAD_DIM)[None, :]
    ang_b = (jnp.arange(nblocks, dtype=jnp.int32) * blk).astype(F32)[:, None] * lane_freq
    ang_r = jnp.arange(blk, dtype=jnp.int32).astype(F32)[:, None] * lane_freq
    cr, sr = jnp.cos(ang_r), jnp.sin(ang_r)
    return jnp.cos(ang_b), jnp.sin(ang_b), cr, sr, sign * cr, sign * sr


def _swa_prompt(proj, sinks, g_q, g_k):
    t = proj.shape[0]
    blk = WINDOW
    nblocks = t // blk
    mean_mat = jnp.asarray(_head_mean_matrix(), BF16)
    tables = _rope_block_tables(nblocks, blk)
    qb, kb, vb = COL_SQ // SWA_Q, COL_SK // SWA_KV, COL_SV // SWA_KV
    whole = lambda n, s: (0, 0)
    grid_spec = pltpu.PrefetchScalarGridSpec(
        num_scalar_prefetch=1,
        grid=(nblocks,),
        in_specs=[pl.BlockSpec((blk, SWA_Q), lambda n, s: (n, qb)),
                  pl.BlockSpec((blk, SWA_KV), lambda n, s: (n, kb)),
                  pl.BlockSpec((blk, SWA_KV), lambda n, s: (n, vb)),
                  pl.BlockSpec((nblocks, LANES), whole),
                  pl.BlockSpec((nblocks, LANES), whole),
                  pl.BlockSpec((blk, LANES), whole),
                  pl.BlockSpec((blk, LANES), whole),
                  pl.BlockSpec((blk, LANES), whole),
                  pl.BlockSpec((blk, LANES), whole),
                  pl.BlockSpec((1, LANES), lambda n, s: (0, 0)),
                  pl.BlockSpec((1, LANES), lambda n, s: (0, 0)),
                  pl.BlockSpec((MXU_DIM, MXU_DIM), lambda n, s: (0, 0))],
        out_specs=[pl.BlockSpec((blk, SWA_Q), lambda n, s: (n, 0)),
                   pl.BlockSpec((blk, SWA_KV), lambda n, s: (0, 0)),
                   pl.BlockSpec((blk, SWA_KV), lambda n, s: (0, 0))],
        scratch_shapes=[pltpu.VMEM((blk, SWA_KV), F32), pltpu.VMEM((blk, SWA_KV), F32)],
    )
    return pl.pallas_call(
        _swa_prompt_body,
        grid_spec=grid_spec,
        out_shape=[jax.ShapeDtypeStruct((t, SWA_Q), BF16),
                   jax.ShapeDtypeStruct((blk, SWA_KV), F32),
                   jax.ShapeDtypeStruct((blk, SWA_KV), F32)],
        compiler_params=_params("arbitrary"),
        name="swa_prompt",
    )(sinks, proj, proj, proj, *tables, g_q, g_k, mean_mat)


def _swa_sample_body(sinks_ref, q_ref, k_ref, v_ref, bk_ref, bv_ref, cos_ref, sin_ref, gq_ref, gk_ref, mm_ref,
                     ob_ref, kout_ref, vout_ref):
    nseq = bk_ref.shape[0]
    cos_t = jnp.concatenate([cos_ref[...]] * nseq, axis=0)
    sin_t = jnp.concatenate([sin_ref[...]] * nseq, axis=0)
    mean_mat = mm_ref[...]
    q_tiles = [t * SWA_SCORE_SCALE
               for t in _qk_norm_rope(q_ref[...].astype(F32), gq_ref[...], cos_t, sin_t, mean_mat)]
    kn = jnp.concatenate(_qk_norm_rope(k_ref[...].astype(F32), gk_ref[...], cos_t, sin_t, mean_mat), axis=1)
    vn = v_ref[...].astype(F32)

    keys = 2 * WINDOW
    qi = lax.broadcasted_iota(jnp.int32, (DEC_SEQ, keys), 0)
    kj = lax.broadcasted_iota(jnp.int32, (DEC_SEQ, keys), 1)
    dist = WINDOW + qi - kj
    valid = (dist >= 0) & (dist <= WINDOW)
    valid = jnp.concatenate([valid] * nseq, axis=0)
    pad = jnp.zeros((keys - WINDOW - DEC_SEQ, SWA_KV), F32)
    rows = nseq * DEC_SEQ

    k_parts, v_parts = [], []
    for s in range(nseq):
        rsl = slice(s * DEC_SEQ, (s + 1) * DEC_SEQ)
        bk, bv = bk_ref[s], bv_ref[s]
        kout_ref[s, :WINDOW - DEC_SEQ, :] = bk[DEC_SEQ:, :]
        kout_ref[s, WINDOW - DEC_SEQ:, :] = kn[rsl, :]
        vout_ref[s, :WINDOW - DEC_SEQ, :] = bv[DEC_SEQ:, :]
        vout_ref[s, WINDOW - DEC_SEQ:, :] = vn[rsl, :]
        k_parts += [bk, kn[rsl, :], pad]
        v_parts += [bv, vn[rsl, :], pad]
    k_stack = jnp.concatenate(k_parts, axis=0)
    v_stack = jnp.concatenate(v_parts, axis=0)

    low = lax.broadcasted_iota(jnp.int32, (nseq * keys, LANES), 1) < SWA_HEAD_DIM
    pairs = SWA_GROUP // 2
    zero_blk = jnp.zeros((DEC_SEQ, keys), F32)
    for g in range(SWA_KV_HEADS):
        lo = (g // 2) * LANES
        kv_low = g % 2 == 0
        keep = low if kv_low else jnp.logical_not(low)
        kk = jnp.where(keep, k_stack[:, lo:lo + LANES], 0.0).astype(BF16)
        vv = jnp.where(keep, v_stack[:, lo:lo + LANES], 0.0).astype(BF16)
        q_g = jnp.concatenate(q_tiles[g * pairs:(g + 1) * pairs], axis=0)
        out = None
        for parity in range(2):
            aligned = (parity == 0) == kv_low
            q_use = q_g if aligned else pltpu.roll(q_g, SWA_HEAD_DIM, 1)
            s_all = _dot_nt(q_use.astype(BF16), kk)
            p_rows, inv = [], []
            for pair in range(pairs):
                sink = sinks_ref[g * SWA_GROUP + 2 * pair + parity] * LOG2_E
                sc = jnp.concatenate(
                    [s_all[pair * rows + s * DEC_SEQ: pair * rows + (s + 1) * DEC_SEQ, s * keys:(s + 1) * keys]
                     for s in range(nseq)], axis=0)
                sc = jnp.where(valid, sc, NEG_INF)
                m = jnp.maximum(jnp.max(sc, axis=-1, keepdims=True), sink)
                p = jnp.exp2(sc - m)
                inv.append(1.0 / (jnp.sum(p, axis=-1, keepdims=True) + jnp.exp2(sink - m)))
                for s in range(nseq):
                    blocks = [p[s * DEC_SEQ:(s + 1) * DEC_SEQ, :] if s2 == s else zero_blk for s2 in range(nseq)]
                    p_rows.append(jnp.concatenate(blocks, axis=1))
            pv = _dot(jnp.concatenate(p_rows, axis=0).astype(BF16), vv)
            pv = jnp.concatenate([pv[pair * rows:(pair + 1) * rows, :] * inv[pair] for pair in range(pairs)], axis=0)
            if not aligned:
                pv = pltpu.roll(pv, SWA_HEAD_DIM, 1)
            out = pv if out is None else out + pv
        for pair in range(pairs):
            tile = g * pairs + pair
            ob_ref[:, tile * LANES:(tile + 1) * LANES] = out[pair * rows:(pair + 1) * rows, :].astype(BF16)


def _swa_sample(proj, buf_k, buf_v, sinks, cos_t, sin_t, g_q, g_k):
    t = proj.shape[0]
    nseq = SWA_SAMPLE_BATCH
    rows = nseq * DEC_SEQ
    mean_mat = jnp.asarray(_head_mean_matrix(), BF16)
    qb, kb, vb = COL_SQ // SWA_Q, COL_SK // SWA_KV, COL_SV // SWA_KV
    grid_spec = pltpu.PrefetchScalarGridSpec(
        num_scalar_prefetch=1,
        grid=(t // rows,),
        in_specs=[pl.BlockSpec((rows, SWA_Q), lambda n, s: (n, qb)),
                  pl.BlockSpec((rows, SWA_KV), lambda n, s: (n, kb)),
                  pl.BlockSpec((rows, SWA_KV), lambda n, s: (n, vb)),
                  pl.BlockSpec((nseq, WINDOW, SWA_KV), lambda n, s: (n, 0, 0)),
                  pl.BlockSpec((nseq, WINDOW, SWA_KV), lambda n, s: (n, 0, 0)),
                  pl.BlockSpec((DEC_SEQ, LANES), lambda n, s: (0, 0)),
                  pl.BlockSpec((DEC_SEQ, LANES), lambda n, s: (0, 0)),
                  pl.BlockSpec((1, LANES), lambda n, s: (0, 0)),
                  pl.BlockSpec((1, LANES), lambda n, s: (0, 0)),
                  pl.BlockSpec((MXU_DIM, MXU_DIM), lambda n, s: (0, 0))],
        out_specs=[pl.BlockSpec((rows, SWA_Q), lambda n, s: (n, 0)),
                   pl.BlockSpec((nseq, WINDOW, SWA_KV), lambda n, s: (n, 0, 0)),
                   pl.BlockSpec((nseq, WINDOW, SWA_KV), lambda n, s: (n, 0, 0))],
    )
    return pl.pallas_call(
        _swa_sample_body,
        grid_spec=grid_spec,
        out_shape=[jax.ShapeDtypeStruct((t, SWA_Q), BF16),
                   jax.ShapeDtypeStruct(buf_k.shape, F32),
                   jax.ShapeDtypeStruct(buf_v.shape, F32)],
        compiler_params=_params("arbitrary"),
        name="swa_sample",
    )(sinks, proj, proj, proj, buf_k, buf_v, cos_t, sin_t, g_q, g_k, mean_mat)


def _merge_body(oa_ref, ob_ref, wpa_ref, wpb_ref, ga_ref, gb_ref, m_ref):
    ga = _sigmoid(ga_ref[...].astype(F32))
    gb = _sigmoid(gb_ref[...].astype(F32))
    m_ref[...] = (ga * _dot(oa_ref[...], wpa_ref[...]) + gb * _dot(ob_ref[...], wpb_ref[...])).astype(BF16)


def _merge(oa, ob, proj, w_pa, w_pb, tm=1024, tn=512):
    t, d = oa.shape
    ga_blk, gb_blk = COL_BG // tn, (COL_BG + D_MODEL) // tn
    return pl.pallas_call(
        _merge_body,
        grid=(t // tm, d // tn),
        in_specs=[pl.BlockSpec((tm, d), lambda i, j: (i, 0)),
                  pl.BlockSpec((tm, d), lambda i, j: (i, 0)),
                  pl.BlockSpec((d, tn), lambda i, j: (0, j)),
                  pl.BlockSpec((d, tn), lambda i, j: (0, j)),
                  pl.BlockSpec((tm, tn), lambda i, j: (i, ga_blk + j)),
                  pl.BlockSpec((tm, tn), lambda i, j: (i, gb_blk + j))],
        out_specs=pl.BlockSpec((tm, tn), lambda i, j: (i, j)),
        out_shape=jax.ShapeDtypeStruct((t, d), BF16),
        compiler_params=_params("arbitrary", "arbitrary"),
        name="merge",
    )(oa, ob, w_pa, w_pb, proj, proj)


def _oproj_body(m_ref, w_ref, x_ref, gate_ref, o_ref, y_scr):
    if gate_ref.shape[0] == 1:
        o_ref[...] = x_ref[...] + gate_ref[...] * _dot(m_ref[...], w_ref[...])
        return
    y_scr[...] = _dot(m_ref[...], w_ref[...])

    def rows(row0):
        sl = pl.ds(row0, ROW_CHUNK)
        o_ref[sl, :] = x_ref[sl, :] + _mod_rows(gate_ref, row0, ROW_CHUNK) * y_scr[sl, :]
    _row_loop(x_ref.shape[0], rows)


def _oproj(merged, w_o, x, gate, tm=2048, tn=512):
    t, d = x.shape
    tm = min(tm, t)
    return pl.pallas_call(
        _oproj_body,
        grid=(t // tm, d // tn),
        in_specs=[pl.BlockSpec((tm, d), lambda i, j: (i, 0)),
                  pl.BlockSpec((d, tn), lambda i, j: (0, j)),
                  pl.BlockSpec((tm, tn), lambda i, j: (i, j)),
                  _mod_spec(gate, tm, tn, lambda j: j)],
        out_specs=pl.BlockSpec((tm, tn), lambda i, j: (i, j)),
        out_shape=jax.ShapeDtypeStruct(x.shape, F32),
        scratch_shapes=[pltpu.VMEM((tm, tn), F32)],
        compiler_params=_params("arbitrary", "arbitrary"),
        name="oproj",
    )(merged, w_o, x, gate)


def _ffn_body(x_ref, gn_ref, sc_ref, sh_ref, gate_ref, wg_ref, wu_ref, wd_ref, o_ref, h_scr):
    f = pl.program_id(1)

    @pl.when(f == 0)
    def _():
        def rows(row0):
            h_scr[pl.ds(row0, ROW_CHUNK), :] = _modulated_norm_rows(
                x_ref, gn_ref, sc_ref, sh_ref, row0, ROW_CHUNK).astype(BF16)
            o_ref[pl.ds(row0, ROW_CHUNK), :] = jnp.zeros((ROW_CHUNK, o_ref.shape[1]), F32)
        _row_loop(x_ref.shape[0], rows)

    h = h_scr[...]
    a = _silu(_dot(h, wg_ref[...])) * _dot(h, wu_ref[...])
    o_ref[...] += _dot(a.astype(BF16), wd_ref[...])

    @pl.when(f == pl.num_programs(1) - 1)
    def _():
        def rows(row0):
            sl = pl.ds(row0, ROW_CHUNK)
            o_ref[sl, :] = x_ref[sl, :] + _mod_rows(gate_ref, row0, ROW_CHUNK) * o_ref[sl, :]
        _row_loop(x_ref.shape[0], rows)


def _ffn(x, gn, scale, shift, gate, w_gate, w_up, w_down, tm=1024, tf=512):
    t, d = x.shape
    dff = w_gate.shape[1]
    zero = lambda j: 0
    return pl.pallas_call(
        _ffn_body,
        grid=(t // tm, dff // tf),
        in_specs=[pl.BlockSpec((tm, d), lambda i, f: (i, 0)),
                  pl.BlockSpec((1, d), lambda i, f: (0, 0)),
                  _mod_spec(scale, tm, d, zero),
                  _mod_spec(shift, tm, d, zero),
                  _mod_spec(gate, tm, d, zero),
                  pl.BlockSpec((d, tf), lambda i, f: (0, f)),
                  pl.BlockSpec((d, tf), lambda i, f: (0, f)),
                  pl.BlockSpec((tf, d), lambda i, f: (f, 0))],
        out_specs=pl.BlockSpec((tm, d), lambda i, f: (i, 0)),
        out_shape=jax.ShapeDtypeStruct(x.shape, F32),
        scratch_shapes=[pltpu.VMEM((tm, d), BF16)],
        compiler_params=_params("arbitrary", "arbitrary"),
        name="ffn",
    )(x, gn, scale, shift, gate, w_gate, w_up, w_down)


def _mods(ada):
    return tuple(ada[:, i * D_MODEL:(i + 1) * D_MODEL] for i in range(6))


def _layer_after_inproj(x, proj, glog, mods, pos, state, buf_k, buf_v, wts):
    g_q, g_k, sinks, g_out, w_pa, w_pb, w_o, gn_ffn, w_fg, w_fu, w_fd = wts
    bsz, tlen, d = x.shape
    x2 = x.reshape(bsz * tlen, d)
    _, _, gate_m, shift_f, scale_f, gate_f = mods
    if state is None:
        oa, s_new = _gla_prompt(proj, glog, g_out)
        s_new = s_new[None]
        ob, k_new, v_new = _swa_prompt(proj, sinks, g_q, g_k)
    else:
        cos_t, sin_t = _rope_tables(pos)
        oa, s_new = _gla_sample(proj, glog, state, g_out)
        ob, k_new, v_new = _swa_sample(proj, buf_k.reshape(bsz, WINDOW, SWA_KV), buf_v.reshape(bsz, WINDOW, SWA_KV),
                                       sinks, cos_t, sin_t, g_q, g_k)
    merged = _merge(oa, ob, proj, w_pa, w_pb)
    x1 = _oproj(merged, w_o, x2, gate_m)
    y = _ffn(x1, gn_ffn, scale_f, shift_f, gate_f, w_fg, w_fu, w_fd)
    return (y.reshape(bsz, tlen, d), s_new,
            k_new.reshape(bsz, WINDOW, SWA_KV_HEADS, SWA_HEAD_DIM),
            v_new.reshape(bsz, WINDOW, SWA_KV_HEADS, SWA_HEAD_DIM))


def kernel(x_prompt, x_sample, c_prompt, c_sample, state_gla, cache_swa_k, cache_swa_v, w_ada, b_ada, g_norm_mix, w_in, w_gk2, b_gk2, g_qnorm, g_knorm, sinks, g_gla_out, w_pa, w_pb, w_o, g_norm_ffn, w_ffn_gate, w_ffn_up, w_ffn_down):
    assert w_in.shape[0] == 1, "single trunk layer"
    pos_p = jnp.arange(SEQ, dtype=jnp.int32)
    pos_s = PAST_LEN + jnp.arange(DEC_SEQ, dtype=jnp.int32)
    l = 0
    wt = jnp.transpose(w_in[l]).astype(BF16)
    wt_lr = jnp.pad(wt[W_IN_GLR:W_IN_GLR + GLA_RANK], ((0, LANES - GLA_RANK), (0, 0)))
    w_gk = jnp.pad(w_gk2[l], ((0, LANES - GLA_RANK), (0, 0))).astype(BF16)
    in_wts = (wt, wt_lr, w_gk, b_gk2[l].reshape(1, GLA_KEY))
    gn_mix = g_norm_mix[l].reshape(1, D_MODEL)

    n_c = 1 + DEC_BATCH
    pad = (-n_c) % (2 * SUBLANES)
    c_all = jnp.pad(jnp.concatenate([c_prompt, c_sample], axis=0), ((0, pad), (0, 0)))
    ada = _ada(c_all, w_ada[l], b_ada[l].reshape(1, -1))
    mods_p, mods_s = _mods(ada[0:1]), _mods(ada[1:n_c])

    later_weights = (w_pa[l], w_pb[l], w_o[l], w_ffn_gate[l], w_ffn_up[l], w_ffn_down[l])
    proj_p, glog_p, *later_bf16 = _inproj(x_prompt.reshape(-1, D_MODEL), gn_mix, mods_p[1], mods_p[0], *in_wts,
                                          cast_weights=later_weights)
    proj_s, glog_s = _inproj(x_sample.reshape(-1, D_MODEL), gn_mix, mods_s[1], mods_s[0], *in_wts)
    w_pa_b, w_pb_b, w_o_b, w_fg_b, w_fu_b, w_fd_b = later_bf16
    wts = (jnp.tile(g_qnorm[l], 2).reshape(1, LANES), jnp.tile(g_knorm[l], 2).reshape(1, LANES),
           sinks[l], g_gla_out[l].reshape(1, GLA_DV), w_pa_b, w_pb_b, w_o_b,
           g_norm_ffn[l].reshape(1, D_MODEL), w_fg_b, w_fu_b, w_fd_b)

    yp, sp, kp, vp = _layer_after_inproj(x_prompt, proj_p, glog_p, mods_p, pos_p, None, None, None, wts)
    ys, ss, ks, vs = _layer_after_inproj(x_sample, proj_s, glog_s, mods_s, pos_s, state_gla[l], cache_swa_k[l],
                                         cache_swa_v[l], wts)
    return (yp, ys, sp[None], kp[None], vp[None], ss[None], ks[None], vs[None])
```

```python
import functools

import numpy as np
import jax
import jax.numpy as jnp
from jax import lax
from jax.experimental import pallas as pl
from jax.experimental.pallas import tpu as pltpu

F32 = jnp.float32
BF16 = jnp.bfloat16

D_MODEL = 2048
SEQ = 16384
DEC_BATCH = 128
DEC_SEQ = 8
PAST_LEN = 16384
GLA_HEADS = 4
GLA_DK = 256
GLA_DV = 512
GLA_KEY = GLA_HEADS * GLA_DK
GLA_VAL = GLA_HEADS * GLA_DV
GLA_RANK = 16
GLA_TAU = 16.0
SWA_HEAD_DIM = 64
SWA_HEADS = 32
SWA_KV_HEADS = 4
SWA_GROUP = SWA_HEADS // SWA_KV_HEADS
SWA_Q = SWA_HEADS * SWA_HEAD_DIM
SWA_KV = SWA_KV_HEADS * SWA_HEAD_DIM
WINDOW = 128
ROPE_THETA = 10000.0
D_FF = 5632
EPS = 1e-6
NEG_INF = -1e30
LOG2_E = 1.4426950408889634
SWA_SCORE_SCALE = (SWA_HEAD_DIM ** -0.5) * LOG2_E

SUBLANES = 8
LANES = 128
MXU_DIM = 256
VMEM_LIMIT_BYTES = 58 * 1024 * 1024

COL_GQ = 0
COL_GK = COL_GQ + GLA_KEY
COL_GV = COL_GK + GLA_KEY
COL_GR = COL_GV + GLA_VAL
COL_SQ = COL_GR + GLA_VAL
COL_SK = COL_SQ + SWA_Q
COL_SV = COL_SK + SWA_KV
COL_BG = COL_SV + SWA_KV
PROJ_COLS = COL_BG + 2 * D_MODEL
W_IN_GLR = COL_GR + GLA_VAL

GLA_CHUNK = 256
GLA_ONE_SIDED_MAX_DECAY = 80.0
GLA_SAMPLE_BATCH = 8
SWA_SAMPLE_BATCH = 4
ROW_CHUNK = 128


def _dot(a, b):
    return jnp.dot(a, b, preferred_element_type=F32)


def _dot_nt(a, b):
    return lax.dot_general(a, b, (((1,), (1,)), ((), ())), preferred_element_type=F32)


def _dot_tn(a, b):
    return lax.dot_general(a, b, (((0,), (0,)), ((), ())), preferred_element_type=F32)


def _sigmoid(x):
    return 1.0 / (1.0 + jnp.exp(-x))


def _silu(x):
    return x * _sigmoid(x)


def _split3(x):
    hi = x.astype(BF16)
    r1 = x - hi.astype(F32)
    mid = r1.astype(BF16)
    lo = (r1 - mid.astype(F32)).astype(BF16)
    return hi, mid, lo


def _params(*sem):
    return pltpu.CompilerParams(dimension_semantics=sem, vmem_limit_bytes=VMEM_LIMIT_BYTES)


def _ada_body(c_ref, w_ref, b_ref, o_ref):
    a = _silu(c_ref[...]).astype(BF16)
    o_ref[...] = _dot(a, w_ref[...].astype(BF16)) + b_ref[...]


def _ada(c_all, w_ada, b_ada, tn=512):
    m, d = c_all.shape
    n = w_ada.shape[1]
    return pl.pallas_call(
        _ada_body,
        grid=(n // tn,),
        in_specs=[pl.BlockSpec((m, d), lambda j: (0, 0)),
                  pl.BlockSpec((d, tn), lambda j: (0, j)),
                  pl.BlockSpec((1, tn), lambda j: (0, j))],
        out_specs=pl.BlockSpec((m, tn), lambda j: (0, j)),
        out_shape=jax.ShapeDtypeStruct((m, n), F32),
        compiler_params=_params("arbitrary"),
        name="ada",
    )(c_all, w_ada, b_ada)


def _mod_spec(mod, tm, width, col):
    if mod.shape[0] == 1:
        return pl.BlockSpec((1, width), lambda i, j: (0, col(j)))
    return pl.BlockSpec((tm // DEC_SEQ, width), lambda i, j: (i, col(j)))


def _mod_rows(ref, row0, nrows):
    if ref.shape[0] == 1:
        return ref[...]
    seq0 = row0 // DEC_SEQ
    return jnp.concatenate(
        [jnp.broadcast_to(ref[pl.ds(seq0 + s, 1), :], (DEC_SEQ, ref.shape[1])) for s in range(nrows // DEC_SEQ)],
        axis=0)


def _modulated_norm_rows(x_ref, gn_ref, sc_ref, sh_ref, row0, nrows):
    x = x_ref[pl.ds(row0, nrows), :]
    ms = jnp.mean(x * x, axis=-1, keepdims=True)
    y = (x * lax.rsqrt(ms + EPS)) * gn_ref[...]
    return y * (1.0 + _mod_rows(sc_ref, row0, nrows)) + _mod_rows(sh_ref, row0, nrows)


def _row_loop(total_rows, body):
    def step(r, carry):
        body(pl.multiple_of(r * ROW_CHUNK, ROW_CHUNK))
        return carry
    lax.fori_loop(0, total_rows // ROW_CHUNK, step, 0)


def _inproj_body(x_ref, gn_ref, sc_ref, sh_ref, wt_ref, wlr_ref, wgk_ref, bgk_ref, *rest, n_casts):
    cast_src = rest[:n_casts]
    proj_ref, glog_ref = rest[n_casts:n_casts + 2]
    cast_dst = rest[n_casts + 2:2 * n_casts + 2]
    h_scr = rest[-1]
    for src, dst in zip(cast_src, cast_dst):
        dst[...] = src[...].astype(BF16)

    @pl.when(pl.program_id(1) == 0)
    def _():
        def rows(row0):
            hb = _modulated_norm_rows(x_ref, gn_ref, sc_ref, sh_ref, row0, ROW_CHUNK).astype(BF16)
            h_scr[pl.ds(row0, ROW_CHUNK), :] = hb
            glr = _dot_nt(hb, wlr_ref[...])
            z = _dot(glr.astype(BF16), wgk_ref[...]) + bgk_ref[...]
            log_sig = jnp.minimum(z, 0.0) - jnp.log1p(jnp.exp(-jnp.abs(z)))
            glog_ref[pl.ds(row0, ROW_CHUNK), :] = log_sig * (1.0 / GLA_TAU)
        _row_loop(x_ref.shape[0], rows)

    proj_ref[...] = _dot_nt(h_scr[...], wt_ref[...]).astype(BF16)


def _cast_block_spec(shape, row_tiles, col_steps):
    rows, cols = shape
    br = rows // row_tiles
    assert br * row_tiles == rows and br % (2 * SUBLANES) == 0
    ncb = max(c for c in range(1, col_steps + 1) if cols % c == 0 and (cols // c) % LANES == 0)
    return pl.BlockSpec((br, cols // ncb), lambda i, j: (i, jnp.minimum(j, ncb - 1)))


def _inproj(x, gn, scale, shift, wt, wt_lr, w_gk, b_gk, cast_weights=(), tm=2048, tn=512):
    t, d = x.shape
    tm = min(tm, t)
    n = wt.shape[0] - GLA_RANK
    n_first = W_IN_GLR // tn
    grid = (t // tm, n // tn)
    zero = lambda j: 0
    wt_rows = lambda i, j: (pl.multiple_of(j * tn + jnp.where(j >= n_first, GLA_RANK, 0), GLA_RANK), 0)
    cast_specs = [_cast_block_spec(w.shape, *grid) for w in cast_weights]
    return pl.pallas_call(
        functools.partial(_inproj_body, n_casts=len(cast_weights)),
        grid=grid,
        in_specs=[pl.BlockSpec((tm, d), lambda i, j: (i, 0), pipeline_mode=pl.Buffered(1)),
                  pl.BlockSpec((1, d), lambda i, j: (0, 0)),
                  _mod_spec(scale, tm, d, zero),
                  _mod_spec(shift, tm, d, zero),
                  pl.BlockSpec((pl.Element(tn), pl.Element(d)), wt_rows),
                  pl.BlockSpec(wt_lr.shape, lambda i, j: (0, 0)),
                  pl.BlockSpec(w_gk.shape, lambda i, j: (0, 0)),
                  pl.BlockSpec(b_gk.shape, lambda i, j: (0, 0))] + cast_specs,
        out_specs=[pl.BlockSpec((tm, tn), lambda i, j: (i, j)),
                   pl.BlockSpec((tm, GLA_KEY), lambda i, j: (i, 0))] + cast_specs,
        out_shape=[jax.ShapeDtypeStruct((t, n), BF16),
                   jax.ShapeDtypeStruct((t, GLA_KEY), F32)]
                  + [jax.ShapeDtypeStruct(w.shape, BF16) for w in cast_weights],
        scratch_shapes=[pltpu.VMEM((tm, d), BF16)],
        compiler_params=_params("arbitrary", "arbitrary"),
        name="inproj",
    )(x, gn, scale, shift, wt, wt_lr, w_gk, b_gk, *cast_weights)


def _gla_level_table(c, group):
    n_levels = int(np.log2(group))
    i = np.arange(c)[:, None]
    j = np.arange(c)[None, :]
    x = np.bitwise_xor(i, j)
    lvl = np.floor(np.log2(np.maximum(x, 1))).astype(np.int32)
    lvl = np.where(i == j, n_levels, lvl)
    valid = (i >= j) & (i // group == j // group)
    return np.where(valid, lvl, -1).astype(np.int32), n_levels


def _gla_tril(c, group):
    i = np.arange(c)[:, None]
    j = np.arange(c)[None, :]
    return ((i >= j) & (i // group == j // group)).astype(np.float32)


def _level_reference(b_scr, rows, level):
    s = 2 ** (level + 1)
    width = b_scr.shape[1]
    pieces = []
    if s >= SUBLANES:
        for blk in range(rows // s):
            mid = blk * s + s // 2
            pieces.append(jnp.broadcast_to(b_scr[mid:mid + 1, :], (s, width)))
    else:
        p = lax.broadcasted_iota(jnp.int32, (SUBLANES, width), 0)
        for tile in range(rows // SUBLANES):
            base = tile * SUBLANES
            mids = [base + q * s + s // 2 for q in range(SUBLANES // s)]
            r = jnp.broadcast_to(b_scr[mids[-1]:mids[-1] + 1, :], (SUBLANES, width))
            for q in range(SUBLANES // s - 2, -1, -1):
                row = jnp.broadcast_to(b_scr[mids[q]:mids[q] + 1, :], (SUBLANES, width))
                r = jnp.where(p < (q + 1) * s, row, r)
            pieces.append(r)
    return jnp.concatenate(pieces, axis=0) if len(pieces) > 1 else pieces[0]


def _gla_intra(q, k, k_bf, b, b_scr, lvl, n_levels):
    rows = q.shape[0]
    a = jnp.where(lvl == n_levels, _dot_nt(q.astype(BF16), k_bf), 0.0)
    for level in range(n_levels):
        f = jnp.exp(-jnp.abs(b - _level_reference(b_scr, rows, level)))
        p = _dot_nt((q * f).astype(BF16), (k * f).astype(BF16))
        a = jnp.where(lvl == level, p, a)
    return a


def _gla_cumsum(g, tril_bf):
    hi, mid, lo = _split3(g)
    return _dot(tril_bf, hi) + _dot(tril_bf, mid) + _dot(tril_bf, lo)


def _gla_out_gate(o, gout, gr):
    ms = jnp.mean(o * o, axis=-1, keepdims=True)
    y = (o * lax.rsqrt(ms + EPS)) * gout
    return (y * _silu(gr.astype(F32))).astype(BF16)


def _gla_prompt_body(q_ref, k_ref, v_ref, gr_ref, g_ref, tril_ref, lvl_ref, gout_ref, wpa_ref,
                     pa_ref, st_ref, st_scr, b_scr, a_scr, oa_scr, *, n_levels, heads):
    c = pl.program_id(0)
    n_chunks = pl.num_programs(0) - 1
    rows = q_ref.shape[0]
    cur = lax.rem(c, 2)
    prev = 1 - cur
    half = pa_ref.shape[1] // 2

    @pl.when(c == 0)
    def _():
        st_scr[...] = jnp.zeros_like(st_scr)
        oa_scr[...] = jnp.zeros_like(oa_scr)

    pa_ref[:, :half] = _dot(oa_scr[prev], wpa_ref[:, :half])

    tril, lvl = tril_ref[...], lvl_ref[...]
    per_head = []
    mild = None
    for h in range(heads):
        ksl = slice(h * GLA_DK, (h + 1) * GLA_DK)
        b = _gla_cumsum(g_ref[:, ksl], tril)
        b_scr[h] = b
        b_end = b_scr[h, rows - 1:rows, :]
        q = q_ref[:, ksl].astype(F32) * (GLA_DK ** -0.5)
        k_bf = k_ref[:, ksl]
        k = k_bf.astype(F32)
        st = st_scr[h]
        qd = (q * jnp.exp(b)).astype(BF16)
        o_inter = _dot_nt(qd, st.astype(BF16))
        head_mild = jnp.min(b_end) > -GLA_ONE_SIDED_MAX_DECAY
        mild = head_mild if mild is None else jnp.logical_and(mild, head_mild)
        per_head.append((b, b_end, q, k, k_bf, st, qd, o_inter))

    @pl.when(mild)
    def _():
        for h, (b, _, _, k, _, _, qd, _) in enumerate(per_head):
            a_scr[h] = jnp.where(lvl >= 0, _dot_nt(qd, (k * jnp.exp(-b)).astype(BF16)), 0.0)

    @pl.when(jnp.logical_not(mild))
    def _():
        for h, (b, _, q, k, k_bf, _, _, _) in enumerate(per_head):
            a_scr[h] = _gla_intra(q, k, k_bf, b, b_scr.at[h], lvl, n_levels)

    pa_ref[:, half:] = _dot(oa_scr[prev], wpa_ref[:, half:])
    for h, (b, b_end, _, k, _, st, _, o_inter) in enumerate(per_head):
        vsl = slice(h * GLA_DV, (h + 1) * GLA_DV)
        v = v_ref[:, vsl]
        o = o_inter + _dot(a_scr[h].astype(BF16), v)
        k_dec = (k * jnp.exp(b_end - b)).astype(BF16)
        st_scr[h] = st * jnp.exp(b_end) + _dot_tn(v, k_dec)
        oa_scr[cur, :, vsl] = _gla_out_gate(o, gout_ref[...], gr_ref[:, vsl])

    @pl.when(c == n_chunks - 1)
    def _():
        for h in range(heads):
            st_ref[h] = st_scr[h].T


def _gla_prompt(proj, glog, g_out, w_pa):
    t = proj.shape[0]
    c = GLA_CHUNK
    n_chunks = t // c
    heads = GLA_HEADS
    lvl, n_levels = _gla_level_table(c, c)
    tril = jnp.asarray(_gla_tril(c, c), BF16)
    lvl = jnp.asarray(lvl)
    qb, kb, vb, rb = COL_GQ // GLA_KEY, COL_GK // GLA_KEY, COL_GV // GLA_VAL, COL_GR // GLA_VAL
    chunk = lambda i: jnp.minimum(i, n_chunks - 1)
    whole = lambda i: (0, 0)
    return pl.pallas_call(
        functools.partial(_gla_prompt_body, n_levels=n_levels, heads=heads),
        grid=(n_chunks + 1,),
        in_specs=[pl.BlockSpec((c, GLA_KEY), lambda i: (chunk(i), qb)),
                  pl.BlockSpec((c, GLA_KEY), lambda i: (chunk(i), kb)),
                  pl.BlockSpec((c, GLA_VAL), lambda i: (chunk(i), vb)),
                  pl.BlockSpec((c, GLA_VAL), lambda i: (chunk(i), rb)),
                  pl.BlockSpec((c, GLA_KEY), lambda i: (chunk(i), 0)),
                  pl.BlockSpec((c, c), whole),
                  pl.BlockSpec((c, c), whole),
                  pl.BlockSpec((1, GLA_DV), whole),
                  pl.BlockSpec(w_pa.shape, whole, pipeline_mode=pl.Buffered(1))],
        out_specs=[pl.BlockSpec((c, w_pa.shape[1]), lambda i: (jnp.maximum(i - 1, 0), 0)),
                   pl.BlockSpec((heads, GLA_DK, GLA_DV), lambda i: (0, 0, 0))],
        out_shape=[jax.ShapeDtypeStruct((t, w_pa.shape[1]), F32),
                   jax.ShapeDtypeStruct((heads, GLA_DK, GLA_DV), F32)],
        scratch_shapes=[pltpu.VMEM((heads, GLA_DV, GLA_DK), F32), pltpu.VMEM((heads, c, GLA_DK), F32),
                        pltpu.VMEM((heads, c, c), F32), pltpu.VMEM((2, c, GLA_VAL), BF16)],
        compiler_params=_params("arbitrary"),
        name="gla_prompt",
    )(proj, proj, proj, proj, glog, tril, lvl, g_out, w_pa)


def _gla_sample_body(q_ref, k_ref, v_ref, gr_ref, g_ref, s0_ref, tril_ref, lvl_ref, gout_ref,
                     oa_ref, s1_ref, b_scr, *, n_levels):
    rows = q_ref.shape[0]
    nseq = rows // DEC_SEQ
    b = _gla_cumsum(g_ref[...], tril_ref[...])
    b_scr[...] = b
    b_end = jnp.concatenate(
        [jnp.broadcast_to(b_scr[(s + 1) * DEC_SEQ - 1:(s + 1) * DEC_SEQ, :], (DEC_SEQ, GLA_DK))
         for s in range(nseq)], axis=0)
    q = q_ref[...].astype(F32) * (GLA_DK ** -0.5)
    k_bf = k_ref[...]
    k = k_bf.astype(F32)
    v = v_ref[...]
    qd = (q * jnp.exp(b)).astype(BF16)
    a = _gla_intra(q, k, k_bf, b, b_scr, lvl_ref[...], n_levels)
    o_intra = _dot(a.astype(BF16), v)

    stacked = jnp.concatenate([k * jnp.exp(b_end - b), jnp.exp(b_end)], axis=0)
    stacked_t = stacked.T
    kd_t = stacked_t[:, :rows]
    lane = lax.broadcasted_iota(jnp.int32, kd_t.shape, 1)
    outs = []
    for s in range(nseq):
        s0 = s0_ref[s, 0]
        outs.append(_dot(qd[s * DEC_SEQ:(s + 1) * DEC_SEQ, :], s0.astype(BF16)))
        in_seq = (lane >= s * DEC_SEQ) & (lane < (s + 1) * DEC_SEQ)
        kd_s = jnp.where(in_seq, kd_t, 0.0).astype(BF16)
        col = rows + s * DEC_SEQ
        decay = stacked_t[:, col:col + 1]
        s1_ref[s, 0] = s0 * decay + _dot(kd_s, v)
    o = jnp.concatenate(outs, axis=0) + o_intra
    oa_ref[...] = _gla_out_gate(o, gout_ref[...], gr_ref[...])


def _gla_sample(proj, glog, state, g_out):
    t = proj.shape[0]
    nseq = GLA_SAMPLE_BATCH
    rows = nseq * DEC_SEQ
    lvl, n_levels = _gla_level_table(rows, DEC_SEQ)
    tril = jnp.asarray(_gla_tril(rows, DEC_SEQ), BF16)
    lvl = jnp.asarray(lvl)
    qb, kb, vb, rb = COL_GQ // GLA_DK, COL_GK // GLA_DK, COL_GV // GLA_DV, COL_GR // GLA_DV
    return pl.pallas_call(
        functools.partial(_gla_sample_body, n_levels=n_levels),
        grid=(t // rows, GLA_HEADS),
        in_specs=[pl.BlockSpec((rows, GLA_DK), lambda i, h: (i, qb + h)),
                  pl.BlockSpec((rows, GLA_DK), lambda i, h: (i, kb + h)),
                  pl.BlockSpec((rows, GLA_DV), lambda i, h: (i, vb + h)),
                  pl.BlockSpec((rows, GLA_DV), lambda i, h: (i, rb + h)),
                  pl.BlockSpec((rows, GLA_DK), lambda i, h: (i, h)),
                  pl.BlockSpec((nseq, 1, GLA_DK, GLA_DV), lambda i, h: (i, h, 0, 0)),
                  pl.BlockSpec((rows, rows), lambda i, h: (0, 0)),
                  pl.BlockSpec((rows, rows), lambda i, h: (0, 0)),
                  pl.BlockSpec((1, GLA_DV), lambda i, h: (0, 0))],
        out_specs=[pl.BlockSpec((rows, GLA_DV), lambda i, h: (i, h)),
                   pl.BlockSpec((nseq, 1, GLA_DK, GLA_DV), lambda i, h: (i, h, 0, 0))],
        out_shape=[jax.ShapeDtypeStruct((t, GLA_VAL), BF16),
                   jax.ShapeDtypeStruct(state.shape, F32)],
        scratch_shapes=[pltpu.VMEM((rows, GLA_DK), F32)],
        compiler_params=_params("arbitrary", "arbitrary"),
        name="gla_sample",
    )(proj, proj, proj, proj, glog, state, tril, lvl, g_out)


def _rope_tables(pos):
    half = SWA_HEAD_DIM // 2
    inv_freq = ROPE_THETA ** (-(jnp.arange(half, dtype=F32) * 2.0) / SWA_HEAD_DIM)
    ang = pos.astype(F32)[:, None] * inv_freq[None, :]
    cos, sin = jnp.cos(ang), jnp.sin(ang)
    cos_t = jnp.concatenate([cos, cos, cos, cos], axis=-1)
    sin_t = jnp.concatenate([-sin, sin, -sin, sin], axis=-1)
    return cos_t, sin_t


def _head_mean_matrix():
    i = np.arange(MXU_DIM)[:, None] // SWA_HEAD_DIM
    j = np.arange(MXU_DIM)[None, :] // SWA_HEAD_DIM
    return (i == j).astype(np.float32) / SWA_HEAD_DIM


def _qk_norm_rope(x, gain, cos_t, sin_t, mean_mat):
    rows, width = x.shape
    lane = lax.broadcasted_iota(jnp.int32, (rows, LANES), 1)
    first_half = (lane % SWA_HEAD_DIM) < (SWA_HEAD_DIM // 2)
    shift = SWA_HEAD_DIM // 2
    sq = (x * x).astype(BF16)
    tiles = []
    for c in range(width // MXU_DIM):
        ms = _dot(sq[:, c * MXU_DIM:(c + 1) * MXU_DIM], mean_mat)
        for t in range(MXU_DIM // LANES):
            lo = c * MXU_DIM + t * LANES
            y = (x[:, lo:lo + LANES] * lax.rsqrt(ms[:, t * LANES:(t + 1) * LANES] + EPS)) * gain
            rot = jnp.where(first_half, pltpu.roll(y, LANES - shift, 1), pltpu.roll(y, shift, 1))
            tiles.append(y * cos_t + rot * sin_t)
    return tiles


def _attend(q_tiles, k_tile, v_tile, parity_of_kv, valid, sinks_ref, kv_head):
    rows = q_tiles[0].shape[0]
    lane = lax.broadcasted_iota(jnp.int32, k_tile.shape, 1)
    low = lane < SWA_HEAD_DIM

    def place(x, want_low):
        have_low = parity_of_kv == 0
        src = x if have_low == want_low else pltpu.roll(x, SWA_HEAD_DIM, 1)
        return jnp.where(low if want_low else ~low, src, 0.0).astype(BF16)

    q_all = jnp.concatenate(q_tiles, axis=0).astype(BF16)
    npairs = len(q_tiles)
    outs = [None] * npairs
    for parity in range(2):
        kk = place(k_tile, parity == 0)
        vv = place(v_tile, parity == 0)
        s_all = _dot_nt(q_all, kk)
        p_list, inv_list = [], []
        for pair in range(npairs):
            sink = sinks_ref[kv_head * SWA_GROUP + 2 * pair + parity] * LOG2_E
            s = jnp.where(valid, s_all[pair * rows:(pair + 1) * rows, :], NEG_INF)
            m = jnp.maximum(jnp.max(s, axis=-1, keepdims=True), sink)
            p = jnp.exp2(s - m)
            denom = jnp.sum(p, axis=-1, keepdims=True) + jnp.exp2(sink - m)
            p_list.append(p.astype(BF16))
            inv_list.append(1.0 / denom)
        pv = _dot(jnp.concatenate(p_list, axis=0), vv)
        for pair in range(npairs):
            contrib = pv[pair * rows:(pair + 1) * rows, :] * inv_list[pair]
            outs[pair] = contrib if outs[pair] is None else outs[pair] + contrib
    return outs


def _attend_all_heads(q_tiles, k_all, v_all, valid, sinks_ref, store):
    pairs = SWA_GROUP // 2
    for g in range(SWA_KV_HEADS):
        lo = (g // 2) * LANES
        outs = _attend(q_tiles[g * pairs:(g + 1) * pairs], k_all[:, lo:lo + LANES], v_all[:, lo:lo + LANES],
                       g % 2, valid, sinks_ref, g)
        for p, o in enumerate(outs):
            store(g * pairs + p, o.astype(BF16))


def _swa_prompt_body(sinks_ref, q_ref, k_ref, v_ref, cb_ref, sb_ref, cr_ref, sr_ref, crs_ref, srs_ref,
                     gq_ref, gk_ref, mm_ref, pa_ref, ga_ref, gb_ref, wpb_ref,
                     m_ref, knew_ref, vnew_ref, kprev_scr, vprev_scr, ob_scr):
    step = pl.program_id(0)
    n = jnp.minimum(step, pl.num_programs(0) - 2)
    blk = q_ref.shape[0]
    cur = lax.rem(step, 2)
    prev = 1 - cur

    @pl.when(step == 0)
    def _():
        kprev_scr[...] = jnp.zeros_like(kprev_scr)
        vprev_scr[...] = jnp.zeros_like(vprev_scr)
        ob_scr[...] = jnp.zeros_like(ob_scr)

    m_ref[...] = (_sigmoid(ga_ref[...].astype(F32)) * pa_ref[...]
                  + _sigmoid(gb_ref[...].astype(F32)) * _dot(ob_scr[prev], wpb_ref[...])).astype(BF16)

    cb, sb = cb_ref[pl.ds(n, 1), :], sb_ref[pl.ds(n, 1), :]
    cos_t = cb * cr_ref[...] - sb * sr_ref[...]
    sin_t = sb * crs_ref[...] + cb * srs_ref[...]
    mean_mat = mm_ref[...]
    q_tiles = [t * SWA_SCORE_SCALE
               for t in _qk_norm_rope(q_ref[...].astype(F32), gq_ref[...], cos_t, sin_t, mean_mat)]
    kn = jnp.concatenate(_qk_norm_rope(k_ref[...].astype(F32), gk_ref[...], cos_t, sin_t, mean_mat), axis=1)
    vn = v_ref[...].astype(F32)
    knew_ref[...] = kn
    vnew_ref[...] = vn
    k_all = jnp.concatenate([kprev_scr[...], kn], axis=0)
    v_all = jnp.concatenate([vprev_scr[...], vn], axis=0)

    qi = lax.broadcasted_iota(jnp.int32, (blk, 2 * blk), 0)
    kj = lax.broadcasted_iota(jnp.int32, (blk, 2 * blk), 1)
    dist = blk + qi - kj
    valid = (dist >= 0) & (dist <= WINDOW) & ((n - 1) * blk + kj >= 0)

    def store(tile, value):
        ob_scr[cur, :, tile * LANES:(tile + 1) * LANES] = value
    _attend_all_heads(q_tiles, k_all, v_all, valid, sinks_ref, store)

    kprev_scr[...] = kn
    vprev_scr[...] = vn


def _rope_block_tables(nblocks, blk):
    half = SWA_HEAD_DIM // 2
    inv_freq = ROPE_THETA ** (-(jnp.arange(half, dtype=F32) * 2.0) / SWA_HEAD_DIM)
    lane_freq = jnp.tile(inv_freq, LANES // half)[None, :]
    sign = jnp.tile(jnp.concatenate([-jnp.ones((half,), F32), jnp.ones((half,), F32)]), LANES // SWA_HEAD_DIM)[None, :]
    ang_b = (jnp.arange(nblocks, dtype=jnp.int32) * blk).astype(F32)[:, None] * lane_freq
    ang_r = jnp.arange(blk, dtype=jnp.int32).astype(F32)[:, None] * lane_freq
    cr, sr = jnp.cos(ang_r), jnp.sin(ang_r)
    return jnp.cos(ang_b), jnp.sin(ang_b), cr, sr, sign * cr, sign * sr


def _swa_prompt(proj, pa, sinks, g_q, g_k, w_pb):
    t = proj.shape[0]
    d = w_pb.shape[1]
    blk = WINDOW
    nblocks = t // blk
    mean_mat = jnp.asarray(_head_mean_matrix(), BF16)
    tables = _rope_block_tables(nblocks, blk)
    qb, kb, vb = COL_SQ // SWA_Q, COL_SK // SWA_KV, COL_SV // SWA_KV
    whole = lambda n, s: (0, 0)
    cur_blk = lambda n: jnp.minimum(n, nblocks - 1)
    prev_blk = lambda n: jnp.maximum(n - 1, 0)
    gate_rows = lambda n: pl.multiple_of(prev_blk(n) * blk, blk)
    grid_spec = pltpu.PrefetchScalarGridSpec(
        num_scalar_prefetch=1,
        grid=(nblocks + 1,),
        in_specs=[pl.BlockSpec((blk, SWA_Q), lambda n, s: (cur_blk(n), qb)),
                  pl.BlockSpec((blk, SWA_KV), lambda n, s: (cur_blk(n), kb)),
                  pl.BlockSpec((blk, SWA_KV), lambda n, s: (cur_blk(n), vb)),
                  pl.BlockSpec((nblocks, LANES), whole),
                  pl.BlockSpec((nblocks, LANES), whole),
                  pl.BlockSpec((blk, LANES), whole),
                  pl.BlockSpec((blk, LANES), whole),
                  pl.BlockSpec((blk, LANES), whole),
                  pl.BlockSpec((blk, LANES), whole),
                  pl.BlockSpec((1, LANES), whole),
                  pl.BlockSpec((1, LANES), whole),
                  pl.BlockSpec((MXU_DIM, MXU_DIM), whole),
                  pl.BlockSpec((blk, d), lambda n, s: (prev_blk(n), 0)),
                  pl.BlockSpec((pl.Element(blk), pl.Element(d)), lambda n, s: (gate_rows(n), COL_BG)),
                  pl.BlockSpec((pl.Element(blk), pl.Element(d)), lambda n, s: (gate_rows(n), COL_BG + d)),
                  pl.BlockSpec(w_pb.shape, whole, pipeline_mode=pl.Buffered(1))],
        out_specs=[pl.BlockSpec((blk, d), lambda n, s: (prev_blk(n), 0)),
                   pl.BlockSpec((blk, SWA_KV), whole),
                   pl.BlockSpec((blk, SWA_KV), whole)],
        scratch_shapes=[pltpu.VMEM((blk, SWA_KV), F32), pltpu.VMEM((blk, SWA_KV), F32),
                        pltpu.VMEM((2, blk, SWA_Q), BF16)],
    )
    return pl.pallas_call(
        _swa_prompt_body,
        grid_spec=grid_spec,
        out_shape=[jax.ShapeDtypeStruct((t, d), BF16),
                   jax.ShapeDtypeStruct((blk, SWA_KV), F32),
                   jax.ShapeDtypeStruct((blk, SWA_KV), F32)],
        compiler_params=_params("arbitrary"),
        name="swa_prompt",
    )(sinks, proj, proj, proj, *tables, g_q, g_k, mean_mat, pa, proj, proj, w_pb)


def _swa_sample_body(sinks_ref, q_ref, k_ref, v_ref, bk_ref, bv_ref, cos_ref, sin_ref, gq_ref, gk_ref, mm_ref,
                     ob_ref, kout_ref, vout_ref):
    nseq = bk_ref.shape[0]
    cos_t = jnp.concatenate([cos_ref[...]] * nseq, axis=0)
    sin_t = jnp.concatenate([sin_ref[...]] * nseq, axis=0)
    mean_mat = mm_ref[...]
    q_tiles = [t * SWA_SCORE_SCALE
               for t in _qk_norm_rope(q_ref[...].astype(F32), gq_ref[...], cos_t, sin_t, mean_mat)]
    kn = jnp.concatenate(_qk_norm_rope(k_ref[...].astype(F32), gk_ref[...], cos_t, sin_t, mean_mat), axis=1)
    vn = v_ref[...].astype(F32)

    keys = 2 * WINDOW
    qi = lax.broadcasted_iota(jnp.int32, (DEC_SEQ, keys), 0)
    kj = lax.broadcasted_iota(jnp.int32, (DEC_SEQ, keys), 1)
    dist = WINDOW + qi - kj
    valid = (dist >= 0) & (dist <= WINDOW)
    valid = jnp.concatenate([valid] * nseq, axis=0)
    pad = jnp.zeros((keys - WINDOW - DEC_SEQ, SWA_KV), F32)
    rows = nseq * DEC_SEQ

    k_parts, v_parts = [], []
    for s in range(nseq):
        rsl = slice(s * DEC_SEQ, (s + 1) * DEC_SEQ)
        bk, bv = bk_ref[s], bv_ref[s]
        kout_ref[s, :WINDOW - DEC_SEQ, :] = bk[DEC_SEQ:, :]
        kout_ref[s, WINDOW - DEC_SEQ:, :] = kn[rsl, :]
        vout_ref[s, :WINDOW - DEC_SEQ, :] = bv[DEC_SEQ:, :]
        vout_ref[s, WINDOW - DEC_SEQ:, :] = vn[rsl, :]
        k_parts += [bk, kn[rsl, :], pad]
        v_parts += [bv, vn[rsl, :], pad]
    k_stack = jnp.concatenate(k_parts, axis=0)
    v_stack = jnp.concatenate(v_parts, axis=0)

    low = lax.broadcasted_iota(jnp.int32, (nseq * keys, LANES), 1) < SWA_HEAD_DIM
    pairs = SWA_GROUP // 2
    zero_blk = jnp.zeros((DEC_SEQ, keys), F32)
    for g in range(SWA_KV_HEADS):
        lo = (g // 2) * LANES
        kv_low = g % 2 == 0
        keep = low if kv_low else jnp.logical_not(low)
        kk = jnp.where(keep, k_stack[:, lo:lo + LANES], 0.0).astype(BF16)
        vv = jnp.where(keep, v_stack[:, lo:lo + LANES], 0.0).astype(BF16)
        q_g = jnp.concatenate(q_tiles[g * pairs:(g + 1) * pairs], axis=0)
        out = None
        for parity in range(2):
            aligned = (parity == 0) == kv_low
            q_use = q_g if aligned else pltpu.roll(q_g, SWA_HEAD_DIM, 1)
            s_all = _dot_nt(q_use.astype(BF16), kk)
            p_rows, inv = [], []
            for pair in range(pairs):
                sink = sinks_ref[g * SWA_GROUP + 2 * pair + parity] * LOG2_E
                sc = jnp.concatenate(
                    [s_all[pair * rows + s * DEC_SEQ: pair * rows + (s + 1) * DEC_SEQ, s * keys:(s + 1) * keys]
                     for s in range(nseq)], axis=0)
                sc = jnp.where(valid, sc, NEG_INF)
                m = jnp.maximum(jnp.max(sc, axis=-1, keepdims=True), sink)
                p = jnp.exp2(sc - m)
                inv.append(1.0 / (jnp.sum(p, axis=-1, keepdims=True) + jnp.exp2(sink - m)))
                for s in range(nseq):
                    blocks = [p[s * DEC_SEQ:(s + 1) * DEC_SEQ, :] if s2 == s else zero_blk for s2 in range(nseq)]
                    p_rows.append(jnp.concatenate(blocks, axis=1))
            pv = _dot(jnp.concatenate(p_rows, axis=0).astype(BF16), vv)
            pv = jnp.concatenate([pv[pair * rows:(pair + 1) * rows, :] * inv[pair] for pair in range(pairs)], axis=0)
            if not aligned:
                pv = pltpu.roll(pv, SWA_HEAD_DIM, 1)
            out = pv if out is None else out + pv
        for pair in range(pairs):
            tile = g * pairs + pair
            ob_ref[:, tile * LANES:(tile + 1) * LANES] = out[pair * rows:(pair + 1) * rows, :].astype(BF16)


def _swa_sample(proj, buf_k, buf_v, sinks, cos_t, sin_t, g_q, g_k):
    t = proj.shape[0]
    nseq = SWA_SAMPLE_BATCH
    rows = nseq * DEC_SEQ
    mean_mat = jnp.asarray(_head_mean_matrix(), BF16)
    qb, kb, vb = COL_SQ // SWA_Q, COL_SK // SWA_KV, COL_SV // SWA_KV
    grid_spec = pltpu.PrefetchScalarGridSpec(
        num_scalar_prefetch=1,
        grid=(t // rows,),
        in_specs=[pl.BlockSpec((rows, SWA_Q), lambda n, s: (n, qb)),
                  pl.BlockSpec((rows, SWA_KV), lambda n, s: (n, kb)),
                  pl.BlockSpec((rows, SWA_KV), lambda n, s: (n, vb)),
                  pl.BlockSpec((nseq, WINDOW, SWA_KV), lambda n, s: (n, 0, 0)),
                  pl.BlockSpec((nseq, WINDOW, SWA_KV), lambda n, s: (n, 0, 0)),
                  pl.BlockSpec((DEC_SEQ, LANES), lambda n, s: (0, 0)),
                  pl.BlockSpec((DEC_SEQ, LANES), lambda n, s: (0, 0)),
                  pl.BlockSpec((1, LANES), lambda n, s: (0, 0)),
                  pl.BlockSpec((1, LANES), lambda n, s: (0, 0)),
                  pl.BlockSpec((MXU_DIM, MXU_DIM), lambda n, s: (0, 0))],
        out_specs=[pl.BlockSpec((rows, SWA_Q), lambda n, s: (n, 0)),
                   pl.BlockSpec((nseq, WINDOW, SWA_KV), lambda n, s: (n, 0, 0)),
                   pl.BlockSpec((nseq, WINDOW, SWA_KV), lambda n, s: (n, 0, 0))],
    )
    return pl.pallas_call(
        _swa_sample_body,
        grid_spec=grid_spec,
        out_shape=[jax.ShapeDtypeStruct((t, SWA_Q), BF16),
                   jax.ShapeDtypeStruct(buf_k.shape, F32),
                   jax.ShapeDtypeStruct(buf_v.shape, F32)],
        compiler_params=_params("arbitrary"),
        name="swa_sample",
    )(sinks, proj, proj, proj, buf_k, buf_v, cos_t, sin_t, g_q, g_k, mean_mat)


def _merge_body(oa_ref, ob_ref, wpa_ref, wpb_ref, ga_ref, gb_ref, m_ref):
    ga = _sigmoid(ga_ref[...].astype(F32))
    gb = _sigmoid(gb_ref[...].astype(F32))
    m_ref[...] = (ga * _dot(oa_ref[...], wpa_ref[...]) + gb * _dot(ob_ref[...], wpb_ref[...])).astype(BF16)


def _merge(oa, ob, proj, w_pa, w_pb, tm=1024, tn=512):
    t, d = oa.shape
    ga_blk, gb_blk = COL_BG // tn, (COL_BG + D_MODEL) // tn
    return pl.pallas_call(
        _merge_body,
        grid=(t // tm, d // tn),
        in_specs=[pl.BlockSpec((tm, d), lambda i, j: (i, 0)),
                  pl.BlockSpec((tm, d), lambda i, j: (i, 0)),
                  pl.BlockSpec((d, tn), lambda i, j: (0, j)),
                  pl.BlockSpec((d, tn), lambda i, j: (0, j)),
                  pl.BlockSpec((tm, tn), lambda i, j: (i, ga_blk + j)),
                  pl.BlockSpec((tm, tn), lambda i, j: (i, gb_blk + j))],
        out_specs=pl.BlockSpec((tm, tn), lambda i, j: (i, j)),
        out_shape=jax.ShapeDtypeStruct((t, d), BF16),
        compiler_params=_params("arbitrary", "arbitrary"),
        name="merge",
    )(oa, ob, w_pa, w_pb, proj, proj)


def _oproj_body(m_ref, w_ref, x_ref, gate_ref, o_ref, y_scr):
    if gate_ref.shape[0] == 1:
        o_ref[...] = x_ref[...] + gate_ref[...] * _dot(m_ref[...], w_ref[...])
        return
    y_scr[...] = _dot(m_ref[...], w_ref[...])

    def rows(row0):
        sl = pl.ds(row0, ROW_CHUNK)
        o_ref[sl, :] = x_ref[sl, :] + _mod_rows(gate_ref, row0, ROW_CHUNK) * y_scr[sl, :]
    _row_loop(x_ref.shape[0], rows)


def _oproj(merged, w_o, x, gate, tm=2048, tn=512):
    t, d = x.shape
    tm = min(tm, t)
    return pl.pallas_call(
        _oproj_body,
        grid=(t // tm, d // tn),
        in_specs=[pl.BlockSpec((tm, d), lambda i, j: (i, 0)),
                  pl.BlockSpec((d, tn), lambda i, j: (0, j)),
                  pl.BlockSpec((tm, tn), lambda i, j: (i, j)),
                  _mod_spec(gate, tm, tn, lambda j: j)],
        out_specs=pl.BlockSpec((tm, tn), lambda i, j: (i, j)),
        out_shape=jax.ShapeDtypeStruct(x.shape, F32),
        scratch_shapes=[pltpu.VMEM((tm, tn), F32)],
        compiler_params=_params("arbitrary", "arbitrary"),
        name="oproj",
    )(merged, w_o, x, gate)


def _ffn_body(x_ref, gn_ref, sc_ref, sh_ref, gate_ref, wg_ref, wu_ref, wd_ref, o_ref, h_scr):
    f = pl.program_id(1)

    @pl.when(f == 0)
    def _():
        def rows(row0):
            h_scr[pl.ds(row0, ROW_CHUNK), :] = _modulated_norm_rows(
                x_ref, gn_ref, sc_ref, sh_ref, row0, ROW_CHUNK).astype(BF16)
            o_ref[pl.ds(row0, ROW_CHUNK), :] = jnp.zeros((ROW_CHUNK, o_ref.shape[1]), F32)
        _row_loop(x_ref.shape[0], rows)

    h = h_scr[...]
    a = _silu(_dot(h, wg_ref[...])) * _dot(h, wu_ref[...])
    o_ref[...] += _dot(a.astype(BF16), wd_ref[...])

    @pl.when(f == pl.num_programs(1) - 1)
    def _():
        def rows(row0):
            sl = pl.ds(row0, ROW_CHUNK)
            o_ref[sl, :] = x_ref[sl, :] + _mod_rows(gate_ref, row0, ROW_CHUNK) * o_ref[sl, :]
        _row_loop(x_ref.shape[0], rows)


def _ffn(x, gn, scale, shift, gate, w_gate, w_up, w_down, tm=1024, tf=512):
    t, d = x.shape
    dff = w_gate.shape[1]
    zero = lambda j: 0
    return pl.pallas_call(
        _ffn_body,
        grid=(t // tm, dff // tf),
        in_specs=[pl.BlockSpec((tm, d), lambda i, f: (i, 0)),
                  pl.BlockSpec((1, d), lambda i, f: (0, 0)),
                  _mod_spec(scale, tm, d, zero),
                  _mod_spec(shift, tm, d, zero),
                  _mod_spec(gate, tm, d, zero),
                  pl.BlockSpec((d, tf), lambda i, f: (0, f)),
                  pl.BlockSpec((d, tf), lambda i, f: (0, f)),
                  pl.BlockSpec((tf, d), lambda i, f: (f, 0))],
        out_specs=pl.BlockSpec((tm, d), lambda i, f: (i, 0)),
        out_shape=jax.ShapeDtypeStruct(x.shape, F32),
        scratch_shapes=[pltpu.VMEM((tm, d), BF16)],
        compiler_params=_params("arbitrary", "arbitrary"),
        name="ffn",
    )(x, gn, scale, shift, gate, w_gate, w_up, w_down)


def _mods(ada):
    return tuple(ada[:, i * D_MODEL:(i + 1) * D_MODEL] for i in range(6))


def _layer_after_inproj(x, proj, glog, mods, pos, state, buf_k, buf_v, wts):
    g_q, g_k, sinks, g_out, w_pa, w_pb, w_o, gn_ffn, w_fg, w_fu, w_fd = wts
    bsz, tlen, d = x.shape
    x2 = x.reshape(bsz * tlen, d)
    _, _, gate_m, shift_f, scale_f, gate_f = mods
    if state is None:
        pa, s_new = _gla_prompt(proj, glog, g_out, w_pa)
        s_new = s_new[None]
        merged, k_new, v_new = _swa_prompt(proj, pa, sinks, g_q, g_k, w_pb)
    else:
        cos_t, sin_t = _rope_tables(pos)
        oa, s_new = _gla_sample(proj, glog, state, g_out)
        ob, k_new, v_new = _swa_sample(proj, buf_k.reshape(bsz, WINDOW, SWA_KV), buf_v.reshape(bsz, WINDOW, SWA_KV),
                                       sinks, cos_t, sin_t, g_q, g_k)
        merged = _merge(oa, ob, proj, w_pa, w_pb)
    x1 = _oproj(merged, w_o, x2, gate_m)
    y = _ffn(x1, gn_ffn, scale_f, shift_f, gate_f, w_fg, w_fu, w_fd)
    return (y.reshape(bsz, tlen, d), s_new,
            k_new.reshape(bsz, WINDOW, SWA_KV_HEADS, SWA_HEAD_DIM),
            v_new.reshape(bsz, WINDOW, SWA_KV_HEADS, SWA_HEAD_DIM))


def kernel(x_prompt, x_sample, c_prompt, c_sample, state_gla, cache_swa_k, cache_swa_v, w_ada, b_ada, g_norm_mix, w_in, w_gk2, b_gk2, g_qnorm, g_knorm, sinks, g_gla_out, w_pa, w_pb, w_o, g_norm_ffn, w_ffn_gate, w_ffn_up, w_ffn_down):
    assert w_in.shape[0] == 1, "single trunk layer"
    pos_p = jnp.arange(SEQ, dtype=jnp.int32)
    pos_s = PAST_LEN + jnp.arange(DEC_SEQ, dtype=jnp.int32)
    l = 0
    wt = jnp.transpose(w_in[l]).astype(BF16)
    wt_lr = jnp.pad(wt[W_IN_GLR:W_IN_GLR + GLA_RANK], ((0, LANES - GLA_RANK), (0, 0)))
    w_gk = jnp.pad(w_gk2[l], ((0, LANES - GLA_RANK), (0, 0))).astype(BF16)
    in_wts = (wt, wt_lr, w_gk, b_gk2[l].reshape(1, GLA_KEY))
    gn_mix = g_norm_mix[l].reshape(1, D_MODEL)

    n_c = 1 + DEC_BATCH
    pad = (-n_c) % (2 * SUBLANES)
    c_all = jnp.pad(jnp.concatenate([c_prompt, c_sample], axis=0), ((0, pad), (0, 0)))
    ada = _ada(c_all, w_ada[l], b_ada[l].reshape(1, -1))
    mods_p, mods_s = _mods(ada[0:1]), _mods(ada[1:n_c])

    later_weights = (w_pa[l], w_pb[l], w_o[l], w_ffn_gate[l], w_ffn_up[l], w_ffn_down[l])
    proj_p, glog_p, *later_bf16 = _inproj(x_prompt.reshape(-1, D_MODEL), gn_mix, mods_p[1], mods_p[0], *in_wts,
                                          cast_weights=later_weights)
    proj_s, glog_s = _inproj(x_sample.reshape(-1, D_MODEL), gn_mix, mods_s[1], mods_s[0], *in_wts)
    w_pa_b, w_pb_b, w_o_b, w_fg_b, w_fu_b, w_fd_b = later_bf16
    wts = (jnp.tile(g_qnorm[l], 2).reshape(1, LANES), jnp.tile(g_knorm[l], 2).reshape(1, LANES),
           sinks[l], g_gla_out[l].reshape(1, GLA_DV), w_pa_b, w_pb_b, w_o_b,
           g_norm_ffn[l].reshape(1, D_MODEL), w_fg_b, w_fu_b, w_fd_b)

    yp, sp, kp, vp = _layer_after_inproj(x_prompt, proj_p, glog_p, mods_p, pos_p, None, None, None, wts)
    ys, ss, ks, vs = _layer_after_inproj(x_sample, proj_s, glog_s, mods_s, pos_s, state_gla[l], cache_swa_k[l],
                                         cache_swa_v[l], wts)
    return (yp, ys, sp[None], kp[None], vp[None], ss[None], ks[None], vs[None])
```

```python
import functools

import numpy as np
import jax
import jax.numpy as jnp
from jax import lax
from jax.experimental import pallas as pl
from jax.experimental.pallas import tpu as pltpu

F32 = jnp.float32
BF16 = jnp.bfloat16

D_MODEL = 2048
SEQ = 16384
DEC_BATCH = 128
DEC_SEQ = 8
PAST_LEN = 16384
GLA_HEADS = 4
GLA_DK = 256
GLA_DV = 512
GLA_KEY = GLA_HEADS * GLA_DK
GLA_VAL = GLA_HEADS * GLA_DV
GLA_RANK = 16
GLA_TAU = 16.0
SWA_HEAD_DIM = 64
SWA_HEADS = 32
SWA_KV_HEADS = 4
SWA_GROUP = SWA_HEADS // SWA_KV_HEADS
SWA_Q = SWA_HEADS * SWA_HEAD_DIM
SWA_KV = SWA_KV_HEADS * SWA_HEAD_DIM
WINDOW = 128
ROPE_THETA = 10000.0
D_FF = 5632
EPS = 1e-6
NEG_INF = -1e30
LOG2_E = 1.4426950408889634
SWA_SCORE_SCALE = (SWA_HEAD_DIM ** -0.5) * LOG2_E

SUBLANES = 8
LANES = 128
MXU_DIM = 256
VMEM_LIMIT_BYTES = 58 * 1024 * 1024

COL_GQ = 0
COL_GK = COL_GQ + GLA_KEY
COL_GV = COL_GK + GLA_KEY
COL_GR = COL_GV + GLA_VAL
COL_SQ = COL_GR + GLA_VAL
COL_SK = COL_SQ + SWA_Q
COL_SV = COL_SK + SWA_KV
COL_BG = COL_SV + SWA_KV
PROJ_COLS = COL_BG + 2 * D_MODEL
W_IN_GLR = COL_GR + GLA_VAL

GLA_CHUNK = 256
GLA_ONE_SIDED_MAX_DECAY = 80.0
GLA_SAMPLE_BATCH = 8
SWA_SAMPLE_BATCH = 4
ROW_CHUNK = 128


def _dot(a, b):
    return jnp.dot(a, b, preferred_element_type=F32)


def _dot_nt(a, b):
    return lax.dot_general(a, b, (((1,), (1,)), ((), ())), preferred_element_type=F32)


def _dot_tn(a, b):
    return lax.dot_general(a, b, (((0,), (0,)), ((), ())), preferred_element_type=F32)


def _sigmoid(x):
    return 1.0 / (1.0 + jnp.exp(-x))


def _silu(x):
    return x * _sigmoid(x)


def _split3(x):
    hi = x.astype(BF16)
    r1 = x - hi.astype(F32)
    mid = r1.astype(BF16)
    lo = (r1 - mid.astype(F32)).astype(BF16)
    return hi, mid, lo


def _params(*sem):
    return pltpu.CompilerParams(dimension_semantics=sem, vmem_limit_bytes=VMEM_LIMIT_BYTES)


def _ada_body(c_ref, w_ref, b_ref, o_ref):
    a = _silu(c_ref[...]).astype(BF16)
    o_ref[...] = _dot(a, w_ref[...].astype(BF16)) + b_ref[...]


def _ada(c_all, w_ada, b_ada, tn=512):
    m, d = c_all.shape
    n = w_ada.shape[1]
    return pl.pallas_call(
        _ada_body,
        grid=(n // tn,),
        in_specs=[pl.BlockSpec((m, d), lambda j: (0, 0)),
                  pl.BlockSpec((d, tn), lambda j: (0, j)),
                  pl.BlockSpec((1, tn), lambda j: (0, j))],
        out_specs=pl.BlockSpec((m, tn), lambda j: (0, j)),
        out_shape=jax.ShapeDtypeStruct((m, n), F32),
        compiler_params=_params("arbitrary"),
        name="ada",
    )(c_all, w_ada, b_ada)


def _mod_spec(mod, tm, width, col):
    if mod.shape[0] == 1:
        return pl.BlockSpec((1, width), lambda i, j: (0, col(j)))
    return pl.BlockSpec((tm // DEC_SEQ, width), lambda i, j: (i, col(j)))


def _mod_rows(ref, row0, nrows):
    if ref.shape[0] == 1:
        return ref[...]
    seq0 = row0 // DEC_SEQ
    return jnp.concatenate(
        [jnp.broadcast_to(ref[pl.ds(seq0 + s, 1), :], (DEC_SEQ, ref.shape[1])) for s in range(nrows // DEC_SEQ)],
        axis=0)


def _modulated_norm_rows(x_ref, gn_ref, sc_ref, sh_ref, row0, nrows):
    x = x_ref[pl.ds(row0, nrows), :]
    ms = jnp.mean(x * x, axis=-1, keepdims=True)
    y = (x * lax.rsqrt(ms + EPS)) * gn_ref[...]
    return y * (1.0 + _mod_rows(sc_ref, row0, nrows)) + _mod_rows(sh_ref, row0, nrows)


def _row_loop(total_rows, body):
    def step(r, carry):
        body(pl.multiple_of(r * ROW_CHUNK, ROW_CHUNK))
        return carry
    lax.fori_loop(0, total_rows // ROW_CHUNK, step, 0)


def _inproj_body(x_ref, gn_ref, sc_ref, sh_ref, wt_ref, wlr_ref, wgk_ref, bgk_ref, *rest, n_casts):
    cast_src = rest[:n_casts]
    proj_ref, glog_ref = rest[n_casts:n_casts + 2]
    cast_dst = rest[n_casts + 2:2 * n_casts + 2]
    h_scr = rest[-1]
    for src, dst in zip(cast_src, cast_dst):
        dst[...] = src[...].astype(BF16)

    @pl.when(pl.program_id(1) == 0)
    def _():
        def rows(row0):
            hb = _modulated_norm_rows(x_ref, gn_ref, sc_ref, sh_ref, row0, ROW_CHUNK).astype(BF16)
            h_scr[pl.ds(row0, ROW_CHUNK), :] = hb
            glr = _dot_nt(hb, wlr_ref[...])
            z = _dot(glr.astype(BF16), wgk_ref[...]) + bgk_ref[...]
            log_sig = jnp.minimum(z, 0.0) - jnp.log1p(jnp.exp(-jnp.abs(z)))
            glog_ref[pl.ds(row0, ROW_CHUNK), :] = log_sig * (1.0 / GLA_TAU)
        _row_loop(x_ref.shape[0], rows)

    proj_ref[...] = _dot_nt(h_scr[...], wt_ref[...]).astype(BF16)


def _cast_block_spec(shape, row_tiles, col_steps):
    rows, cols = shape
    br = rows // row_tiles
    assert br * row_tiles == rows and br % (2 * SUBLANES) == 0
    ncb = max(c for c in range(1, col_steps + 1) if cols % c == 0 and (cols // c) % LANES == 0)
    return pl.BlockSpec((br, cols // ncb), lambda i, j: (i, jnp.minimum(j, ncb - 1)))


def _inproj(x, gn, scale, shift, wt, wt_lr, w_gk, b_gk, cast_weights=(), tm=2048, tn=512):
    t, d = x.shape
    tm = min(tm, t)
    n = wt.shape[0] - GLA_RANK
    n_first = W_IN_GLR // tn
    grid = (t // tm, n // tn)
    zero = lambda j: 0
    wt_rows = lambda i, j: (pl.multiple_of(j * tn + jnp.where(j >= n_first, GLA_RANK, 0), GLA_RANK), 0)
    cast_specs = [_cast_block_spec(w.shape, *grid) for w in cast_weights]
    return pl.pallas_call(
        functools.partial(_inproj_body, n_casts=len(cast_weights)),
        grid=grid,
        in_specs=[pl.BlockSpec((tm, d), lambda i, j: (i, 0), pipeline_mode=pl.Buffered(1)),
                  pl.BlockSpec((1, d), lambda i, j: (0, 0)),
                  _mod_spec(scale, tm, d, zero),
                  _mod_spec(shift, tm, d, zero),
                  pl.BlockSpec((pl.Element(tn), pl.Element(d)), wt_rows),
                  pl.BlockSpec(wt_lr.shape, lambda i, j: (0, 0)),
                  pl.BlockSpec(w_gk.shape, lambda i, j: (0, 0)),
                  pl.BlockSpec(b_gk.shape, lambda i, j: (0, 0))] + cast_specs,
        out_specs=[pl.BlockSpec((tm, tn), lambda i, j: (i, j)),
                   pl.BlockSpec((tm, GLA_KEY), lambda i, j: (i, 0))] + cast_specs,
        out_shape=[jax.ShapeDtypeStruct((t, n), BF16),
                   jax.ShapeDtypeStruct((t, GLA_KEY), F32)]
                  + [jax.ShapeDtypeStruct(w.shape, BF16) for w in cast_weights],
        scratch_shapes=[pltpu.VMEM((tm, d), BF16)],
        compiler_params=_params("arbitrary", "arbitrary"),
        name="inproj",
    )(x, gn, scale, shift, wt, wt_lr, w_gk, b_gk, *cast_weights)


def _gla_level_table(c, group):
    n_levels = int(np.log2(group))
    i = np.arange(c)[:, None]
    j = np.arange(c)[None, :]
    x = np.bitwise_xor(i, j)
    lvl = np.floor(np.log2(np.maximum(x, 1))).astype(np.int32)
    lvl = np.where(i == j, n_levels, lvl)
    valid = (i >= j) & (i // group == j // group)
    return np.where(valid, lvl, -1).astype(np.int32), n_levels


def _gla_tril(c, group):
    i = np.arange(c)[:, None]
    j = np.arange(c)[None, :]
    return ((i >= j) & (i // group == j // group)).astype(np.float32)


def _level_reference(b_scr, rows, level):
    s = 2 ** (level + 1)
    width = b_scr.shape[1]
    pieces = []
    if s >= SUBLANES:
        for blk in range(rows // s):
            mid = blk * s + s // 2
            pieces.append(jnp.broadcast_to(b_scr[mid:mid + 1, :], (s, width)))
    else:
        p = lax.broadcasted_iota(jnp.int32, (SUBLANES, width), 0)
        for tile in range(rows // SUBLANES):
            base = tile * SUBLANES
            mids = [base + q * s + s // 2 for q in range(SUBLANES // s)]
            r = jnp.broadcast_to(b_scr[mids[-1]:mids[-1] + 1, :], (SUBLANES, width))
            for q in range(SUBLANES // s - 2, -1, -1):
                row = jnp.broadcast_to(b_scr[mids[q]:mids[q] + 1, :], (SUBLANES, width))
                r = jnp.where(p < (q + 1) * s, row, r)
            pieces.append(r)
    return jnp.concatenate(pieces, axis=0) if len(pieces) > 1 else pieces[0]


def _gla_intra(q, k, k_bf, b, b_scr, lvl, n_levels):
    rows = q.shape[0]
    a = jnp.where(lvl == n_levels, _dot_nt(q.astype(BF16), k_bf), 0.0)
    for level in range(n_levels):
        f = jnp.exp(-jnp.abs(b - _level_reference(b_scr, rows, level)))
        p = _dot_nt((q * f).astype(BF16), (k * f).astype(BF16))
        a = jnp.where(lvl == level, p, a)
    return a


def _gla_cumsum(g, tril_bf):
    hi, mid, lo = _split3(g)
    return _dot(tril_bf, hi) + _dot(tril_bf, mid) + _dot(tril_bf, lo)


def _gla_out_gate(o, gout, gr):
    ms = jnp.mean(o * o, axis=-1, keepdims=True)
    y = (o * lax.rsqrt(ms + EPS)) * gout
    return (y * _silu(gr.astype(F32))).astype(BF16)


def _gla_prompt_body(q_ref, k_ref, v_ref, gr_ref, g_ref, tril_ref, lvl_ref, gout_ref, wpa_ref,
                     pa_ref, st_ref, st_scr, b_scr, a_scr, oa_scr, *, n_levels, heads):
    c = pl.program_id(0)
    n_chunks = pl.num_programs(0) - 1
    rows = q_ref.shape[0]
    cur = lax.rem(c, 2)
    prev = 1 - cur
    half = pa_ref.shape[1] // 2

    @pl.when(c == 0)
    def _():
        st_scr[...] = jnp.zeros_like(st_scr)
        oa_scr[...] = jnp.zeros_like(oa_scr)

    pa_ref[:, :half] = _dot(oa_scr[prev], wpa_ref[:, :half])

    tril, lvl = tril_ref[...], lvl_ref[...]
    per_head = []
    mild = None
    for h in range(heads):
        ksl = slice(h * GLA_DK, (h + 1) * GLA_DK)
        b = _gla_cumsum(g_ref[:, ksl], tril)
        b_scr[h] = b
        b_end = b_scr[h, rows - 1:rows, :]
        q = q_ref[:, ksl].astype(F32) * (GLA_DK ** -0.5)
        k_bf = k_ref[:, ksl]
        k = k_bf.astype(F32)
        st = st_scr[h]
        qd = (q * jnp.exp(b)).astype(BF16)
        o_inter = _dot_nt(qd, st.astype(BF16))
        head_mild = jnp.min(b_end) > -GLA_ONE_SIDED_MAX_DECAY
        mild = head_mild if mild is None else jnp.logical_and(mild, head_mild)
        per_head.append((b, b_end, q, k, k_bf, st, qd, o_inter))

    @pl.when(mild)
    def _():
        for h, (b, _, _, k, _, _, qd, _) in enumerate(per_head):
            a_scr[h] = jnp.where(lvl >= 0, _dot_nt(qd, (k * jnp.exp(-b)).astype(BF16)), 0.0)

    @pl.when(jnp.logical_not(mild))
    def _():
        for h, (b, _, q, k, k_bf, _, _, _) in enumerate(per_head):
            a_scr[h] = _gla_intra(q, k, k_bf, b, b_scr.at[h], lvl, n_levels)

    pa_ref[:, half:] = _dot(oa_scr[prev], wpa_ref[:, half:])
    for h, (b, b_end, _, k, _, st, _, o_inter) in enumerate(per_head):
        vsl = slice(h * GLA_DV, (h + 1) * GLA_DV)
        v = v_ref[:, vsl]
        o = o_inter + _dot(a_scr[h].astype(BF16), v)
        k_dec = (k * jnp.exp(b_end - b)).astype(BF16)
        st_scr[h] = st * jnp.exp(b_end) + _dot_tn(v, k_dec)
        oa_scr[cur, :, vsl] = _gla_out_gate(o, gout_ref[...], gr_ref[:, vsl])

    @pl.when(c == n_chunks - 1)
    def _():
        for h in range(heads):
            st_ref[h] = st_scr[h].T


def _gla_prompt(proj, glog, g_out, w_pa):
    t = proj.shape[0]
    c = GLA_CHUNK
    n_chunks = t // c
    heads = GLA_HEADS
    lvl, n_levels = _gla_level_table(c, c)
    tril = jnp.asarray(_gla_tril(c, c), BF16)
    lvl = jnp.asarray(lvl)
    qb, kb, vb, rb = COL_GQ // GLA_KEY, COL_GK // GLA_KEY, COL_GV // GLA_VAL, COL_GR // GLA_VAL
    chunk = lambda i: jnp.minimum(i, n_chunks - 1)
    whole = lambda i: (0, 0)
    return pl.pallas_call(
        functools.partial(_gla_prompt_body, n_levels=n_levels, heads=heads),
        grid=(n_chunks + 1,),
        in_specs=[pl.BlockSpec((c, GLA_KEY), lambda i: (chunk(i), qb)),
                  pl.BlockSpec((c, GLA_KEY), lambda i: (chunk(i), kb)),
                  pl.BlockSpec((c, GLA_VAL), lambda i: (chunk(i), vb)),
                  pl.BlockSpec((c, GLA_VAL), lambda i: (chunk(i), rb)),
                  pl.BlockSpec((c, GLA_KEY), lambda i: (chunk(i), 0)),
                  pl.BlockSpec((c, c), whole),
                  pl.BlockSpec((c, c), whole),
                  pl.BlockSpec((1, GLA_DV), whole),
                  pl.BlockSpec(w_pa.shape, whole, pipeline_mode=pl.Buffered(1))],
        out_specs=[pl.BlockSpec((c, w_pa.shape[1]), lambda i: (jnp.maximum(i - 1, 0), 0)),
                   pl.BlockSpec((heads, GLA_DK, GLA_DV), lambda i: (0, 0, 0))],
        out_shape=[jax.ShapeDtypeStruct((t, w_pa.shape[1]), F32),
                   jax.ShapeDtypeStruct((heads, GLA_DK, GLA_DV), F32)],
        scratch_shapes=[pltpu.VMEM((heads, GLA_DV, GLA_DK), F32), pltpu.VMEM((heads, c, GLA_DK), F32),
                        pltpu.VMEM((heads, c, c), F32), pltpu.VMEM((2, c, GLA_VAL), BF16)],
        compiler_params=_params("arbitrary"),
        name="gla_prompt",
    )(proj, proj, proj, proj, glog, tril, lvl, g_out, w_pa)


def _gla_sample_body(q_ref, k_ref, v_ref, gr_ref, g_ref, s0_ref, tril_ref, lvl_ref, gout_ref,
                     oa_ref, s1_ref, b_scr, *, n_levels):
    rows = q_ref.shape[0]
    nseq = rows // DEC_SEQ
    b = _gla_cumsum(g_ref[...], tril_ref[...])
    b_scr[...] = b
    b_end = jnp.concatenate(
        [jnp.broadcast_to(b_scr[(s + 1) * DEC_SEQ - 1:(s + 1) * DEC_SEQ, :], (DEC_SEQ, GLA_DK))
         for s in range(nseq)], axis=0)
    q = q_ref[...].astype(F32) * (GLA_DK ** -0.5)
    k_bf = k_ref[...]
    k = k_bf.astype(F32)
    v = v_ref[...]
    qd = (q * jnp.exp(b)).astype(BF16)
    a = _gla_intra(q, k, k_bf, b, b_scr, lvl_ref[...], n_levels)
    o_intra = _dot(a.astype(BF16), v)

    stacked = jnp.concatenate([k * jnp.exp(b_end - b), jnp.exp(b_end)], axis=0)
    stacked_t = stacked.T
    kd_t = stacked_t[:, :rows]
    lane = lax.broadcasted_iota(jnp.int32, kd_t.shape, 1)
    outs = []
    for s in range(nseq):
        s0 = s0_ref[s, 0]
        outs.append(_dot(qd[s * DEC_SEQ:(s + 1) * DEC_SEQ, :], s0.astype(BF16)))
        in_seq = (lane >= s * DEC_SEQ) & (lane < (s + 1) * DEC_SEQ)
        kd_s = jnp.where(in_seq, kd_t, 0.0).astype(BF16)
        col = rows + s * DEC_SEQ
        decay = stacked_t[:, col:col + 1]
        s1_ref[s, 0] = s0 * decay + _dot(kd_s, v)
    o = jnp.concatenate(outs, axis=0) + o_intra
    oa_ref[...] = _gla_out_gate(o, gout_ref[...], gr_ref[...])


def _gla_sample(proj, glog, state, g_out):
    t = proj.shape[0]
    nseq = GLA_SAMPLE_BATCH
    rows = nseq * DEC_SEQ
    lvl, n_levels = _gla_level_table(rows, DEC_SEQ)
    tril = jnp.asarray(_gla_tril(rows, DEC_SEQ), BF16)
    lvl = jnp.asarray(lvl)
    qb, kb, vb, rb = COL_GQ // GLA_DK, COL_GK // GLA_DK, COL_GV // GLA_DV, COL_GR // GLA_DV
    return pl.pallas_call(
        functools.partial(_gla_sample_body, n_levels=n_levels),
        grid=(t // rows, GLA_HEADS),
        in_specs=[pl.BlockSpec((rows, GLA_DK), lambda i, h: (i, qb + h)),
                  pl.BlockSpec((rows, GLA_DK), lambda i, h: (i, kb + h)),
                  pl.BlockSpec((rows, GLA_DV), lambda i, h: (i, vb + h)),
                  pl.BlockSpec((rows, GLA_DV), lambda i, h: (i, rb + h)),
                  pl.BlockSpec((rows, GLA_DK), lambda i, h: (i, h)),
                  pl.BlockSpec((nseq, 1, GLA_DK, GLA_DV), lambda i, h: (i, h, 0, 0)),
                  pl.BlockSpec((rows, rows), lambda i, h: (0, 0)),
                  pl.BlockSpec((rows, rows), lambda i, h: (0, 0)),
                  pl.BlockSpec((1, GLA_DV), lambda i, h: (0, 0))],
        out_specs=[pl.BlockSpec((rows, GLA_DV), lambda i, h: (i, h)),
                   pl.BlockSpec((nseq, 1, GLA_DK, GLA_DV), lambda i, h: (i, h, 0, 0))],
        out_shape=[jax.ShapeDtypeStruct((t, GLA_VAL), BF16),
                   jax.ShapeDtypeStruct(state.shape, F32)],
        scratch_shapes=[pltpu.VMEM((rows, GLA_DK), F32)],
        compiler_params=_params("arbitrary", "arbitrary"),
        name="gla_sample",
    )(proj, proj, proj, proj, glog, state, tril, lvl, g_out)


def _rope_tables(pos):
    half = SWA_HEAD_DIM // 2
    inv_freq = ROPE_THETA ** (-(jnp.arange(half, dtype=F32) * 2.0) / SWA_HEAD_DIM)
    ang = pos.astype(F32)[:, None] * inv_freq[None, :]
    cos, sin = jnp.cos(ang), jnp.sin(ang)
    cos_t = jnp.concatenate([cos, cos, cos, cos], axis=-1)
    sin_t = jnp.concatenate([-sin, sin, -sin, sin], axis=-1)
    return cos_t, sin_t


def _head_mean_matrix():
    i = np.arange(MXU_DIM)[:, None] // SWA_HEAD_DIM
    j = np.arange(MXU_DIM)[None, :] // SWA_HEAD_DIM
    return (i == j).astype(np.float32) / SWA_HEAD_DIM


def _qk_norm_rope(x, gain, cos_t, sin_t, mean_mat):
    rows, width = x.shape
    lane = lax.broadcasted_iota(jnp.int32, (rows, LANES), 1)
    first_half = (lane % SWA_HEAD_DIM) < (SWA_HEAD_DIM // 2)
    shift = SWA_HEAD_DIM // 2
    sq = (x * x).astype(BF16)
    tiles = []
    for c in range(width // MXU_DIM):
        ms = _dot(sq[:, c * MXU_DIM:(c + 1) * MXU_DIM], mean_mat)
        for t in range(MXU_DIM // LANES):
            lo = c * MXU_DIM + t * LANES
            y = (x[:, lo:lo + LANES] * lax.rsqrt(ms[:, t * LANES:(t + 1) * LANES] + EPS)) * gain
            rot = jnp.where(first_half, pltpu.roll(y, LANES - shift, 1), pltpu.roll(y, shift, 1))
            tiles.append(y * cos_t + rot * sin_t)
    return tiles


def _attend(q_tiles, k_tile, v_tile, parity_of_kv, valid, sinks_ref, kv_head):
    rows = q_tiles[0].shape[0]
    lane = lax.broadcasted_iota(jnp.int32, k_tile.shape, 1)
    low = lane < SWA_HEAD_DIM

    def place(x, want_low):
        have_low = parity_of_kv == 0
        src = x if have_low == want_low else pltpu.roll(x, SWA_HEAD_DIM, 1)
        return jnp.where(low if want_low else ~low, src, 0.0).astype(BF16)

    q_all = jnp.concatenate(q_tiles, axis=0).astype(BF16)
    npairs = len(q_tiles)
    outs = [None] * npairs
    for parity in range(2):
        kk = place(k_tile, parity == 0)
        vv = place(v_tile, parity == 0)
        s_all = _dot_nt(q_all, kk)
        p_list, inv_list = [], []
        for pair in range(npairs):
            sink = sinks_ref[kv_head * SWA_GROUP + 2 * pair + parity] * LOG2_E
            s = jnp.where(valid, s_all[pair * rows:(pair + 1) * rows, :], NEG_INF)
            m = jnp.maximum(jnp.max(s, axis=-1, keepdims=True), sink)
            p = jnp.exp2(s - m)
            denom = jnp.sum(p, axis=-1, keepdims=True) + jnp.exp2(sink - m)
            p_list.append(p.astype(BF16))
            inv_list.append(1.0 / denom)
        pv = _dot(jnp.concatenate(p_list, axis=0), vv)
        for pair in range(npairs):
            contrib = pv[pair * rows:(pair + 1) * rows, :] * inv_list[pair]
            outs[pair] = contrib if outs[pair] is None else outs[pair] + contrib
    return outs


def _attend_all_heads(q_tiles, k_all, v_all, valid, sinks_ref, store):
    pairs = SWA_GROUP // 2
    for g in range(SWA_KV_HEADS):
        lo = (g // 2) * LANES
        outs = _attend(q_tiles[g * pairs:(g + 1) * pairs], k_all[:, lo:lo + LANES], v_all[:, lo:lo + LANES],
                       g % 2, valid, sinks_ref, g)
        for p, o in enumerate(outs):
            store(g * pairs + p, o.astype(BF16))


def _swa_prompt_body(sinks_ref, q_ref, k_ref, v_ref, cb_ref, sb_ref, cr_ref, sr_ref, crs_ref, srs_ref,
                     gq_ref, gk_ref, mm_ref, pa_ref, ga_ref, gb_ref, wpb_ref,
                     m_ref, knew_ref, vnew_ref, kprev_scr, vprev_scr, ob_scr):
    step = pl.program_id(0)
    n = jnp.minimum(step, pl.num_programs(0) - 2)
    blk = q_ref.shape[0]
    cur = lax.rem(step, 2)
    prev = 1 - cur

    @pl.when(step == 0)
    def _():
        kprev_scr[...] = jnp.zeros_like(kprev_scr)
        vprev_scr[...] = jnp.zeros_like(vprev_scr)
        ob_scr[...] = jnp.zeros_like(ob_scr)

    m_ref[...] = (_sigmoid(ga_ref[...].astype(F32)) * pa_ref[...]
                  + _sigmoid(gb_ref[...].astype(F32)) * _dot(ob_scr[prev], wpb_ref[...])).astype(BF16)

    cb, sb = cb_ref[pl.ds(n, 1), :], sb_ref[pl.ds(n, 1), :]
    cos_t = cb * cr_ref[...] - sb * sr_ref[...]
    sin_t = sb * crs_ref[...] + cb * srs_ref[...]
    mean_mat = mm_ref[...]
    q_tiles = [t * SWA_SCORE_SCALE
               for t in _qk_norm_rope(q_ref[...].astype(F32), gq_ref[...], cos_t, sin_t, mean_mat)]
    kn = jnp.concatenate(_qk_norm_rope(k_ref[...].astype(F32), gk_ref[...], cos_t, sin_t, mean_mat), axis=1)
    vn = v_ref[...].astype(F32)
    knew_ref[...] = kn
    vnew_ref[...] = vn
    k_all = jnp.concatenate([kprev_scr[...], kn], axis=0)
    v_all = jnp.concatenate([vprev_scr[...], vn], axis=0)

    qi = lax.broadcasted_iota(jnp.int32, (blk, 2 * blk), 0)
    kj = lax.broadcasted_iota(jnp.int32, (blk, 2 * blk), 1)
    dist = blk + qi - kj
    valid = (dist >= 0) & (dist <= WINDOW) & ((n - 1) * blk + kj >= 0)

    def store(tile, value):
        ob_scr[cur, :, tile * LANES:(tile + 1) * LANES] = value
    _attend_all_heads(q_tiles, k_all, v_all, valid, sinks_ref, store)

    kprev_scr[...] = kn
    vprev_scr[...] = vn


def _rope_block_tables(nblocks, blk):
    half = SWA_HEAD_DIM // 2
    inv_freq = ROPE_THETA ** (-(jnp.arange(half, dtype=F32) * 2.0) / SWA_HEAD_DIM)
    lane_freq = jnp.tile(inv_freq, LANES // half)[None, :]
    sign = jnp.tile(jnp.concatenate([-jnp.ones((half,), F32), jnp.ones((half,), F32)]), LANES // SWA_HEAD_DIM)[None, :]
    ang_b = (jnp.arange(nblocks, dtype=jnp.int32) * blk).astype(F32)[:, None] * lane_freq
    ang_r = jnp.arange(blk, dtype=jnp.int32).astype(F32)[:, None] * lane_freq
    cr, sr = jnp.cos(ang_r), jnp.sin(ang_r)
    return jnp.cos(ang_b), jnp.sin(ang_b), cr, sr, sign * cr, sign * sr


def _swa_prompt(proj, pa, sinks, g_q, g_k, w_pb):
    t = proj.shape[0]
    d = w_pb.shape[1]
    blk = WINDOW
    nblocks = t // blk
    mean_mat = jnp.asarray(_head_mean_matrix(), BF16)
    tables = _rope_block_tables(nblocks, blk)
    qb, kb, vb = COL_SQ // SWA_Q, COL_SK // SWA_KV, COL_SV // SWA_KV
    whole = lambda n, s: (0, 0)
    cur_blk = lambda n: jnp.minimum(n, nblocks - 1)
    prev_blk = lambda n: jnp.maximum(n - 1, 0)
    gate_rows = lambda n: pl.multiple_of(prev_blk(n) * blk, blk)
    grid_spec = pltpu.PrefetchScalarGridSpec(
        num_scalar_prefetch=1,
        grid=(nblocks + 1,),
        in_specs=[pl.BlockSpec((blk, SWA_Q), lambda n, s: (cur_blk(n), qb)),
                  pl.BlockSpec((blk, SWA_KV), lambda n, s: (cur_blk(n), kb)),
                  pl.BlockSpec((blk, SWA_KV), lambda n, s: (cur_blk(n), vb)),
                  pl.BlockSpec((nblocks, LANES), whole),
                  pl.BlockSpec((nblocks, LANES), whole),
                  pl.BlockSpec((blk, LANES), whole),
                  pl.BlockSpec((blk, LANES), whole),
                  pl.BlockSpec((blk, LANES), whole),
                  pl.BlockSpec((blk, LANES), whole),
                  pl.BlockSpec((1, LANES), whole),
                  pl.BlockSpec((1, LANES), whole),
                  pl.BlockSpec((MXU_DIM, MXU_DIM), whole),
                  pl.BlockSpec((blk, d), lambda n, s: (prev_blk(n), 0)),
                  pl.BlockSpec((pl.Element(blk), pl.Element(d)), lambda n, s: (gate_rows(n), COL_BG)),
                  pl.BlockSpec((pl.Element(blk), pl.Element(d)), lambda n, s: (gate_rows(n), COL_BG + d)),
                  pl.BlockSpec(w_pb.shape, whole, pipeline_mode=pl.Buffered(1))],
        out_specs=[pl.BlockSpec((blk, d), lambda n, s: (prev_blk(n), 0)),
                   pl.BlockSpec((blk, SWA_KV), whole),
                   pl.BlockSpec((blk, SWA_KV), whole)],
        scratch_shapes=[pltpu.VMEM((blk, SWA_KV), F32), pltpu.VMEM((blk, SWA_KV), F32),
                        pltpu.VMEM((2, blk, SWA_Q), BF16)],
    )
    return pl.pallas_call(
        _swa_prompt_body,
        grid_spec=grid_spec,
        out_shape=[jax.ShapeDtypeStruct((t, d), BF16),
                   jax.ShapeDtypeStruct((blk, SWA_KV), F32),
                   jax.ShapeDtypeStruct((blk, SWA_KV), F32)],
        compiler_params=_params("arbitrary"),
        name="swa_prompt",
    )(sinks, proj, proj, proj, *tables, g_q, g_k, mean_mat, pa, proj, proj, w_pb)


def _swa_sample_body(sinks_ref, q_ref, k_ref, v_ref, bk_ref, bv_ref, cos_ref, sin_ref, gq_ref, gk_ref, mm_ref,
                     ob_ref, kout_ref, vout_ref):
    nseq = bk_ref.shape[0]
    rows = nseq * DEC_SEQ
    cos_t = jnp.concatenate([cos_ref[...]] * nseq, axis=0)
    sin_t = jnp.concatenate([sin_ref[...]] * nseq, axis=0)
    mean_mat = mm_ref[...]
    q_tiles = [t * SWA_SCORE_SCALE
               for t in _qk_norm_rope(q_ref[...].astype(F32), gq_ref[...], cos_t, sin_t, mean_mat)]
    kn = jnp.concatenate(_qk_norm_rope(k_ref[...].astype(F32), gk_ref[...], cos_t, sin_t, mean_mat), axis=1)
    vn = v_ref[...].astype(F32)

    lane_w = lax.broadcasted_iota(jnp.int32, (SWA_KV, WINDOW), 1)
    pad_rows = jnp.zeros((WINDOW - DEC_SEQ, SWA_KV), F32)
    for s in range(nseq):
        rsl = slice(s * DEC_SEQ, (s + 1) * DEC_SEQ)
        for buf_ref, new, out_ref in ((bk_ref, kn, kout_ref), (bv_ref, vn, vout_ref)):
            new_t = jnp.concatenate([pad_rows, new[rsl, :]], axis=0).T
            out_ref[s] = jnp.where(lane_w < WINDOW - DEC_SEQ, pltpu.roll(buf_ref[s], WINDOW - DEC_SEQ, 1), new_t)

    tok = lax.broadcasted_iota(jnp.int32, (rows, WINDOW), 0) % DEC_SEQ
    valid_c = lax.broadcasted_iota(jnp.int32, (rows, WINDOW), 1) >= tok
    r_n = lax.broadcasted_iota(jnp.int32, (rows, LANES), 0)
    c_n = lax.broadcasted_iota(jnp.int32, (rows, LANES), 1)
    valid_n = (r_n // DEC_SEQ == c_n // DEC_SEQ) & (c_n % DEC_SEQ <= r_n % DEC_SEQ)
    pad_new = jnp.zeros((LANES - rows, LANES), F32)

    pairs = SWA_GROUP // 2
    zeros_dk = jnp.zeros((SWA_HEAD_DIM, nseq * WINDOW), F32)
    zero_blk = jnp.zeros((DEC_SEQ, WINDOW), F32)
    low = lax.broadcasted_iota(jnp.int32, (rows, LANES), 1) < SWA_HEAD_DIM
    for g in range(SWA_KV_HEADS):
        dsl = slice(g * SWA_HEAD_DIM, (g + 1) * SWA_HEAD_DIM)
        kt = jnp.concatenate([bk_ref[s, dsl, :] for s in range(nseq)], axis=1)
        vt = jnp.concatenate([bv_ref[s, dsl, :] for s in range(nseq)], axis=1)
        lo = (g // 2) * LANES
        kn_tile, vn_tile = kn[:, lo:lo + LANES], vn[:, lo:lo + LANES]
        kv_low = g % 2 == 0
        q_g = jnp.concatenate(q_tiles[g * pairs:(g + 1) * pairs], axis=0).astype(BF16)
        out = None
        for parity in range(2):
            want_low = parity == 0
            stack = (lambda a: jnp.concatenate([a, zeros_dk], axis=0)) if want_low else \
                    (lambda a: jnp.concatenate([zeros_dk, a], axis=0))
            keep = low if want_low else jnp.logical_not(low)
            align = (lambda a: a) if kv_low == want_low else (lambda a: pltpu.roll(a, SWA_HEAD_DIM, 1))
            place = lambda a: jnp.concatenate([jnp.where(keep, align(a), 0.0), pad_new], axis=0).astype(BF16)
            kn_p, vn_p = place(kn_tile), place(vn_tile)
            s_c = _dot(q_g, stack(kt).astype(BF16))
            s_n = _dot_nt(q_g, kn_p)
            pc_rows, pn_rows, inv = [], [], []
            for pair in range(pairs):
                sink = sinks_ref[g * SWA_GROUP + 2 * pair + parity] * LOG2_E
                psl = slice(pair * rows, (pair + 1) * rows)
                sc = jnp.concatenate(
                    [s_c[pair * rows + s * DEC_SEQ: pair * rows + (s + 1) * DEC_SEQ, s * WINDOW:(s + 1) * WINDOW]
                     for s in range(nseq)], axis=0)
                sc = jnp.where(valid_c, sc, NEG_INF)
                sn = jnp.where(valid_n, s_n[psl, :], NEG_INF)
                m = jnp.maximum(jnp.maximum(jnp.max(sc, axis=-1, keepdims=True),
                                            jnp.max(sn, axis=-1, keepdims=True)), sink)
                pc = jnp.exp2(sc - m)
                pn = jnp.exp2(sn - m)
                inv.append(1.0 / (jnp.sum(pc, axis=-1, keepdims=True) + jnp.sum(pn, axis=-1, keepdims=True)
                                  + jnp.exp2(sink - m)))
                pn_rows.append(pn)
                for s in range(nseq):
                    blocks = [pc[s * DEC_SEQ:(s + 1) * DEC_SEQ, :] if s2 == s else zero_blk for s2 in range(nseq)]
                    pc_rows.append(jnp.concatenate(blocks, axis=1))
            pv = (_dot_nt(jnp.concatenate(pc_rows, axis=0).astype(BF16), stack(vt).astype(BF16))
                  + _dot(jnp.concatenate(pn_rows, axis=0).astype(BF16), vn_p))
            pv = jnp.concatenate([pv[pair * rows:(pair + 1) * rows, :] * inv[pair] for pair in range(pairs)], axis=0)
            out = pv if out is None else out + pv
        for pair in range(pairs):
            tile = g * pairs + pair
            ob_ref[:, tile * LANES:(tile + 1) * LANES] = out[pair * rows:(pair + 1) * rows, :].astype(BF16)


def _swa_sample(proj, buf_k, buf_v, sinks, cos_t, sin_t, g_q, g_k):
    t = proj.shape[0]
    nseq = SWA_SAMPLE_BATCH
    rows = nseq * DEC_SEQ
    mean_mat = jnp.asarray(_head_mean_matrix(), BF16)
    qb, kb, vb = COL_SQ // SWA_Q, COL_SK // SWA_KV, COL_SV // SWA_KV
    grid_spec = pltpu.PrefetchScalarGridSpec(
        num_scalar_prefetch=1,
        grid=(t // rows,),
        in_specs=[pl.BlockSpec((rows, SWA_Q), lambda n, s: (n, qb)),
                  pl.BlockSpec((rows, SWA_KV), lambda n, s: (n, kb)),
                  pl.BlockSpec((rows, SWA_KV), lambda n, s: (n, vb)),
                  pl.BlockSpec((nseq, SWA_KV, WINDOW), lambda n, s: (n, 0, 0)),
                  pl.BlockSpec((nseq, SWA_KV, WINDOW), lambda n, s: (n, 0, 0)),
                  pl.BlockSpec((DEC_SEQ, LANES), lambda n, s: (0, 0)),
                  pl.BlockSpec((DEC_SEQ, LANES), lambda n, s: (0, 0)),
                  pl.BlockSpec((1, LANES), lambda n, s: (0, 0)),
                  pl.BlockSpec((1, LANES), lambda n, s: (0, 0)),
                  pl.BlockSpec((MXU_DIM, MXU_DIM), lambda n, s: (0, 0))],
        out_specs=[pl.BlockSpec((rows, SWA_Q), lambda n, s: (n, 0)),
                   pl.BlockSpec((nseq, SWA_KV, WINDOW), lambda n, s: (n, 0, 0)),
                   pl.BlockSpec((nseq, SWA_KV, WINDOW), lambda n, s: (n, 0, 0))],
    )
    return pl.pallas_call(
        _swa_sample_body,
        grid_spec=grid_spec,
        out_shape=[jax.ShapeDtypeStruct((t, SWA_Q), BF16),
                   jax.ShapeDtypeStruct(buf_k.shape, F32),
                   jax.ShapeDtypeStruct(buf_v.shape, F32)],
        compiler_params=_params("arbitrary"),
        name="swa_sample",
    )(sinks, proj, proj, proj, buf_k, buf_v, cos_t, sin_t, g_q, g_k, mean_mat)


def _merge_body(oa_ref, ob_ref, wpa_ref, wpb_ref, ga_ref, gb_ref, m_ref):
    ga = _sigmoid(ga_ref[...].astype(F32))
    gb = _sigmoid(gb_ref[...].astype(F32))
    m_ref[...] = (ga * _dot(oa_ref[...], wpa_ref[...]) + gb * _dot(ob_ref[...], wpb_ref[...])).astype(BF16)


def _merge(oa, ob, proj, w_pa, w_pb, tm=1024, tn=512):
    t, d = oa.shape
    ga_blk, gb_blk = COL_BG // tn, (COL_BG + D_MODEL) // tn
    return pl.pallas_call(
        _merge_body,
        grid=(t // tm, d // tn),
        in_specs=[pl.BlockSpec((tm, d), lambda i, j: (i, 0)),
                  pl.BlockSpec((tm, d), lambda i, j: (i, 0)),
                  pl.BlockSpec((d, tn), lambda i, j: (0, j)),
                  pl.BlockSpec((d, tn), lambda i, j: (0, j)),
                  pl.BlockSpec((tm, tn), lambda i, j: (i, ga_blk + j)),
                  pl.BlockSpec((tm, tn), lambda i, j: (i, gb_blk + j))],
        out_specs=pl.BlockSpec((tm, tn), lambda i, j: (i, j)),
        out_shape=jax.ShapeDtypeStruct((t, d), BF16),
        compiler_params=_params("arbitrary", "arbitrary"),
        name="merge",
    )(oa, ob, w_pa, w_pb, proj, proj)


OPROJ_ROW_CHUNK = 256


def _oproj_body(m_ref, w_ref, x_ref, gate_ref, gn_ref, sc_ref, sh_ref, x1_ref, h2_ref):
    for row0 in range(0, x_ref.shape[0], OPROJ_ROW_CHUNK):
        sl = slice(row0, row0 + OPROJ_ROW_CHUNK)
        y = _dot(m_ref[sl, :], w_ref[...])
        x1 = x_ref[sl, :] + _mod_rows(gate_ref, row0, OPROJ_ROW_CHUNK) * y
        x1_ref[sl, :] = x1
        ms = jnp.mean(x1 * x1, axis=-1, keepdims=True)
        h = (x1 * lax.rsqrt(ms + EPS)) * gn_ref[...]
        h2_ref[sl, :] = (h * (1.0 + _mod_rows(sc_ref, row0, OPROJ_ROW_CHUNK))
                         + _mod_rows(sh_ref, row0, OPROJ_ROW_CHUNK)).astype(BF16)


def _oproj(merged, w_o, x, gate, gn, scale, shift, tm=512):
    t, d = x.shape
    rows = lambda i: (i, 0)
    whole = lambda i: (0, 0)
    mod = lambda m: (pl.BlockSpec((1, d), whole) if m.shape[0] == 1
                     else pl.BlockSpec((tm // DEC_SEQ, d), rows))
    return pl.pallas_call(
        _oproj_body,
        grid=(t // tm,),
        in_specs=[pl.BlockSpec((tm, d), rows),
                  pl.BlockSpec(w_o.shape, whole, pipeline_mode=pl.Buffered(1)),
                  pl.BlockSpec((tm, d), rows),
                  mod(gate),
                  pl.BlockSpec((1, d), whole),
                  mod(scale),
                  mod(shift)],
        out_specs=[pl.BlockSpec((tm, d), rows), pl.BlockSpec((tm, d), rows)],
        out_shape=[jax.ShapeDtypeStruct(x.shape, F32), jax.ShapeDtypeStruct(x.shape, BF16)],
        compiler_params=_params("arbitrary"),
        name="oproj",
    )(merged, w_o, x, gate, gn, scale, shift)


def _ffn_body(h_ref, x_ref, gate_ref, wg_ref, wu_ref, wd_ref, o_ref):
    f = pl.program_id(1)

    @pl.when(f == 0)
    def _():
        def rows(row0):
            o_ref[pl.ds(row0, ROW_CHUNK), :] = jnp.zeros((ROW_CHUNK, o_ref.shape[1]), F32)
        _row_loop(x_ref.shape[0], rows)

    h = h_ref[...]
    a = _silu(_dot(h, wg_ref[...])) * _dot(h, wu_ref[...])
    o_ref[...] += _dot(a.astype(BF16), wd_ref[...])

    @pl.when(f == pl.num_programs(1) - 1)
    def _():
        def rows(row0):
            sl = pl.ds(row0, ROW_CHUNK)
            o_ref[sl, :] = x_ref[sl, :] + _mod_rows(gate_ref, row0, ROW_CHUNK) * o_ref[sl, :]
        _row_loop(x_ref.shape[0], rows)


def _ffn(h, x, gate, w_gate, w_up, w_down, tm=1024, tf=512):
    t, d = x.shape
    dff = w_gate.shape[1]
    zero = lambda j: 0
    return pl.pallas_call(
        _ffn_body,
        grid=(t // tm, dff // tf),
        in_specs=[pl.BlockSpec((tm, d), lambda i, f: (i, 0)),
                  pl.BlockSpec((tm, d), lambda i, f: (i, 0)),
                  _mod_spec(gate, tm, d, zero),
                  pl.BlockSpec((d, tf), lambda i, f: (0, f)),
                  pl.BlockSpec((d, tf), lambda i, f: (0, f)),
                  pl.BlockSpec((tf, d), lambda i, f: (f, 0))],
        out_specs=pl.BlockSpec((tm, d), lambda i, f: (i, 0)),
        out_shape=jax.ShapeDtypeStruct(x.shape, F32),
        compiler_params=_params("arbitrary", "arbitrary"),
        name="ffn",
    )(h, x, gate, w_gate, w_up, w_down)


def _mods(ada):
    return tuple(ada[:, i * D_MODEL:(i + 1) * D_MODEL] for i in range(6))


def _layer_after_inproj(x, proj, glog, mods, pos, state, buf_k, buf_v, wts):
    g_q, g_k, sinks, g_out, w_pa, w_pb, w_o, gn_ffn, w_fg, w_fu, w_fd = wts
    bsz, tlen, d = x.shape
    x2 = x.reshape(bsz * tlen, d)
    _, _, gate_m, shift_f, scale_f, gate_f = mods
    if state is None:
        pa, s_new = _gla_prompt(proj, glog, g_out, w_pa)
        s_new = s_new[None]
        merged, k_new, v_new = _swa_prompt(proj, pa, sinks, g_q, g_k, w_pb)
        k_new = k_new.reshape(bsz, WINDOW, SWA_KV_HEADS, SWA_HEAD_DIM)
        v_new = v_new.reshape(bsz, WINDOW, SWA_KV_HEADS, SWA_HEAD_DIM)
    else:
        to_dev = lambda c: jnp.transpose(c, (0, 2, 3, 1)).reshape(bsz, SWA_KV, WINDOW)
        from_dev = lambda c: jnp.transpose(c.reshape(bsz, SWA_KV_HEADS, SWA_HEAD_DIM, WINDOW), (0, 3, 1, 2))
        cos_t, sin_t = _rope_tables(pos)
        oa, s_new = _gla_sample(proj, glog, state, g_out)
        ob, k_new, v_new = _swa_sample(proj, to_dev(buf_k), to_dev(buf_v), sinks, cos_t, sin_t, g_q, g_k)
        k_new, v_new = from_dev(k_new), from_dev(v_new)
        merged = _merge(oa, ob, proj, w_pa, w_pb)
    x1, h2 = _oproj(merged, w_o, x2, gate_m, gn_ffn, scale_f, shift_f)
    y = _ffn(h2, x1, gate_f, w_fg, w_fu, w_fd)
    return y.reshape(bsz, tlen, d), s_new, k_new, v_new


def kernel(x_prompt, x_sample, c_prompt, c_sample, state_gla, cache_swa_k, cache_swa_v, w_ada, b_ada, g_norm_mix, w_in, w_gk2, b_gk2, g_qnorm, g_knorm, sinks, g_gla_out, w_pa, w_pb, w_o, g_norm_ffn, w_ffn_gate, w_ffn_up, w_ffn_down):
    assert w_in.shape[0] == 1, "single trunk layer"
    pos_p = jnp.arange(SEQ, dtype=jnp.int32)
    pos_s = PAST_LEN + jnp.arange(DEC_SEQ, dtype=jnp.int32)
    l = 0
    wt = jnp.transpose(w_in[l]).astype(BF16)
    wt_lr = jnp.pad(wt[W_IN_GLR:W_IN_GLR + GLA_RANK], ((0, LANES - GLA_RANK), (0, 0)))
    w_gk = jnp.pad(w_gk2[l], ((0, LANES - GLA_RANK), (0, 0))).astype(BF16)
    in_wts = (wt, wt_lr, w_gk, b_gk2[l].reshape(1, GLA_KEY))
    gn_mix = g_norm_mix[l].reshape(1, D_MODEL)

    n_c = 1 + DEC_BATCH
    pad = (-n_c) % (2 * SUBLANES)
    c_all = jnp.pad(jnp.concatenate([c_prompt, c_sample], axis=0), ((0, pad), (0, 0)))
    ada = _ada(c_all, w_ada[l], b_ada[l].reshape(1, -1))
    mods_p, mods_s = _mods(ada[0:1]), _mods(ada[1:n_c])

    later_weights = (w_pa[l], w_pb[l], w_o[l], w_ffn_gate[l], w_ffn_up[l], w_ffn_down[l])
    proj_p, glog_p, *later_bf16 = _inproj(x_prompt.reshape(-1, D_MODEL), gn_mix, mods_p[1], mods_p[0], *in_wts,
                                          cast_weights=later_weights)
    proj_s, glog_s = _inproj(x_sample.reshape(-1, D_MODEL), gn_mix, mods_s[1], mods_s[0], *in_wts)
    w_pa_b, w_pb_b, w_o_b, w_fg_b, w_fu_b, w_fd_b = later_bf16
    wts = (jnp.tile(g_qnorm[l], 2).reshape(1, LANES), jnp.tile(g_knorm[l], 2).reshape(1, LANES),
           sinks[l], g_gla_out[l].reshape(1, GLA_DV), w_pa_b, w_pb_b, w_o_b,
           g_norm_ffn[l].reshape(1, D_MODEL), w_fg_b, w_fu_b, w_fd_b)

    yp, sp, kp, vp = _layer_after_inproj(x_prompt, proj_p, glog_p, mods_p, pos_p, None, None, None, wts)
    ys, ss, ks, vs = _layer_after_inproj(x_sample, proj_s, glog_s, mods_s, pos_s, state_gla[l], cache_swa_k[l],
                                         cache_swa_v[l], wts)
    return (yp, ys, sp[None], kp[None], vp[None], ss[None], ks[None], vs[None])
```

```python
import functools

import numpy as np
import jax
import jax.numpy as jnp
from jax import lax
from jax.experimental import pallas as pl
from jax.experimental.pallas import tpu as pltpu

F32 = jnp.float32
BF16 = jnp.bfloat16

D_MODEL = 2048
SEQ = 16384
DEC_BATCH = 128
DEC_SEQ = 8
PAST_LEN = 16384
GLA_HEADS = 4
GLA_DK = 256
GLA_DV = 512
GLA_KEY = GLA_HEADS * GLA_DK
GLA_VAL = GLA_HEADS * GLA_DV
GLA_RANK = 16
GLA_TAU = 16.0
SWA_HEAD_DIM = 64
SWA_HEADS = 32
SWA_KV_HEADS = 4
SWA_GROUP = SWA_HEADS // SWA_KV_HEADS
SWA_Q = SWA_HEADS * SWA_HEAD_DIM
SWA_KV = SWA_KV_HEADS * SWA_HEAD_DIM
WINDOW = 128
ROPE_THETA = 10000.0
D_FF = 5632
EPS = 1e-6
NEG_INF = -1e30
LOG2_E = 1.4426950408889634
SWA_SCORE_SCALE = (SWA_HEAD_DIM ** -0.5) * LOG2_E

SUBLANES = 8
LANES = 128
MXU_DIM = 256
VMEM_LIMIT_BYTES = 58 * 1024 * 1024

COL_GQ = 0
COL_GK = COL_GQ + GLA_KEY
COL_GV = COL_GK + GLA_KEY
COL_GR = COL_GV + GLA_VAL
COL_SQ = COL_GR + GLA_VAL
COL_SK = COL_SQ + SWA_Q
COL_SV = COL_SK + SWA_KV
COL_BG = COL_SV + SWA_KV
PROJ_COLS = COL_BG + 2 * D_MODEL
W_IN_GLR = COL_GR + GLA_VAL

GLA_CHUNK = 256
GLA_ONE_SIDED_MAX_DECAY = 80.0
GLA_SAMPLE_BATCH = 16
SWA_SAMPLE_BATCH = 4
SWA_PROMPT_BLOCKS_PER_STEP = 1
ROW_CHUNK = 256


def _dot(a, b):
    return jnp.dot(a, b, preferred_element_type=F32)


def _dot_nt(a, b):
    return lax.dot_general(a, b, (((1,), (1,)), ((), ())), preferred_element_type=F32)


def _dot_tn(a, b):
    return lax.dot_general(a, b, (((0,), (0,)), ((), ())), preferred_element_type=F32)


def _sigmoid(x):
    return 1.0 / (1.0 + jnp.exp(-x))


def _silu(x):
    return x * _sigmoid(x)


def _split3(x):
    hi = x.astype(BF16)
    r1 = x - hi.astype(F32)
    mid = r1.astype(BF16)
    lo = (r1 - mid.astype(F32)).astype(BF16)
    return hi, mid, lo


def _params(*sem):
    return pltpu.CompilerParams(dimension_semantics=sem, vmem_limit_bytes=VMEM_LIMIT_BYTES)


def _ada_body(c_ref, w_ref, b_ref, o_ref):
    a = _silu(c_ref[...]).astype(BF16)
    o_ref[...] = _dot(a, w_ref[...].astype(BF16)) + b_ref[...]


def _ada(c_all, w_ada, b_ada, tn=512):
    m, d = c_all.shape
    n = w_ada.shape[1]
    return pl.pallas_call(
        _ada_body,
        grid=(n // tn,),
        in_specs=[pl.BlockSpec((m, d), lambda j: (0, 0)),
                  pl.BlockSpec((d, tn), lambda j: (0, j)),
                  pl.BlockSpec((1, tn), lambda j: (0, j))],
        out_specs=pl.BlockSpec((m, tn), lambda j: (0, j)),
        out_shape=jax.ShapeDtypeStruct((m, n), F32),
        compiler_params=_params("arbitrary"),
        name="ada",
    )(c_all, w_ada, b_ada)


def _mod_spec(mod, tm, width, col):
    if mod.shape[0] == 1:
        return pl.BlockSpec((1, width), lambda i, j: (0, col(j)))
    return pl.BlockSpec((tm // DEC_SEQ, width), lambda i, j: (i, col(j)))


def _mod_rows(ref, row0, nrows):
    if ref.shape[0] == 1:
        return ref[...]
    seq0 = row0 // DEC_SEQ
    return jnp.concatenate(
        [jnp.broadcast_to(ref[pl.ds(seq0 + s, 1), :], (DEC_SEQ, ref.shape[1])) for s in range(nrows // DEC_SEQ)],
        axis=0)


def _modulated_norm_rows(x_ref, gn_ref, sc_ref, sh_ref, row0, nrows):
    x = x_ref[pl.ds(row0, nrows), :]
    ms = jnp.mean(x * x, axis=-1, keepdims=True)
    y = (x * lax.rsqrt(ms + EPS)) * gn_ref[...]
    return y * (1.0 + _mod_rows(sc_ref, row0, nrows)) + _mod_rows(sh_ref, row0, nrows)


def _row_loop(total_rows, body):
    assert total_rows % ROW_CHUNK == 0, (total_rows, ROW_CHUNK)

    def step(r, carry):
        body(pl.multiple_of(r * ROW_CHUNK, ROW_CHUNK))
        return carry
    lax.fori_loop(0, total_rows // ROW_CHUNK, step, 0)


def _inproj_body(x_ref, gn_ref, sc_ref, sh_ref, wt_ref, wlr_ref, wgk_ref, bgk_ref, *rest, n_casts):
    cast_src = rest[:n_casts]
    proj_ref, glog_ref = rest[n_casts:n_casts + 2]
    cast_dst = rest[n_casts + 2:2 * n_casts + 2]
    h_scr = rest[-1]
    for src, dst in zip(cast_src, cast_dst):
        dst[...] = src[...].astype(BF16)

    @pl.when(pl.program_id(1) == 0)
    def _():
        def rows(row0):
            hb = _modulated_norm_rows(x_ref, gn_ref, sc_ref, sh_ref, row0, ROW_CHUNK).astype(BF16)
            h_scr[pl.ds(row0, ROW_CHUNK), :] = hb
            glr = _dot_nt(hb, wlr_ref[...])
            z = _dot(glr.astype(BF16), wgk_ref[...]) + bgk_ref[...]
            log_sig = jnp.minimum(z, 0.0) - jnp.log1p(jnp.exp(-jnp.abs(z)))
            glog_ref[pl.ds(row0, ROW_CHUNK), :] = log_sig * (1.0 / GLA_TAU)
        _row_loop(x_ref.shape[0], rows)

    proj_ref[...] = _dot_nt(h_scr[...], wt_ref[...]).astype(BF16)


def _cast_block_spec(shape, row_tiles, col_steps):
    rows, cols = shape
    br = rows // row_tiles
    assert br * row_tiles == rows and br % (2 * SUBLANES) == 0
    ncb = max(c for c in range(1, col_steps + 1) if cols % c == 0 and (cols // c) % LANES == 0)
    return pl.BlockSpec((br, cols // ncb), lambda i, j: (i, jnp.minimum(j, ncb - 1)))


def _inproj(x, gn, scale, shift, wt, wt_lr, w_gk, b_gk, cast_weights=(), tm=2048, tn=512):
    t, d = x.shape
    tm = min(tm, t)
    n = wt.shape[0] - GLA_RANK
    n_first = W_IN_GLR // tn
    grid = (t // tm, n // tn)
    zero = lambda j: 0
    wt_rows = lambda i, j: (pl.multiple_of(j * tn + jnp.where(j >= n_first, GLA_RANK, 0), GLA_RANK), 0)
    cast_specs = [_cast_block_spec(w.shape, *grid) for w in cast_weights]
    return pl.pallas_call(
        functools.partial(_inproj_body, n_casts=len(cast_weights)),
        grid=grid,
        in_specs=[pl.BlockSpec((tm, d), lambda i, j: (i, 0), pipeline_mode=pl.Buffered(1)),
                  pl.BlockSpec((1, d), lambda i, j: (0, 0)),
                  _mod_spec(scale, tm, d, zero),
                  _mod_spec(shift, tm, d, zero),
                  pl.BlockSpec((pl.Element(tn), pl.Element(d)), wt_rows),
                  pl.BlockSpec(wt_lr.shape, lambda i, j: (0, 0)),
                  pl.BlockSpec(w_gk.shape, lambda i, j: (0, 0)),
                  pl.BlockSpec(b_gk.shape, lambda i, j: (0, 0))] + cast_specs,
        out_specs=[pl.BlockSpec((tm, tn), lambda i, j: (i, j)),
                   pl.BlockSpec((tm, GLA_KEY), lambda i, j: (i, 0))] + cast_specs,
        out_shape=[jax.ShapeDtypeStruct((t, n), BF16),
                   jax.ShapeDtypeStruct((t, GLA_KEY), F32)]
                  + [jax.ShapeDtypeStruct(w.shape, BF16) for w in cast_weights],
        scratch_shapes=[pltpu.VMEM((tm, d), BF16)],
        compiler_params=_params("arbitrary", "arbitrary"),
        name="inproj",
    )(x, gn, scale, shift, wt, wt_lr, w_gk, b_gk, *cast_weights)


def _gla_level_table(c, group):
    n_levels = int(np.log2(group))
    i = np.arange(c)[:, None]
    j = np.arange(c)[None, :]
    x = np.bitwise_xor(i, j)
    lvl = np.floor(np.log2(np.maximum(x, 1))).astype(np.int32)
    lvl = np.where(i == j, n_levels, lvl)
    valid = (i >= j) & (i // group == j // group)
    return np.where(valid, lvl, -1).astype(np.int32), n_levels


def _gla_tril(c, group):
    i = np.arange(c)[:, None]
    j = np.arange(c)[None, :]
    return ((i >= j) & (i // group == j // group)).astype(np.float32)


def _level_reference(b_scr, rows, level):
    s = 2 ** (level + 1)
    width = b_scr.shape[1]
    pieces = []
    if s >= SUBLANES:
        for blk in range(rows // s):
            mid = blk * s + s // 2
            pieces.append(jnp.broadcast_to(b_scr[mid:mid + 1, :], (s, width)))
    else:
        p = lax.broadcasted_iota(jnp.int32, (SUBLANES, width), 0)
        for tile in range(rows // SUBLANES):
            base = tile * SUBLANES
            mids = [base + q * s + s // 2 for q in range(SUBLANES // s)]
            r = jnp.broadcast_to(b_scr[mids[-1]:mids[-1] + 1, :], (SUBLANES, width))
            for q in range(SUBLANES // s - 2, -1, -1):
                row = jnp.broadcast_to(b_scr[mids[q]:mids[q] + 1, :], (SUBLANES, width))
                r = jnp.where(p < (q + 1) * s, row, r)
            pieces.append(r)
    return jnp.concatenate(pieces, axis=0) if len(pieces) > 1 else pieces[0]


def _gla_intra(q, k, k_bf, b, b_scr, lvl, n_levels):
    rows = q.shape[0]
    a = jnp.where(lvl == n_levels, _dot_nt(q.astype(BF16), k_bf), 0.0)
    for level in range(n_levels):
        f = jnp.exp(-jnp.abs(b - _level_reference(b_scr, rows, level)))
        p = _dot_nt((q * f).astype(BF16), (k * f).astype(BF16))
        a = jnp.where(lvl == level, p, a)
    return a


def _gla_cumsum(g, tril_bf):
    hi, mid, lo = _split3(g)
    return _dot(tril_bf, hi) + _dot(tril_bf, mid) + _dot(tril_bf, lo)


def _gla_out_gate(o, gout, gr):
    ms = jnp.mean(o * o, axis=-1, keepdims=True)
    y = (o * lax.rsqrt(ms + EPS)) * gout
    return (y * _silu(gr.astype(F32))).astype(BF16)


def _gla_prompt_body(q_ref, k_ref, v_ref, gr_ref, g_ref, tril_ref, lvl_ref, gout_ref, wpa_ref,
                     pa_ref, st_ref, st_scr, b_scr, a_scr, oa_scr, *, n_levels, heads):
    c = pl.program_id(0)
    n_chunks = pl.num_programs(0) - 1
    rows = q_ref.shape[0]
    cur = lax.rem(c, 2)
    prev = 1 - cur
    half = pa_ref.shape[1] // 2

    @pl.when(c == 0)
    def _():
        st_scr[...] = jnp.zeros_like(st_scr)
        oa_scr[...] = jnp.zeros_like(oa_scr)

    tril, lvl = tril_ref[...], lvl_ref[...]
    per_head = []
    mild = None
    for h in range(heads):
        ksl = slice(h * GLA_DK, (h + 1) * GLA_DK)
        b = _gla_cumsum(g_ref[:, ksl], tril)
        b_scr[h] = b
        b_end = b_scr[h, rows - 1:rows, :]
        q = q_ref[:, ksl].astype(F32) * (GLA_DK ** -0.5)
        k_bf = k_ref[:, ksl]
        k = k_bf.astype(F32)
        st = st_scr[h]
        qd = (q * jnp.exp(b)).astype(BF16)
        o_inter = _dot_nt(qd, st.astype(BF16))
        head_mild = jnp.min(b_end) > -GLA_ONE_SIDED_MAX_DECAY
        mild = head_mild if mild is None else jnp.logical_and(mild, head_mild)
        per_head.append((b, b_end, q, k, k_bf, st, qd, o_inter))

    pa_ref[:, :half] = _dot(oa_scr[prev], wpa_ref[:, :half])

    @pl.when(mild)
    def _():
        for h, (b, _, _, k, _, _, qd, _) in enumerate(per_head):
            a_scr[h] = jnp.where(lvl >= 0, _dot_nt(qd, (k * jnp.exp(-b)).astype(BF16)), 0.0)

    @pl.when(jnp.logical_not(mild))
    def _():
        for h, (b, _, q, k, k_bf, _, _, _) in enumerate(per_head):
            a_scr[h] = _gla_intra(q, k, k_bf, b, b_scr.at[h], lvl, n_levels)

    outs = []
    for h, (b, b_end, _, k, _, st, _, o_inter) in enumerate(per_head):
        v = v_ref[:, h * GLA_DV:(h + 1) * GLA_DV]
        outs.append(o_inter + _dot(a_scr[h].astype(BF16), v))
        k_dec = (k * jnp.exp(b_end - b)).astype(BF16)
        st_scr[h] = st * jnp.exp(b_end) + _dot_tn(v, k_dec)
    pa_ref[:, half:] = _dot(oa_scr[prev], wpa_ref[:, half:])
    for h, o in enumerate(outs):
        vsl = slice(h * GLA_DV, (h + 1) * GLA_DV)
        oa_scr[cur, :, vsl] = _gla_out_gate(o, gout_ref[...], gr_ref[:, vsl])

    @pl.when(c == n_chunks - 1)
    def _():
        for h in range(heads):
            st_ref[h] = st_scr[h].T


def _gla_prompt(proj, glog, g_out, w_pa):
    t = proj.shape[0]
    c = GLA_CHUNK
    n_chunks = t // c
    heads = GLA_HEADS
    lvl, n_levels = _gla_level_table(c, c)
    tril = jnp.asarray(_gla_tril(c, c), BF16)
    lvl = jnp.asarray(lvl)
    qb, kb, vb, rb = COL_GQ // GLA_KEY, COL_GK // GLA_KEY, COL_GV // GLA_VAL, COL_GR // GLA_VAL
    chunk = lambda i: jnp.minimum(i, n_chunks - 1)
    whole = lambda i: (0, 0)
    return pl.pallas_call(
        functools.partial(_gla_prompt_body, n_levels=n_levels, heads=heads),
        grid=(n_chunks + 1,),
        in_specs=[pl.BlockSpec((c, GLA_KEY), lambda i: (chunk(i), qb)),
                  pl.BlockSpec((c, GLA_KEY), lambda i: (chunk(i), kb)),
                  pl.BlockSpec((c, GLA_VAL), lambda i: (chunk(i), vb)),
                  pl.BlockSpec((c, GLA_VAL), lambda i: (chunk(i), rb)),
                  pl.BlockSpec((c, GLA_KEY), lambda i: (chunk(i), 0)),
                  pl.BlockSpec((c, c), whole),
                  pl.BlockSpec((c, c), whole),
                  pl.BlockSpec((1, GLA_DV), whole),
                  pl.BlockSpec(w_pa.shape, whole, pipeline_mode=pl.Buffered(1))],
        out_specs=[pl.BlockSpec((c, w_pa.shape[1]), lambda i: (jnp.maximum(i - 1, 0), 0)),
                   pl.BlockSpec((heads, GLA_DK, GLA_DV), lambda i: (0, 0, 0))],
        out_shape=[jax.ShapeDtypeStruct((t, w_pa.shape[1]), F32),
                   jax.ShapeDtypeStruct((heads, GLA_DK, GLA_DV), F32)],
        scratch_shapes=[pltpu.VMEM((heads, GLA_DV, GLA_DK), F32), pltpu.VMEM((heads, c, GLA_DK), F32),
                        pltpu.VMEM((heads, c, c), F32), pltpu.VMEM((2, c, GLA_VAL), BF16)],
        compiler_params=_params("arbitrary"),
        name="gla_prompt",
    )(proj, proj, proj, proj, glog, tril, lvl, g_out, w_pa)


def _gla_sample_body(q_ref, k_ref, v_ref, gr_ref, g_ref, s0_ref, tril_ref, lvl_ref, gout_ref,
                     oa_ref, s1_ref, b_scr, *, n_levels):
    rows = q_ref.shape[0]
    nseq = rows // DEC_SEQ
    b = _gla_cumsum(g_ref[...], tril_ref[...])
    b_scr[...] = b
    b_end = jnp.concatenate(
        [jnp.broadcast_to(b_scr[(s + 1) * DEC_SEQ - 1:(s + 1) * DEC_SEQ, :], (DEC_SEQ, GLA_DK))
         for s in range(nseq)], axis=0)
    q = q_ref[...].astype(F32) * (GLA_DK ** -0.5)
    k_bf = k_ref[...]
    k = k_bf.astype(F32)
    v = v_ref[...]
    qd = (q * jnp.exp(b)).astype(BF16)
    a = _gla_intra(q, k, k_bf, b, b_scr, lvl_ref[...], n_levels)
    o_intra = _dot(a.astype(BF16), v)

    stacked = jnp.concatenate([k * jnp.exp(b_end - b), jnp.exp(b_end)], axis=0)
    stacked_t = stacked.T
    kd_t = stacked_t[:, :rows]
    lane = lax.broadcasted_iota(jnp.int32, kd_t.shape, 1)
    outs = []
    for s in range(nseq):
        s0 = s0_ref[s, 0]
        outs.append(_dot(qd[s * DEC_SEQ:(s + 1) * DEC_SEQ, :], s0.astype(BF16)))
        in_seq = (lane >= s * DEC_SEQ) & (lane < (s + 1) * DEC_SEQ)
        kd_s = jnp.where(in_seq, kd_t, 0.0).astype(BF16)
        col = rows + s * DEC_SEQ
        decay = stacked_t[:, col:col + 1]
        s1_ref[s, 0] = s0 * decay + _dot(kd_s, v)
    o = jnp.concatenate(outs, axis=0) + o_intra
    oa_ref[...] = _gla_out_gate(o, gout_ref[...], gr_ref[...])


def _gla_sample(proj, glog, state, g_out):
    t = proj.shape[0]
    nseq = GLA_SAMPLE_BATCH
    rows = nseq * DEC_SEQ
    lvl, n_levels = _gla_level_table(rows, DEC_SEQ)
    tril = jnp.asarray(_gla_tril(rows, DEC_SEQ), BF16)
    lvl = jnp.asarray(lvl)
    qb, kb, vb, rb = COL_GQ // GLA_DK, COL_GK // GLA_DK, COL_GV // GLA_DV, COL_GR // GLA_DV
    return pl.pallas_call(
        functools.partial(_gla_sample_body, n_levels=n_levels),
        grid=(t // rows, GLA_HEADS),
        in_specs=[pl.BlockSpec((rows, GLA_DK), lambda i, h: (i, qb + h)),
                  pl.BlockSpec((rows, GLA_DK), lambda i, h: (i, kb + h)),
                  pl.BlockSpec((rows, GLA_DV), lambda i, h: (i, vb + h)),
                  pl.BlockSpec((rows, GLA_DV), lambda i, h: (i, rb + h)),
                  pl.BlockSpec((rows, GLA_DK), lambda i, h: (i, h)),
                  pl.BlockSpec((nseq, 1, GLA_DK, GLA_DV), lambda i, h: (i, h, 0, 0)),
                  pl.BlockSpec((rows, rows), lambda i, h: (0, 0)),
                  pl.BlockSpec((rows, rows), lambda i, h: (0, 0)),
                  pl.BlockSpec((1, GLA_DV), lambda i, h: (0, 0))],
        out_specs=[pl.BlockSpec((rows, GLA_DV), lambda i, h: (i, h)),
                   pl.BlockSpec((nseq, 1, GLA_DK, GLA_DV), lambda i, h: (i, h, 0, 0))],
        out_shape=[jax.ShapeDtypeStruct((t, GLA_VAL), BF16),
                   jax.ShapeDtypeStruct(state.shape, F32)],
        scratch_shapes=[pltpu.VMEM((rows, GLA_DK), F32)],
        compiler_params=_params("arbitrary", "arbitrary"),
        name="gla_sample",
    )(proj, proj, proj, proj, glog, state, tril, lvl, g_out)


def _rope_tables(pos):
    half = SWA_HEAD_DIM // 2
    inv_freq = ROPE_THETA ** (-(jnp.arange(half, dtype=F32) * 2.0) / SWA_HEAD_DIM)
    ang = pos.astype(F32)[:, None] * inv_freq[None, :]
    cos, sin = jnp.cos(ang), jnp.sin(ang)
    cos_t = jnp.concatenate([cos, cos, cos, cos], axis=-1)
    sin_t = jnp.concatenate([-sin, sin, -sin, sin], axis=-1)
    return cos_t, sin_t


def _head_mean_matrix():
    i = np.arange(MXU_DIM)[:, None] // SWA_HEAD_DIM
    j = np.arange(MXU_DIM)[None, :] // SWA_HEAD_DIM
    return (i == j).astype(np.float32) / SWA_HEAD_DIM


def _qk_norm_rope(x, gain, cos_t, sin_t, mean_mat):
    rows, width = x.shape
    lane = lax.broadcasted_iota(jnp.int32, (rows, LANES), 1)
    first_half = (lane % SWA_HEAD_DIM) < (SWA_HEAD_DIM // 2)
    shift = SWA_HEAD_DIM // 2
    sq = (x * x).astype(BF16)
    tiles = []
    for c in range(width // MXU_DIM):
        ms = _dot(sq[:, c * MXU_DIM:(c + 1) * MXU_DIM], mean_mat)
        for t in range(MXU_DIM // LANES):
            lo = c * MXU_DIM + t * LANES
            y = (x[:, lo:lo + LANES] * lax.rsqrt(ms[:, t * LANES:(t + 1) * LANES] + EPS)) * gain
            rot = jnp.where(first_half, pltpu.roll(y, LANES - shift, 1), pltpu.roll(y, shift, 1))
            tiles.append(y * cos_t + rot * sin_t)
    return tiles


def _attend_all_heads(q_tiles, k_all, v_all, valid, sinks_ref, store):
    pairs = SWA_GROUP // 2
    rows = q_tiles[0].shape[0]
    low = lax.broadcasted_iota(jnp.int32, (k_all.shape[0], LANES), 1) < SWA_HEAD_DIM
    chains = [(g, parity) for g in range(SWA_KV_HEADS) for parity in range(2)]
    q_group = {}

    def scores(g, parity):
        lo = (g // 2) * LANES
        have_low, want_low = g % 2 == 0, parity == 0

        def place(x):
            src = x if have_low == want_low else pltpu.roll(x, SWA_HEAD_DIM, 1)
            return jnp.where(low if want_low else jnp.logical_not(low), src, 0.0).astype(BF16)
        if g not in q_group:
            q_group[g] = jnp.concatenate(q_tiles[g * pairs:(g + 1) * pairs], axis=0).astype(BF16)
        return _dot_nt(q_group[g], place(k_all[:, lo:lo + LANES])), place(v_all[:, lo:lo + LANES])

    def softmax(g, parity, s_all):
        p_list, inv_list = [], []
        for pair in range(pairs):
            sink = sinks_ref[g * SWA_GROUP + 2 * pair + parity] * LOG2_E
            s = jnp.where(valid, s_all[pair * rows:(pair + 1) * rows, :], NEG_INF)
            m = jnp.maximum(jnp.max(s, axis=-1, keepdims=True), sink)
            p = jnp.exp2(s - m)
            inv_list.append(1.0 / (jnp.sum(p, axis=-1, keepdims=True) + jnp.exp2(sink - m)))
            p_list.append(p.astype(BF16))
        return jnp.concatenate(p_list, axis=0), inv_list

    partial = None
    for g, parity in chains:
        s_all, vv = scores(g, parity)
        p_all, inv_list = softmax(g, parity, s_all)
        pv = _dot(p_all, vv)
        outs = [pv[pair * rows:(pair + 1) * rows, :] * inv_list[pair] for pair in range(pairs)]
        if parity == 0:
            partial = outs
        else:
            for pair in range(pairs):
                store(g * pairs + pair, (partial[pair] + outs[pair]).astype(BF16))


def _swa_prompt_body(sinks_ref, q_ref, k_ref, v_ref, cb_ref, sb_ref, cr_ref, sr_ref, crs_ref, srs_ref,
                     gq_ref, gk_ref, mm_ref, pa_ref, ga_ref, gb_ref, wpb_ref,
                     m_ref, knew_ref, vnew_ref, kprev_scr, vprev_scr, ob_scr):
    step = pl.program_id(0)
    n = jnp.minimum(step, pl.num_programs(0) - 2)
    blk = WINDOW
    nsub = q_ref.shape[0] // blk
    cur = lax.rem(step, 2)
    prev = 1 - cur

    @pl.when(step == 0)
    def _():
        kprev_scr[...] = jnp.zeros_like(kprev_scr)
        vprev_scr[...] = jnp.zeros_like(vprev_scr)
        ob_scr[...] = jnp.zeros_like(ob_scr)

    mean_mat = mm_ref[...]
    qi = lax.broadcasted_iota(jnp.int32, (blk, 2 * blk), 0)
    kj = lax.broadcasted_iota(jnp.int32, (blk, 2 * blk), 1)
    dist = blk + qi - kj
    in_window = (dist >= 0) & (dist <= WINDOW)
    k_prev, v_prev = kprev_scr[...], vprev_scr[...]
    for sub in range(nsub):
        rsl = slice(sub * blk, (sub + 1) * blk)
        b = n * nsub + sub
        cb, sb = cb_ref[pl.ds(b, 1), :], sb_ref[pl.ds(b, 1), :]
        cos_t = cb * cr_ref[...] - sb * sr_ref[...]
        sin_t = sb * crs_ref[...] + cb * srs_ref[...]
        q_tiles = [t * SWA_SCORE_SCALE
                   for t in _qk_norm_rope(q_ref[rsl, :].astype(F32), gq_ref[...], cos_t, sin_t, mean_mat)]
        kn = jnp.concatenate(_qk_norm_rope(k_ref[rsl, :].astype(F32), gk_ref[...], cos_t, sin_t, mean_mat), axis=1)
        vn = v_ref[rsl, :].astype(F32)
        k_all = jnp.concatenate([k_prev, kn], axis=0)
        v_all = jnp.concatenate([v_prev, vn], axis=0)
        valid = in_window & ((b - 1) * blk + kj >= 0)
        if sub == 0:
            m_ref[...] = (_sigmoid(ga_ref[...].astype(F32)) * pa_ref[...]
                          + _sigmoid(gb_ref[...].astype(F32)) * _dot(ob_scr[prev], wpb_ref[...])).astype(BF16)

        def store(tile, value, rsl=rsl):
            ob_scr[cur, rsl, tile * LANES:(tile + 1) * LANES] = value
        _attend_all_heads(q_tiles, k_all, v_all, valid, sinks_ref, store)
        k_prev, v_prev = kn, vn

    knew_ref[...] = k_prev
    vnew_ref[...] = v_prev
    kprev_scr[...] = k_prev
    vprev_scr[...] = v_prev


def _rope_block_tables(nblocks, blk):
    half = SWA_HEAD_DIM // 2
    inv_freq = ROPE_THETA ** (-(jnp.arange(half, dtype=F32) * 2.0) / SWA_HEAD_DIM)
    lane_freq = jnp.tile(inv_freq, LANES // half)[None, :]
    sign = jnp.tile(jnp.concatenate([-jnp.ones((half,), F32), jnp.ones((half,), F32)]), LANES // SWA_HEAD_DIM)[None, :]
    ang_b = (jnp.arange(nblocks, dtype=jnp.int32) * blk).astype(F32)[:, None] * lane_freq
    ang_r = jnp.arange(blk, dtype=jnp.int32).astype(F32)[:, None] * lane_freq
    cr, sr = jnp.cos(ang_r), jnp.sin(ang_r)
    return jnp.cos(ang_b), jnp.sin(ang_b), cr, sr, sign * cr, sign * sr


def _swa_prompt(proj, pa, sinks, g_q, g_k, w_pb):
    t = proj.shape[0]
    d = w_pb.shape[1]
    blk = WINDOW
    rows = SWA_PROMPT_BLOCKS_PER_STEP * blk
    nblocks, nsteps = t // blk, t // rows
    mean_mat = jnp.asarray(_head_mean_matrix(), BF16)
    tables = _rope_block_tables(nblocks, blk)
    qb, kb, vb = COL_SQ // SWA_Q, COL_SK // SWA_KV, COL_SV // SWA_KV
    whole = lambda n, s: (0, 0)
    cur = lambda n: jnp.minimum(n, nsteps - 1)
    prev = lambda n: jnp.maximum(n - 1, 0)
    gate_rows = lambda n: pl.multiple_of(prev(n) * rows, rows)
    grid_spec = pltpu.PrefetchScalarGridSpec(
        num_scalar_prefetch=1,
        grid=(nsteps + 1,),
        in_specs=[pl.BlockSpec((rows, SWA_Q), lambda n, s: (cur(n), qb)),
                  pl.BlockSpec((rows, SWA_KV), lambda n, s: (cur(n), kb)),
                  pl.BlockSpec((rows, SWA_KV), lambda n, s: (cur(n), vb)),
                  pl.BlockSpec((nblocks, LANES), whole),
                  pl.BlockSpec((nblocks, LANES), whole),
                  pl.BlockSpec((blk, LANES), whole),
                  pl.BlockSpec((blk, LANES), whole),
                  pl.BlockSpec((blk, LANES), whole),
                  pl.BlockSpec((blk, LANES), whole),
                  pl.BlockSpec((1, LANES), whole),
                  pl.BlockSpec((1, LANES), whole),
                  pl.BlockSpec((MXU_DIM, MXU_DIM), whole),
                  pl.BlockSpec((rows, d), lambda n, s: (prev(n), 0)),
                  pl.BlockSpec((pl.Element(rows), pl.Element(d)), lambda n, s: (gate_rows(n), COL_BG)),
                  pl.BlockSpec((pl.Element(rows), pl.Element(d)), lambda n, s: (gate_rows(n), COL_BG + d)),
                  pl.BlockSpec(w_pb.shape, whole, pipeline_mode=pl.Buffered(1))],
        out_specs=[pl.BlockSpec((rows, d), lambda n, s: (prev(n), 0)),
                   pl.BlockSpec((blk, SWA_KV), whole),
                   pl.BlockSpec((blk, SWA_KV), whole)],
        scratch_shapes=[pltpu.VMEM((blk, SWA_KV), F32), pltpu.VMEM((blk, SWA_KV), F32),
                        pltpu.VMEM((2, rows, SWA_Q), BF16)],
    )
    return pl.pallas_call(
        _swa_prompt_body,
        grid_spec=grid_spec,
        out_shape=[jax.ShapeDtypeStruct((t, d), BF16),
                   jax.ShapeDtypeStruct((blk, SWA_KV), F32),
                   jax.ShapeDtypeStruct((blk, SWA_KV), F32)],
        compiler_params=_params("arbitrary"),
        name="swa_prompt",
    )(sinks, proj, proj, proj, *tables, g_q, g_k, mean_mat, pa, proj, proj, w_pb)


def _swa_sample_body(sinks_ref, q_ref, k_ref, v_ref, bk_ref, bv_ref, cos_ref, sin_ref, gq_ref, gk_ref, mm_ref,
                     ob_ref, kout_ref, vout_ref):
    nseq = bk_ref.shape[0]
    rows = nseq * DEC_SEQ
    cos_t = jnp.concatenate([cos_ref[...]] * nseq, axis=0)
    sin_t = jnp.concatenate([sin_ref[...]] * nseq, axis=0)
    mean_mat = mm_ref[...]
    q_tiles = [t * SWA_SCORE_SCALE
               for t in _qk_norm_rope(q_ref[...].astype(F32), gq_ref[...], cos_t, sin_t, mean_mat)]
    kn = jnp.concatenate(_qk_norm_rope(k_ref[...].astype(F32), gk_ref[...], cos_t, sin_t, mean_mat), axis=1)
    vn = v_ref[...].astype(F32)

    lane_w = lax.broadcasted_iota(jnp.int32, (SWA_KV, WINDOW), 1)
    pad_rows = jnp.zeros((WINDOW - DEC_SEQ, SWA_KV), F32)
    for s in range(nseq):
        rsl = slice(s * DEC_SEQ, (s + 1) * DEC_SEQ)
        for buf_ref, new, out_ref in ((bk_ref, kn, kout_ref), (bv_ref, vn, vout_ref)):
            new_t = jnp.concatenate([pad_rows, new[rsl, :]], axis=0).T
            out_ref[s] = jnp.where(lane_w < WINDOW - DEC_SEQ, pltpu.roll(buf_ref[s], WINDOW - DEC_SEQ, 1), new_t)

    tok = lax.broadcasted_iota(jnp.int32, (rows, WINDOW), 0) % DEC_SEQ
    valid_c = lax.broadcasted_iota(jnp.int32, (rows, WINDOW), 1) >= tok
    r_n = lax.broadcasted_iota(jnp.int32, (rows, LANES), 0)
    c_n = lax.broadcasted_iota(jnp.int32, (rows, LANES), 1)
    valid_n = (r_n // DEC_SEQ == c_n // DEC_SEQ) & (c_n % DEC_SEQ <= r_n % DEC_SEQ)
    pad_new = jnp.zeros((LANES - rows, LANES), F32)

    pairs = SWA_GROUP // 2
    zeros_dk = jnp.zeros((SWA_HEAD_DIM, nseq * WINDOW), F32)
    zero_blk = jnp.zeros((DEC_SEQ, WINDOW), F32)
    low = lax.broadcasted_iota(jnp.int32, (rows, LANES), 1) < SWA_HEAD_DIM
    for g in range(SWA_KV_HEADS):
        dsl = slice(g * SWA_HEAD_DIM, (g + 1) * SWA_HEAD_DIM)
        kt = jnp.concatenate([bk_ref[s, dsl, :] for s in range(nseq)], axis=1)
        vt = jnp.concatenate([bv_ref[s, dsl, :] for s in range(nseq)], axis=1)
        lo = (g // 2) * LANES
        kn_tile, vn_tile = kn[:, lo:lo + LANES], vn[:, lo:lo + LANES]
        kv_low = g % 2 == 0
        q_g = jnp.concatenate(q_tiles[g * pairs:(g + 1) * pairs], axis=0).astype(BF16)
        out = None
        for parity in range(2):
            want_low = parity == 0
            stack = (lambda a: jnp.concatenate([a, zeros_dk], axis=0)) if want_low else \
                    (lambda a: jnp.concatenate([zeros_dk, a], axis=0))
            keep = low if want_low else jnp.logical_not(low)
            align = (lambda a: a) if kv_low == want_low else (lambda a: pltpu.roll(a, SWA_HEAD_DIM, 1))
            place = lambda a: jnp.concatenate([jnp.where(keep, align(a), 0.0), pad_new], axis=0).astype(BF16)
            kn_p, vn_p = place(kn_tile), place(vn_tile)
            s_c = _dot(q_g, stack(kt).astype(BF16))
            s_n = _dot_nt(q_g, kn_p)
            pc_rows, pn_rows, inv = [], [], []
            for pair in range(pairs):
                sink = sinks_ref[g * SWA_GROUP + 2 * pair + parity] * LOG2_E
                psl = slice(pair * rows, (pair + 1) * rows)
                sc = jnp.concatenate(
                    [s_c[pair * rows + s * DEC_SEQ: pair * rows + (s + 1) * DEC_SEQ, s * WINDOW:(s + 1) * WINDOW]
                     for s in range(nseq)], axis=0)
                sc = jnp.where(valid_c, sc, NEG_INF)
                sn = jnp.where(valid_n, s_n[psl, :], NEG_INF)
                m = jnp.maximum(jnp.maximum(jnp.max(sc, axis=-1, keepdims=True),
                                            jnp.max(sn, axis=-1, keepdims=True)), sink)
                pc = jnp.exp2(sc - m)
                pn = jnp.exp2(sn - m)
                inv.append(1.0 / (jnp.sum(pc, axis=-1, keepdims=True) + jnp.sum(pn, axis=-1, keepdims=True)
                                  + jnp.exp2(sink - m)))
                pn_rows.append(pn)
                for s in range(nseq):
                    blocks = [pc[s * DEC_SEQ:(s + 1) * DEC_SEQ, :] if s2 == s else zero_blk for s2 in range(nseq)]
                    pc_rows.append(jnp.concatenate(blocks, axis=1))
            pv = (_dot_nt(jnp.concatenate(pc_rows, axis=0).astype(BF16), stack(vt).astype(BF16))
                  + _dot(jnp.concatenate(pn_rows, axis=0).astype(BF16), vn_p))
            pv = jnp.concatenate([pv[pair * rows:(pair + 1) * rows, :] * inv[pair] for pair in range(pairs)], axis=0)
            out = pv if out is None else out + pv
        for pair in range(pairs):
            tile = g * pairs + pair
            ob_ref[:, tile * LANES:(tile + 1) * LANES] = out[pair * rows:(pair + 1) * rows, :].astype(BF16)


def _swa_sample(proj, buf_k, buf_v, sinks, cos_t, sin_t, g_q, g_k):
    t = proj.shape[0]
    nseq = SWA_SAMPLE_BATCH
    rows = nseq * DEC_SEQ
    mean_mat = jnp.asarray(_head_mean_matrix(), BF16)
    qb, kb, vb = COL_SQ // SWA_Q, COL_SK // SWA_KV, COL_SV // SWA_KV
    grid_spec = pltpu.PrefetchScalarGridSpec(
        num_scalar_prefetch=1,
        grid=(t // rows,),
        in_specs=[pl.BlockSpec((rows, SWA_Q), lambda n, s: (n, qb)),
                  pl.BlockSpec((rows, SWA_KV), lambda n, s: (n, kb)),
                  pl.BlockSpec((rows, SWA_KV), lambda n, s: (n, vb)),
                  pl.BlockSpec((nseq, SWA_KV, WINDOW), lambda n, s: (n, 0, 0)),
                  pl.BlockSpec((nseq, SWA_KV, WINDOW), lambda n, s: (n, 0, 0)),
                  pl.BlockSpec((DEC_SEQ, LANES), lambda n, s: (0, 0)),
                  pl.BlockSpec((DEC_SEQ, LANES), lambda n, s: (0, 0)),
                  pl.BlockSpec((1, LANES), lambda n, s: (0, 0)),
                  pl.BlockSpec((1, LANES), lambda n, s: (0, 0)),
                  pl.BlockSpec((MXU_DIM, MXU_DIM), lambda n, s: (0, 0))],
        out_specs=[pl.BlockSpec((rows, SWA_Q), lambda n, s: (n, 0)),
                   pl.BlockSpec((nseq, SWA_KV, WINDOW), lambda n, s: (n, 0, 0)),
                   pl.BlockSpec((nseq, SWA_KV, WINDOW), lambda n, s: (n, 0, 0))],
    )
    return pl.pallas_call(
        _swa_sample_body,
        grid_spec=grid_spec,
        out_shape=[jax.ShapeDtypeStruct((t, SWA_Q), BF16),
                   jax.ShapeDtypeStruct(buf_k.shape, F32),
                   jax.ShapeDtypeStruct(buf_v.shape, F32)],
        compiler_params=_params("arbitrary"),
        name="swa_sample",
    )(sinks, proj, proj, proj, buf_k, buf_v, cos_t, sin_t, g_q, g_k, mean_mat)


def _merge_body(oa_ref, ob_ref, wpa_ref, wpb_ref, ga_ref, gb_ref, m_ref):
    ga = _sigmoid(ga_ref[...].astype(F32))
    gb = _sigmoid(gb_ref[...].astype(F32))
    m_ref[...] = (ga * _dot(oa_ref[...], wpa_ref[...]) + gb * _dot(ob_ref[...], wpb_ref[...])).astype(BF16)


def _merge(oa, ob, proj, w_pa, w_pb, tm=1024, tn=512):
    t, d = oa.shape
    ga_blk, gb_blk = COL_BG // tn, (COL_BG + D_MODEL) // tn
    return pl.pallas_call(
        _merge_body,
        grid=(t // tm, d // tn),
        in_specs=[pl.BlockSpec((tm, d), lambda i, j: (i, 0)),
                  pl.BlockSpec((tm, d), lambda i, j: (i, 0)),
                  pl.BlockSpec((d, tn), lambda i, j: (0, j)),
                  pl.BlockSpec((d, tn), lambda i, j: (0, j)),
                  pl.BlockSpec((tm, tn), lambda i, j: (i, ga_blk + j)),
                  pl.BlockSpec((tm, tn), lambda i, j: (i, gb_blk + j))],
        out_specs=pl.BlockSpec((tm, tn), lambda i, j: (i, j)),
        out_shape=jax.ShapeDtypeStruct((t, d), BF16),
        compiler_params=_params("arbitrary", "arbitrary"),
        name="merge",
    )(oa, ob, w_pa, w_pb, proj, proj)


OPROJ_ROW_CHUNK = 256


def _oproj_body(m_ref, w_ref, x_ref, gate_ref, gn_ref, sc_ref, sh_ref, x1_ref, h2_ref):
    for row0 in range(0, x_ref.shape[0], OPROJ_ROW_CHUNK):
        sl = slice(row0, row0 + OPROJ_ROW_CHUNK)
        y = _dot(m_ref[sl, :], w_ref[...])
        x1 = x_ref[sl, :] + _mod_rows(gate_ref, row0, OPROJ_ROW_CHUNK) * y
        x1_ref[sl, :] = x1
        ms = jnp.mean(x1 * x1, axis=-1, keepdims=True)
        h = (x1 * lax.rsqrt(ms + EPS)) * gn_ref[...]
        h2_ref[sl, :] = (h * (1.0 + _mod_rows(sc_ref, row0, OPROJ_ROW_CHUNK))
                         + _mod_rows(sh_ref, row0, OPROJ_ROW_CHUNK)).astype(BF16)


def _oproj(merged, w_o, x, gate, gn, scale, shift, tm=512):
    t, d = x.shape
    rows = lambda i: (i, 0)
    whole = lambda i: (0, 0)
    mod = lambda m: (pl.BlockSpec((1, d), whole) if m.shape[0] == 1
                     else pl.BlockSpec((tm // DEC_SEQ, d), rows))
    return pl.pallas_call(
        _oproj_body,
        grid=(t // tm,),
        in_specs=[pl.BlockSpec((tm, d), rows),
                  pl.BlockSpec(w_o.shape, whole, pipeline_mode=pl.Buffered(1)),
                  pl.BlockSpec((tm, d), rows),
                  mod(gate),
                  pl.BlockSpec((1, d), whole),
                  mod(scale),
                  mod(shift)],
        out_specs=[pl.BlockSpec((tm, d), rows), pl.BlockSpec((tm, d), rows)],
        out_shape=[jax.ShapeDtypeStruct(x.shape, F32), jax.ShapeDtypeStruct(x.shape, BF16)],
        compiler_params=_params("arbitrary"),
        name="oproj",
    )(merged, w_o, x, gate, gn, scale, shift)


def _ffn_body(h_ref, x_ref, gate_ref, wg_ref, wu_ref, wd_ref, o_ref):
    f = pl.program_id(1)

    @pl.when(f == 0)
    def _():
        def rows(row0):
            o_ref[pl.ds(row0, ROW_CHUNK), :] = jnp.zeros((ROW_CHUNK, o_ref.shape[1]), F32)
        _row_loop(x_ref.shape[0], rows)

    h = h_ref[...]
    a = _silu(_dot(h, wg_ref[...])) * _dot(h, wu_ref[...])
    o_ref[...] += _dot(a.astype(BF16), wd_ref[...])

    @pl.when(f == pl.num_programs(1) - 1)
    def _():
        def rows(row0):
            sl = pl.ds(row0, ROW_CHUNK)
            o_ref[sl, :] = x_ref[sl, :] + _mod_rows(gate_ref, row0, ROW_CHUNK) * o_ref[sl, :]
        _row_loop(x_ref.shape[0], rows)


def _ffn(h, x, gate, w_gate, w_up, w_down, tm=1024, tf=512):
    t, d = x.shape
    dff = w_gate.shape[1]
    zero = lambda j: 0
    return pl.pallas_call(
        _ffn_body,
        grid=(t // tm, dff // tf),
        in_specs=[pl.BlockSpec((tm, d), lambda i, f: (i, 0)),
                  pl.BlockSpec((tm, d), lambda i, f: (i, 0)),
                  _mod_spec(gate, tm, d, zero),
                  pl.BlockSpec((d, tf), lambda i, f: (0, f)),
                  pl.BlockSpec((d, tf), lambda i, f: (0, f)),
                  pl.BlockSpec((tf, d), lambda i, f: (f, 0))],
        out_specs=pl.BlockSpec((tm, d), lambda i, f: (i, 0)),
        out_shape=jax.ShapeDtypeStruct(x.shape, F32),
        compiler_params=_params("arbitrary", "arbitrary"),
        name="ffn",
    )(h, x, gate, w_gate, w_up, w_down)


def _mods(ada):
    return tuple(ada[:, i * D_MODEL:(i + 1) * D_MODEL] for i in range(6))


def _layer_after_inproj(x, proj, glog, mods, pos, state, buf_k, buf_v, wts):
    g_q, g_k, sinks, g_out, w_pa, w_pb, w_o, gn_ffn, w_fg, w_fu, w_fd = wts
    bsz, tlen, d = x.shape
    x2 = x.reshape(bsz * tlen, d)
    _, _, gate_m, shift_f, scale_f, gate_f = mods
    if state is None:
        pa, s_new = _gla_prompt(proj, glog, g_out, w_pa)
        s_new = s_new[None]
        merged, k_new, v_new = _swa_prompt(proj, pa, sinks, g_q, g_k, w_pb)
        k_new = k_new.reshape(bsz, WINDOW, SWA_KV_HEADS, SWA_HEAD_DIM)
        v_new = v_new.reshape(bsz, WINDOW, SWA_KV_HEADS, SWA_HEAD_DIM)
    else:
        to_dev = lambda c: jnp.transpose(c, (0, 2, 3, 1)).reshape(bsz, SWA_KV, WINDOW)
        from_dev = lambda c: jnp.transpose(c.reshape(bsz, SWA_KV_HEADS, SWA_HEAD_DIM, WINDOW), (0, 3, 1, 2))
        cos_t, sin_t = _rope_tables(pos)
        oa, s_new = _gla_sample(proj, glog, state, g_out)
        ob, k_new, v_new = _swa_sample(proj, to_dev(buf_k), to_dev(buf_v), sinks, cos_t, sin_t, g_q, g_k)
        k_new, v_new = from_dev(k_new), from_dev(v_new)
        merged = _merge(oa, ob, proj, w_pa, w_pb)
    x1, h2 = _oproj(merged, w_o, x2, gate_m, gn_ffn, scale_f, shift_f)
    y = _ffn(h2, x1, gate_f, w_fg, w_fu, w_fd)
    return y.reshape(bsz, tlen, d), s_new, k_new, v_new


def kernel(x_prompt, x_sample, c_prompt, c_sample, state_gla, cache_swa_k, cache_swa_v, w_ada, b_ada, g_norm_mix, w_in, w_gk2, b_gk2, g_qnorm, g_knorm, sinks, g_gla_out, w_pa, w_pb, w_o, g_norm_ffn, w_ffn_gate, w_ffn_up, w_ffn_down):
    assert w_in.shape[0] == 1, "single trunk layer"
    pos_p = jnp.arange(SEQ, dtype=jnp.int32)
    pos_s = PAST_LEN + jnp.arange(DEC_SEQ, dtype=jnp.int32)
    l = 0
    wt = jnp.transpose(w_in[l]).astype(BF16)
    wt_lr = jnp.pad(wt[W_IN_GLR:W_IN_GLR + GLA_RANK], ((0, LANES - GLA_RANK), (0, 0)))
    w_gk = jnp.pad(w_gk2[l], ((0, LANES - GLA_RANK), (0, 0))).astype(BF16)
    in_wts = (wt, wt_lr, w_gk, b_gk2[l].reshape(1, GLA_KEY))
    gn_mix = g_norm_mix[l].reshape(1, D_MODEL)

    n_c = 1 + DEC_BATCH
    pad = (-n_c) % (2 * SUBLANES)
    c_all = jnp.pad(jnp.concatenate([c_prompt, c_sample], axis=0), ((0, pad), (0, 0)))
    ada = _ada(c_all, w_ada[l], b_ada[l].reshape(1, -1))
    mods_p, mods_s = _mods(ada[0:1]), _mods(ada[1:n_c])

    later_weights = (w_pa[l], w_pb[l], w_o[l], w_ffn_gate[l], w_ffn_up[l], w_ffn_down[l])
    proj_p, glog_p, *later_bf16 = _inproj(x_prompt.reshape(-1, D_MODEL), gn_mix, mods_p[1], mods_p[0], *in_wts,
                                          cast_weights=later_weights)
    proj_s, glog_s = _inproj(x_sample.reshape(-1, D_MODEL), gn_mix, mods_s[1], mods_s[0], *in_wts)
    w_pa_b, w_pb_b, w_o_b, w_fg_b, w_fu_b, w_fd_b = later_bf16
    wts = (jnp.tile(g_qnorm[l], 2).reshape(1, LANES), jnp.tile(g_knorm[l], 2).reshape(1, LANES),
           sinks[l], g_gla_out[l].reshape(1, GLA_DV), w_pa_b, w_pb_b, w_o_b,
           g_norm_ffn[l].reshape(1, D_MODEL), w_fg_b, w_fu_b, w_fd_b)

    yp, sp, kp, vp = _layer_after_inproj(x_prompt, proj_p, glog_p, mods_p, pos_p, None, None, None, wts)
    ys, ss, ks, vs = _layer_after_inproj(x_sample, proj_s, glog_s, mods_s, pos_s, state_gla[l], cache_swa_k[l],
                                         cache_swa_v[l], wts)
    return (yp, ys, sp[None], kp[None], vp[None], ss[None], ks[None], vs[None])
```

```python
import functools

import numpy as np
import jax
import jax.numpy as jnp
from jax import lax
from jax.experimental import pallas as pl
from jax.experimental.pallas import tpu as pltpu

F32 = jnp.float32
BF16 = jnp.bfloat16

D_MODEL = 2048
SEQ = 16384
DEC_BATCH = 128
DEC_SEQ = 8
PAST_LEN = 16384
GLA_HEADS = 4
GLA_DK = 256
GLA_DV = 512
GLA_KEY = GLA_HEADS * GLA_DK
GLA_VAL = GLA_HEADS * GLA_DV
GLA_RANK = 16
GLA_TAU = 16.0
SWA_HEAD_DIM = 64
SWA_HEADS = 32
SWA_KV_HEADS = 4
SWA_GROUP = SWA_HEADS // SWA_KV_HEADS
SWA_Q = SWA_HEADS * SWA_HEAD_DIM
SWA_KV = SWA_KV_HEADS * SWA_HEAD_DIM
WINDOW = 128
ROPE_THETA = 10000.0
D_FF = 5632
EPS = 1e-6
NEG_INF = -1e30
LOG2_E = 1.4426950408889634
SWA_SCORE_SCALE = (SWA_HEAD_DIM ** -0.5) * LOG2_E

SUBLANES = 8
LANES = 128
MXU_DIM = 256
VMEM_LIMIT_BYTES = 58 * 1024 * 1024

COL_GQ = 0
COL_GK = COL_GQ + GLA_KEY
COL_GV = COL_GK + GLA_KEY
COL_GR = COL_GV + GLA_VAL
COL_SQ = COL_GR + GLA_VAL
COL_SK = COL_SQ + SWA_Q
COL_SV = COL_SK + SWA_KV
COL_BG = COL_SV + SWA_KV
PROJ_COLS = COL_BG + 2 * D_MODEL
W_IN_GLR = COL_GR + GLA_VAL

GLA_CHUNK = 256
GLA_ONE_SIDED_MAX_DECAY = 80.0
GLA_SAMPLE_BATCH = 16
SWA_SAMPLE_BATCH = 4
SWA_PROMPT_BLOCKS_PER_STEP = 1
ROW_CHUNK = 256
INPROJ_ROW_PIECE = 512


def _dot(a, b):
    return jnp.dot(a, b, preferred_element_type=F32)


def _dot_nt(a, b):
    return lax.dot_general(a, b, (((1,), (1,)), ((), ())), preferred_element_type=F32)


def _dot_tn(a, b):
    return lax.dot_general(a, b, (((0,), (0,)), ((), ())), preferred_element_type=F32)


def _sigmoid(x):
    return 1.0 / (1.0 + jnp.exp(-x))


def _silu(x):
    return x * _sigmoid(x)


def _split3(x):
    hi = x.astype(BF16)
    r1 = x - hi.astype(F32)
    mid = r1.astype(BF16)
    lo = (r1 - mid.astype(F32)).astype(BF16)
    return hi, mid, lo


def _params(*sem):
    return pltpu.CompilerParams(dimension_semantics=sem, vmem_limit_bytes=VMEM_LIMIT_BYTES)


def _ada_body(c_ref, w_ref, b_ref, wt_ref, o_ref, wtb_ref):
    a = _silu(c_ref[...]).astype(BF16)
    o_ref[...] = _dot(a, w_ref[...].astype(BF16)) + b_ref[...]
    wtb_ref[...] = wt_ref[...].astype(BF16)


def _ada(c_all, w_ada, b_ada, wt, tn=512):
    m, d = c_all.shape
    n = w_ada.shape[1]
    steps = n // tn
    rows = wt.shape[0]
    align = 2 * SUBLANES
    slab = -(-rows // (steps * align)) * align
    assert (steps - 1) * slab < rows <= steps * slab
    slab_spec = pl.BlockSpec((slab, wt.shape[1]), lambda j: (j, 0))
    return pl.pallas_call(
        _ada_body,
        grid=(steps,),
        in_specs=[pl.BlockSpec((m, d), lambda j: (0, 0)),
                  pl.BlockSpec((d, tn), lambda j: (0, j)),
                  pl.BlockSpec((1, tn), lambda j: (0, j)),
                  slab_spec],
        out_specs=[pl.BlockSpec((m, tn), lambda j: (0, j)), slab_spec],
        out_shape=[jax.ShapeDtypeStruct((m, n), F32), jax.ShapeDtypeStruct(wt.shape, BF16)],
        compiler_params=_params("arbitrary"),
        name="ada",
    )(c_all, w_ada, b_ada, wt)


def _mod_spec(mod, tm, width, col):
    if mod.shape[0] == 1:
        return pl.BlockSpec((1, width), lambda i, j: (0, col(j)))
    return pl.BlockSpec((tm // DEC_SEQ, width), lambda i, j: (i, col(j)))


def _mod_rows(ref, row0, nrows):
    if ref.shape[0] == 1:
        return ref[...]
    seq0 = row0 // DEC_SEQ
    return jnp.concatenate(
        [jnp.broadcast_to(ref[pl.ds(seq0 + s, 1), :], (DEC_SEQ, ref.shape[1])) for s in range(nrows // DEC_SEQ)],
        axis=0)


def _modulated_norm_rows(x_ref, gn_ref, sc_ref, sh_ref, row0, nrows):
    x = x_ref[pl.ds(row0, nrows), :]
    ms = jnp.mean(x * x, axis=-1, keepdims=True)
    y = (x * lax.rsqrt(ms + EPS)) * gn_ref[...]
    return y * (1.0 + _mod_rows(sc_ref, row0, nrows)) + _mod_rows(sh_ref, row0, nrows)


def _row_loop(total_rows, body):
    assert total_rows % ROW_CHUNK == 0, (total_rows, ROW_CHUNK)

    def step(r, carry):
        body(pl.multiple_of(r * ROW_CHUNK, ROW_CHUNK))
        return carry
    lax.fori_loop(0, total_rows // ROW_CHUNK, step, 0)


def _inproj_body(x_ref, gn_ref, sc_ref, sh_ref, wt_ref, wlr_ref, wgk_ref, bgk_ref, *rest, n_casts):
    cast_src = rest[:n_casts]
    proj_ref, glog_ref = rest[n_casts:n_casts + 2]
    cast_dst = rest[n_casts + 2:2 * n_casts + 2]
    h_scr = rest[-1]
    for src, dst in zip(cast_src, cast_dst):
        dst[...] = src[...].astype(BF16)

    @pl.when(pl.program_id(1) == 0)
    def _():
        def rows(row0):
            hb = _modulated_norm_rows(x_ref, gn_ref, sc_ref, sh_ref, row0, ROW_CHUNK).astype(BF16)
            h_scr[pl.ds(row0, ROW_CHUNK), :] = hb
            glr = _dot_nt(hb, wlr_ref[...])
            z = _dot(glr.astype(BF16), wgk_ref[...]) + bgk_ref[...]
            log_sig = jnp.minimum(z, 0.0) - jnp.log1p(jnp.exp(-jnp.abs(z)))
            glog_ref[pl.ds(row0, ROW_CHUNK), :] = log_sig * (1.0 / GLA_TAU)
        _row_loop(x_ref.shape[0], rows)

    piece = min(INPROJ_ROW_PIECE, proj_ref.shape[0])
    for row0 in range(0, proj_ref.shape[0], piece):
        sl = slice(row0, row0 + piece)
        proj_ref[sl, :] = _dot_nt(h_scr[sl, :], wt_ref[...]).astype(BF16)


def _cast_block_spec(shape, row_tiles, col_steps):
    rows, cols = shape
    br = rows // row_tiles
    assert br * row_tiles == rows and br % (2 * SUBLANES) == 0
    ncb = max(c for c in range(1, col_steps + 1) if cols % c == 0 and (cols // c) % LANES == 0)
    return pl.BlockSpec((br, cols // ncb), lambda i, j: (i, jnp.minimum(j, ncb - 1)))


def _inproj(x, gn, scale, shift, wt, wt_lr, w_gk, b_gk, cast_weights=(), tm=2048, tn=512):
    t, d = x.shape
    tm = min(tm, t)
    n = wt.shape[0] - GLA_RANK
    n_first = W_IN_GLR // tn
    grid = (t // tm, n // tn)
    zero = lambda j: 0
    wt_rows = lambda i, j: (pl.multiple_of(j * tn + jnp.where(j >= n_first, GLA_RANK, 0), GLA_RANK), 0)
    cast_specs = [_cast_block_spec(w.shape, *grid) for w in cast_weights]
    return pl.pallas_call(
        functools.partial(_inproj_body, n_casts=len(cast_weights)),
        grid=grid,
        in_specs=[pl.BlockSpec((tm, d), lambda i, j: (i, 0), pipeline_mode=pl.Buffered(1)),
                  pl.BlockSpec((1, d), lambda i, j: (0, 0)),
                  _mod_spec(scale, tm, d, zero),
                  _mod_spec(shift, tm, d, zero),
                  pl.BlockSpec((pl.Element(tn), pl.Element(d)), wt_rows),
                  pl.BlockSpec(wt_lr.shape, lambda i, j: (0, 0)),
                  pl.BlockSpec(w_gk.shape, lambda i, j: (0, 0)),
                  pl.BlockSpec(b_gk.shape, lambda i, j: (0, 0))] + cast_specs,
        out_specs=[pl.BlockSpec((tm, tn), lambda i, j: (i, j)),
                   pl.BlockSpec((tm, GLA_KEY), lambda i, j: (i, 0))] + cast_specs,
        out_shape=[jax.ShapeDtypeStruct((t, n), BF16),
                   jax.ShapeDtypeStruct((t, GLA_KEY), F32)]
                  + [jax.ShapeDtypeStruct(w.shape, BF16) for w in cast_weights],
        scratch_shapes=[pltpu.VMEM((tm, d), BF16)],
        compiler_params=_params("arbitrary", "arbitrary"),
        name="inproj",
    )(x, gn, scale, shift, wt, wt_lr, w_gk, b_gk, *cast_weights)


def _gla_level_table(c, group):
    n_levels = int(np.log2(group))
    i = np.arange(c)[:, None]
    j = np.arange(c)[None, :]
    x = np.bitwise_xor(i, j)
    lvl = np.floor(np.log2(np.maximum(x, 1))).astype(np.int32)
    lvl = np.where(i == j, n_levels, lvl)
    valid = (i >= j) & (i // group == j // group)
    return np.where(valid, lvl, -1).astype(np.int32), n_levels


def _gla_tril(c, group):
    i = np.arange(c)[:, None]
    j = np.arange(c)[None, :]
    return ((i >= j) & (i // group == j // group)).astype(np.float32)


def _level_reference(b_scr, rows, level):
    s = 2 ** (level + 1)
    width = b_scr.shape[1]
    pieces = []
    if s >= SUBLANES:
        for blk in range(rows // s):
            mid = blk * s + s // 2
            pieces.append(jnp.broadcast_to(b_scr[mid:mid + 1, :], (s, width)))
    else:
        p = lax.broadcasted_iota(jnp.int32, (SUBLANES, width), 0)
        for tile in range(rows // SUBLANES):
            base = tile * SUBLANES
            mids = [base + q * s + s // 2 for q in range(SUBLANES // s)]
            r = jnp.broadcast_to(b_scr[mids[-1]:mids[-1] + 1, :], (SUBLANES, width))
            for q in range(SUBLANES // s - 2, -1, -1):
                row = jnp.broadcast_to(b_scr[mids[q]:mids[q] + 1, :], (SUBLANES, width))
                r = jnp.where(p < (q + 1) * s, row, r)
            pieces.append(r)
    return jnp.concatenate(pieces, axis=0) if len(pieces) > 1 else pieces[0]


def _gla_intra(q, k, k_bf, b, b_scr, lvl, n_levels):
    rows = q.shape[0]
    a = jnp.where(lvl == n_levels, _dot_nt(q.astype(BF16), k_bf), 0.0)
    for level in range(n_levels):
        f = jnp.exp(-jnp.abs(b - _level_reference(b_scr, rows, level)))
        p = _dot_nt((q * f).astype(BF16), (k * f).astype(BF16))
        a = jnp.where(lvl == level, p, a)
    return a


def _gla_cumsum(g, tril_bf):
    hi, mid, lo = _split3(g)
    return _dot(tril_bf, hi) + _dot(tril_bf, mid) + _dot(tril_bf, lo)


def _gla_out_gate(o, gout, gr):
    ms = jnp.mean(o * o, axis=-1, keepdims=True)
    y = (o * lax.rsqrt(ms + EPS)) * gout
    return (y * _silu(gr.astype(F32))).astype(BF16)


def _gla_prompt_body(q_ref, k_ref, v_ref, gr_ref, g_ref, tril_ref, lvl_ref, gout_ref, wpa_ref,
                     pa_ref, st_ref, st_scr, b_scr, a_scr, oa_scr, *, n_levels, heads):
    c = pl.program_id(0)
    n_chunks = pl.num_programs(0) - 1
    rows = q_ref.shape[0]
    cur = lax.rem(c, 2)
    prev = 1 - cur
    half = pa_ref.shape[1] // 2

    @pl.when(c == 0)
    def _():
        st_scr[...] = jnp.zeros_like(st_scr)
        oa_scr[...] = jnp.zeros_like(oa_scr)

    tril, lvl = tril_ref[...], lvl_ref[...]
    per_head = []
    mild = None
    for h in range(heads):
        ksl = slice(h * GLA_DK, (h + 1) * GLA_DK)
        b = _gla_cumsum(g_ref[:, ksl], tril)
        b_scr[h] = b
        b_end = b_scr[h, rows - 1:rows, :]
        q = q_ref[:, ksl].astype(F32) * (GLA_DK ** -0.5)
        k_bf = k_ref[:, ksl]
        k = k_bf.astype(F32)
        st = st_scr[h]
        qd = (q * jnp.exp(b)).astype(BF16)
        o_inter = _dot_nt(qd, st.astype(BF16))
        head_mild = jnp.min(b_end) > -GLA_ONE_SIDED_MAX_DECAY
        mild = head_mild if mild is None else jnp.logical_and(mild, head_mild)
        per_head.append((b, b_end, q, k, k_bf, st, qd, o_inter))

    pa_ref[:, :half] = _dot(oa_scr[prev], wpa_ref[:, :half])

    @pl.when(mild)
    def _():
        for h, (b, _, _, k, _, _, qd, _) in enumerate(per_head):
            a_scr[h] = jnp.where(lvl >= 0, _dot_nt(qd, (k * jnp.exp(-b)).astype(BF16)), 0.0)

    @pl.when(jnp.logical_not(mild))
    def _():
        for h, (b, _, q, k, k_bf, _, _, _) in enumerate(per_head):
            a_scr[h] = _gla_intra(q, k, k_bf, b, b_scr.at[h], lvl, n_levels)

    outs = []
    for h, (b, b_end, _, k, _, st, _, o_inter) in enumerate(per_head):
        v = v_ref[:, h * GLA_DV:(h + 1) * GLA_DV]
        outs.append(o_inter + _dot(a_scr[h].astype(BF16), v))
        k_dec = (k * jnp.exp(b_end - b)).astype(BF16)
        st_scr[h] = st * jnp.exp(b_end) + _dot_tn(v, k_dec)
    pa_ref[:, half:] = _dot(oa_scr[prev], wpa_ref[:, half:])
    for h, o in enumerate(outs):
        vsl = slice(h * GLA_DV, (h + 1) * GLA_DV)
        oa_scr[cur, :, vsl] = _gla_out_gate(o, gout_ref[...], gr_ref[:, vsl])

    @pl.when(c == n_chunks - 1)
    def _():
        for h in range(heads):
            st_ref[h] = st_scr[h].T


def _gla_prompt(proj, glog, g_out, w_pa):
    t = proj.shape[0]
    c = GLA_CHUNK
    n_chunks = t // c
    heads = GLA_HEADS
    lvl, n_levels = _gla_level_table(c, c)
    tril = jnp.asarray(_gla_tril(c, c), BF16)
    lvl = jnp.asarray(lvl)
    qb, kb, vb, rb = COL_GQ // GLA_KEY, COL_GK // GLA_KEY, COL_GV // GLA_VAL, COL_GR // GLA_VAL
    chunk = lambda i: jnp.minimum(i, n_chunks - 1)
    whole = lambda i: (0, 0)
    return pl.pallas_call(
        functools.partial(_gla_prompt_body, n_levels=n_levels, heads=heads),
        grid=(n_chunks + 1,),
        in_specs=[pl.BlockSpec((c, GLA_KEY), lambda i: (chunk(i), qb)),
                  pl.BlockSpec((c, GLA_KEY), lambda i: (chunk(i), kb)),
                  pl.BlockSpec((c, GLA_VAL), lambda i: (chunk(i), vb)),
                  pl.BlockSpec((c, GLA_VAL), lambda i: (chunk(i), rb)),
                  pl.BlockSpec((c, GLA_KEY), lambda i: (chunk(i), 0)),
                  pl.BlockSpec((c, c), whole),
                  pl.BlockSpec((c, c), whole),
                  pl.BlockSpec((1, GLA_DV), whole),
                  pl.BlockSpec(w_pa.shape, whole, pipeline_mode=pl.Buffered(1))],
        out_specs=[pl.BlockSpec((c, w_pa.shape[1]), lambda i: (jnp.maximum(i - 1, 0), 0)),
                   pl.BlockSpec((heads, GLA_DK, GLA_DV), lambda i: (0, 0, 0))],
        out_shape=[jax.ShapeDtypeStruct((t, w_pa.shape[1]), F32),
                   jax.ShapeDtypeStruct((heads, GLA_DK, GLA_DV), F32)],
        scratch_shapes=[pltpu.VMEM((heads, GLA_DV, GLA_DK), F32), pltpu.VMEM((heads, c, GLA_DK), F32),
                        pltpu.VMEM((heads, c, c), F32), pltpu.VMEM((2, c, GLA_VAL), BF16)],
        compiler_params=_params("arbitrary"),
        name="gla_prompt",
    )(proj, proj, proj, proj, glog, tril, lvl, g_out, w_pa)


def _gla_sample_body(q_ref, k_ref, v_ref, gr_ref, g_ref, s0_ref, tril_ref, lvl_ref, gout_ref,
                     oa_ref, s1_ref, b_scr, *, n_levels):
    rows = q_ref.shape[0]
    nseq = rows // DEC_SEQ
    b = _gla_cumsum(g_ref[...], tril_ref[...])
    b_scr[...] = b
    b_end = jnp.concatenate(
        [jnp.broadcast_to(b_scr[(s + 1) * DEC_SEQ - 1:(s + 1) * DEC_SEQ, :], (DEC_SEQ, GLA_DK))
         for s in range(nseq)], axis=0)
    q = q_ref[...].astype(F32) * (GLA_DK ** -0.5)
    k_bf = k_ref[...]
    k = k_bf.astype(F32)
    v = v_ref[...]
    qd = (q * jnp.exp(b)).astype(BF16)
    a = _gla_intra(q, k, k_bf, b, b_scr, lvl_ref[...], n_levels)
    o_intra = _dot(a.astype(BF16), v)

    stacked = jnp.concatenate([k * jnp.exp(b_end - b), jnp.exp(b_end)], axis=0)
    stacked_t = stacked.T
    kd_t = stacked_t[:, :rows]
    lane = lax.broadcasted_iota(jnp.int32, kd_t.shape, 1)
    outs = []
    for s in range(nseq):
        s0 = s0_ref[s, 0]
        outs.append(_dot(qd[s * DEC_SEQ:(s + 1) * DEC_SEQ, :], s0.astype(BF16)))
        in_seq = (lane >= s * DEC_SEQ) & (lane < (s + 1) * DEC_SEQ)
        kd_s = jnp.where(in_seq, kd_t, 0.0).astype(BF16)
        col = rows + s * DEC_SEQ
        decay = stacked_t[:, col:col + 1]
        s1_ref[s, 0] = s0 * decay + _dot(kd_s, v)
    o = jnp.concatenate(outs, axis=0) + o_intra
    oa_ref[...] = _gla_out_gate(o, gout_ref[...], gr_ref[...])


def _gla_sample(proj, glog, state, g_out):
    t = proj.shape[0]
    nseq = GLA_SAMPLE_BATCH
    rows = nseq * DEC_SEQ
    lvl, n_levels = _gla_level_table(rows, DEC_SEQ)
    tril = jnp.asarray(_gla_tril(rows, DEC_SEQ), BF16)
    lvl = jnp.asarray(lvl)
    qb, kb, vb, rb = COL_GQ // GLA_DK, COL_GK // GLA_DK, COL_GV // GLA_DV, COL_GR // GLA_DV
    return pl.pallas_call(
        functools.partial(_gla_sample_body, n_levels=n_levels),
        grid=(t // rows, GLA_HEADS),
        in_specs=[pl.BlockSpec((rows, GLA_DK), lambda i, h: (i, qb + h)),
                  pl.BlockSpec((rows, GLA_DK), lambda i, h: (i, kb + h)),
                  pl.BlockSpec((rows, GLA_DV), lambda i, h: (i, vb + h)),
                  pl.BlockSpec((rows, GLA_DV), lambda i, h: (i, rb + h)),
                  pl.BlockSpec((rows, GLA_DK), lambda i, h: (i, h)),
                  pl.BlockSpec((nseq, 1, GLA_DK, GLA_DV), lambda i, h: (i, h, 0, 0)),
                  pl.BlockSpec((rows, rows), lambda i, h: (0, 0)),
                  pl.BlockSpec((rows, rows), lambda i, h: (0, 0)),
                  pl.BlockSpec((1, GLA_DV), lambda i, h: (0, 0))],
        out_specs=[pl.BlockSpec((rows, GLA_DV), lambda i, h: (i, h)),
                   pl.BlockSpec((nseq, 1, GLA_DK, GLA_DV), lambda i, h: (i, h, 0, 0))],
        out_shape=[jax.ShapeDtypeStruct((t, GLA_VAL), BF16),
                   jax.ShapeDtypeStruct(state.shape, F32)],
        scratch_shapes=[pltpu.VMEM((rows, GLA_DK), F32)],
        compiler_params=_params("arbitrary", "arbitrary"),
        name="gla_sample",
    )(proj, proj, proj, proj, glog, state, tril, lvl, g_out)


def _rope_tables(pos):
    half = SWA_HEAD_DIM // 2
    inv_freq = ROPE_THETA ** (-(jnp.arange(half, dtype=F32) * 2.0) / SWA_HEAD_DIM)
    ang = pos.astype(F32)[:, None] * inv_freq[None, :]
    cos, sin = jnp.cos(ang), jnp.sin(ang)
    cos_t = jnp.concatenate([cos, cos, cos, cos], axis=-1)
    sin_t = jnp.concatenate([-sin, sin, -sin, sin], axis=-1)
    return cos_t, sin_t


def _head_mean_matrix():
    i = np.arange(MXU_DIM)[:, None] // SWA_HEAD_DIM
    j = np.arange(MXU_DIM)[None, :] // SWA_HEAD_DIM
    return (i == j).astype(np.float32) / SWA_HEAD_DIM


def _qk_norm_rope(x, gain, cos_t, sin_t, mean_mat):
    return _norm_rope_tiles(x, _head_mean_squares(x, mean_mat), gain, cos_t, sin_t)


def _head_mean_squares(x, mean_mat):
    sq = (x * x).astype(BF16)
    return [_dot(sq[:, c * MXU_DIM:(c + 1) * MXU_DIM], mean_mat) for c in range(x.shape[1] // MXU_DIM)]


def _norm_rope_tiles(x, mean_sq, gain, cos_t, sin_t):
    rows = x.shape[0]
    lane = lax.broadcasted_iota(jnp.int32, (rows, LANES), 1)
    first_half = (lane % SWA_HEAD_DIM) < (SWA_HEAD_DIM // 2)
    shift = SWA_HEAD_DIM // 2
    tiles = []
    for c, ms in enumerate(mean_sq):
        for t in range(MXU_DIM // LANES):
            lo = c * MXU_DIM + t * LANES
            y = (x[:, lo:lo + LANES] * lax.rsqrt(ms[:, t * LANES:(t + 1) * LANES] + EPS)) * gain
            rot = jnp.where(first_half, pltpu.roll(y, LANES - shift, 1), pltpu.roll(y, shift, 1))
            tiles.append(y * cos_t + rot * sin_t)
    return tiles


def _attend_all_heads(q_tiles, k_all, v_all, valid, sinks_ref, store):
    pairs = SWA_GROUP // 2
    rows = valid.shape[0]
    low = lax.broadcasted_iota(jnp.int32, (k_all.shape[0], LANES), 1) < SWA_HEAD_DIM
    chains = [(g, parity) for g in range(SWA_KV_HEADS) for parity in range(2)]
    q_group = {}

    def scores(g, parity):
        lo = (g // 2) * LANES
        have_low, want_low = g % 2 == 0, parity == 0

        def place(x):
            src = x if have_low == want_low else pltpu.roll(x, SWA_HEAD_DIM, 1)
            return jnp.where(low if want_low else jnp.logical_not(low), src, 0.0).astype(BF16)
        if g not in q_group:
            q_group[g] = jnp.concatenate(q_tiles[g * pairs:(g + 1) * pairs], axis=0).astype(BF16)
        return _dot_nt(q_group[g], place(k_all[:, lo:lo + LANES])), place(v_all[:, lo:lo + LANES])

    def softmax(g, parity, s_all):
        p_list, inv_list = [], []
        for pair in range(pairs):
            sink = sinks_ref[g * SWA_GROUP + 2 * pair + parity] * LOG2_E
            s = jnp.where(valid, s_all[pair * rows:(pair + 1) * rows, :], NEG_INF)
            m = jnp.maximum(jnp.max(s, axis=-1, keepdims=True), sink)
            p = jnp.exp2(s - m)
            inv_list.append(1.0 / (jnp.sum(p, axis=-1, keepdims=True) + jnp.exp2(sink - m)))
            p_list.append(p.astype(BF16))
        return jnp.concatenate(p_list, axis=0), inv_list

    partial = None
    for g, parity in chains:
        s_all, vv = scores(g, parity)
        p_all, inv_list = softmax(g, parity, s_all)
        pv = _dot(p_all, vv)
        outs = [pv[pair * rows:(pair + 1) * rows, :] * inv_list[pair] for pair in range(pairs)]
        if parity == 0:
            partial = outs
        else:
            for pair in range(pairs):
                store(g * pairs + pair, (partial[pair] + outs[pair]).astype(BF16))


def _swa_prompt_body(sinks_ref, q_ref, k_ref, v_ref, cb_ref, sb_ref, cr_ref, sr_ref, crs_ref, srs_ref,
                     gq_ref, gk_ref, mm_ref, pa_ref, ga_ref, gb_ref, wpb_ref,
                     m_ref, knew_ref, vnew_ref, kprev_scr, vprev_scr, ob_scr):
    step = pl.program_id(0)
    n = jnp.minimum(step, pl.num_programs(0) - 2)
    blk = WINDOW
    nsub = q_ref.shape[0] // blk
    cur = lax.rem(step, 2)
    prev = 1 - cur

    @pl.when(step == 0)
    def _():
        kprev_scr[...] = jnp.zeros_like(kprev_scr)
        vprev_scr[...] = jnp.zeros_like(vprev_scr)
        ob_scr[...] = jnp.zeros_like(ob_scr)

    mean_mat = mm_ref[...]
    qi = lax.broadcasted_iota(jnp.int32, (blk, 2 * blk), 0)
    kj = lax.broadcasted_iota(jnp.int32, (blk, 2 * blk), 1)
    dist = blk + qi - kj
    in_window = (dist >= 0) & (dist <= WINDOW)
    k_prev, v_prev = kprev_scr[...], vprev_scr[...]
    for sub in range(nsub):
        rsl = slice(sub * blk, (sub + 1) * blk)
        b = n * nsub + sub
        cb, sb = cb_ref[pl.ds(b, 1), :], sb_ref[pl.ds(b, 1), :]
        cos_t = cb * cr_ref[...] - sb * sr_ref[...]
        sin_t = sb * crs_ref[...] + cb * srs_ref[...]
        xq, xk = q_ref[rsl, :].astype(F32), k_ref[rsl, :].astype(F32)
        ms_q, ms_k = _head_mean_squares(xq, mean_mat), _head_mean_squares(xk, mean_mat)
        if sub == 0:
            m_ref[...] = (_sigmoid(ga_ref[...].astype(F32)) * pa_ref[...]
                          + _sigmoid(gb_ref[...].astype(F32)) * _dot(ob_scr[prev], wpb_ref[...])).astype(BF16)
        q_tiles = [t * SWA_SCORE_SCALE for t in _norm_rope_tiles(xq, ms_q, gq_ref[...], cos_t, sin_t)]
        kn = jnp.concatenate(_norm_rope_tiles(xk, ms_k, gk_ref[...], cos_t, sin_t), axis=1)
        vn = v_ref[rsl, :].astype(F32)
        k_all = jnp.concatenate([k_prev, kn], axis=0)
        v_all = jnp.concatenate([v_prev, vn], axis=0)
        valid = in_window & ((b - 1) * blk + kj >= 0)

        def store(tile, value, rsl=rsl):
            ob_scr[cur, rsl, tile * LANES:(tile + 1) * LANES] = value
        _attend_all_heads(q_tiles, k_all, v_all, valid, sinks_ref, store)
        k_prev, v_prev = kn, vn

    knew_ref[...] = k_prev
    vnew_ref[...] = v_prev
    kprev_scr[...] = k_prev
    vprev_scr[...] = v_prev


def _rope_block_tables(nblocks, blk):
    half = SWA_HEAD_DIM // 2
    inv_freq = ROPE_THETA ** (-(jnp.arange(half, dtype=F32) * 2.0) / SWA_HEAD_DIM)
    lane_freq = jnp.tile(inv_freq, LANES // half)[None, :]
    sign = jnp.tile(jnp.concatenate([-jnp.ones((half,), F32), jnp.ones((half,), F32)]), LANES // SWA_HEAD_DIM)[None, :]
    ang_b = (jnp.arange(nblocks, dtype=jnp.int32) * blk).astype(F32)[:, None] * lane_freq
    ang_r = jnp.arange(blk, dtype=jnp.int32).astype(F32)[:, None] * lane_freq
    cr, sr = jnp.cos(ang_r), jnp.sin(ang_r)
    return jnp.cos(ang_b), jnp.sin(ang_b), cr, sr, sign * cr, sign * sr


def _swa_prompt(proj, pa, sinks, g_q, g_k, w_pb):
    t = proj.shape[0]
    d = w_pb.shape[1]
    blk = WINDOW
    rows = SWA_PROMPT_BLOCKS_PER_STEP * blk
    nblocks, nsteps = t // blk, t // rows
    mean_mat = jnp.asarray(_head_mean_matrix(), BF16)
    tables = _rope_block_tables(nblocks, blk)
    qb, kb, vb = COL_SQ // SWA_Q, COL_SK // SWA_KV, COL_SV // SWA_KV
    whole = lambda n, s: (0, 0)
    cur = lambda n: jnp.minimum(n, nsteps - 1)
    prev = lambda n: jnp.maximum(n - 1, 0)
    gate_rows = lambda n: pl.multiple_of(prev(n) * rows, rows)
    grid_spec = pltpu.PrefetchScalarGridSpec(
        num_scalar_prefetch=1,
        grid=(nsteps + 1,),
        in_specs=[pl.BlockSpec((rows, SWA_Q), lambda n, s: (cur(n), qb)),
                  pl.BlockSpec((rows, SWA_KV), lambda n, s: (cur(n), kb)),
                  pl.BlockSpec((rows, SWA_KV), lambda n, s: (cur(n), vb)),
                  pl.BlockSpec((nblocks, LANES), whole),
                  pl.BlockSpec((nblocks, LANES), whole),
                  pl.BlockSpec((blk, LANES), whole),
                  pl.BlockSpec((blk, LANES), whole),
                  pl.BlockSpec((blk, LANES), whole),
                  pl.BlockSpec((blk, LANES), whole),
                  pl.BlockSpec((1, LANES), whole),
                  pl.BlockSpec((1, LANES), whole),
                  pl.BlockSpec((MXU_DIM, MXU_DIM), whole),
                  pl.BlockSpec((rows, d), lambda n, s: (prev(n), 0)),
                  pl.BlockSpec((pl.Element(rows), pl.Element(d)), lambda n, s: (gate_rows(n), COL_BG)),
                  pl.BlockSpec((pl.Element(rows), pl.Element(d)), lambda n, s: (gate_rows(n), COL_BG + d)),
                  pl.BlockSpec(w_pb.shape, whole, pipeline_mode=pl.Buffered(1))],
        out_specs=[pl.BlockSpec((rows, d), lambda n, s: (prev(n), 0)),
                   pl.BlockSpec((blk, SWA_KV), whole),
                   pl.BlockSpec((blk, SWA_KV), whole)],
        scratch_shapes=[pltpu.VMEM((blk, SWA_KV), F32), pltpu.VMEM((blk, SWA_KV), F32),
                        pltpu.VMEM((2, rows, SWA_Q), BF16)],
    )
    return pl.pallas_call(
        _swa_prompt_body,
        grid_spec=grid_spec,
        out_shape=[jax.ShapeDtypeStruct((t, d), BF16),
                   jax.ShapeDtypeStruct((blk, SWA_KV), F32),
                   jax.ShapeDtypeStruct((blk, SWA_KV), F32)],
        compiler_params=_params("arbitrary"),
        name="swa_prompt",
    )(sinks, proj, proj, proj, *tables, g_q, g_k, mean_mat, pa, proj, proj, w_pb)


def _swa_sample_body(sinks_ref, q_ref, k_ref, v_ref, bk_ref, bv_ref, cos_ref, sin_ref, gq_ref, gk_ref, mm_ref,
                     ob_ref, kout_ref, vout_ref):
    nseq = bk_ref.shape[0]
    rows = nseq * DEC_SEQ
    cos_t = jnp.concatenate([cos_ref[...]] * nseq, axis=0)
    sin_t = jnp.concatenate([sin_ref[...]] * nseq, axis=0)
    mean_mat = mm_ref[...]
    q_tiles = [t * SWA_SCORE_SCALE
               for t in _qk_norm_rope(q_ref[...].astype(F32), gq_ref[...], cos_t, sin_t, mean_mat)]
    kn = jnp.concatenate(_qk_norm_rope(k_ref[...].astype(F32), gk_ref[...], cos_t, sin_t, mean_mat), axis=1)
    vn = v_ref[...].astype(F32)

    lane_w = lax.broadcasted_iota(jnp.int32, (SWA_KV, WINDOW), 1)
    pad_rows = jnp.zeros((WINDOW - DEC_SEQ, SWA_KV), F32)
    for s in range(nseq):
        rsl = slice(s * DEC_SEQ, (s + 1) * DEC_SEQ)
        for buf_ref, new, out_ref in ((bk_ref, kn, kout_ref), (bv_ref, vn, vout_ref)):
            new_t = jnp.concatenate([pad_rows, new[rsl, :]], axis=0).T
            out_ref[s] = jnp.where(lane_w < WINDOW - DEC_SEQ, pltpu.roll(buf_ref[s], WINDOW - DEC_SEQ, 1), new_t)

    tok = lax.broadcasted_iota(jnp.int32, (rows, WINDOW), 0) % DEC_SEQ
    valid_c = lax.broadcasted_iota(jnp.int32, (rows, WINDOW), 1) >= tok
    r_n = lax.broadcasted_iota(jnp.int32, (rows, LANES), 0)
    c_n = lax.broadcasted_iota(jnp.int32, (rows, LANES), 1)
    valid_n = (r_n // DEC_SEQ == c_n // DEC_SEQ) & (c_n % DEC_SEQ <= r_n % DEC_SEQ)
    pad_new = jnp.zeros((LANES - rows, LANES), F32)

    pairs = SWA_GROUP // 2
    zeros_dk = jnp.zeros((SWA_HEAD_DIM, nseq * WINDOW), F32)
    zero_blk = jnp.zeros((DEC_SEQ, WINDOW), F32)
    low = lax.broadcasted_iota(jnp.int32, (rows, LANES), 1) < SWA_HEAD_DIM
    for g in range(SWA_KV_HEADS):
        dsl = slice(g * SWA_HEAD_DIM, (g + 1) * SWA_HEAD_DIM)
        kt = jnp.concatenate([bk_ref[s, dsl, :] for s in range(nseq)], axis=1)
        vt = jnp.concatenate([bv_ref[s, dsl, :] for s in range(nseq)], axis=1)
        lo = (g // 2) * LANES
        kn_tile, vn_tile = kn[:, lo:lo + LANES], vn[:, lo:lo + LANES]
        kv_low = g % 2 == 0
        q_g = jnp.concatenate(q_tiles[g * pairs:(g + 1) * pairs], axis=0).astype(BF16)
        out = None
        for parity in range(2):
            want_low = parity == 0
            stack = (lambda a: jnp.concatenate([a, zeros_dk], axis=0)) if want_low else \
                    (lambda a: jnp.concatenate([zeros_dk, a], axis=0))
            keep = low if want_low else jnp.logical_not(low)
            align = (lambda a: a) if kv_low == want_low else (lambda a: pltpu.roll(a, SWA_HEAD_DIM, 1))
            place = lambda a: jnp.concatenate([jnp.where(keep, align(a), 0.0), pad_new], axis=0).astype(BF16)
            kn_p, vn_p = place(kn_tile), place(vn_tile)
            s_c = _dot(q_g, stack(kt).astype(BF16))
            s_n = _dot_nt(q_g, kn_p)
            pc_rows, pn_rows, inv = [], [], []
            for pair in range(pairs):
                sink = sinks_ref[g * SWA_GROUP + 2 * pair + parity] * LOG2_E
                psl = slice(pair * rows, (pair + 1) * rows)
                sc = jnp.concatenate(
                    [s_c[pair * rows + s * DEC_SEQ: pair * rows + (s + 1) * DEC_SEQ, s * WINDOW:(s + 1) * WINDOW]
                     for s in range(nseq)], axis=0)
                sc = jnp.where(valid_c, sc, NEG_INF)
                sn = jnp.where(valid_n, s_n[psl, :], NEG_INF)
                m = jnp.maximum(jnp.maximum(jnp.max(sc, axis=-1, keepdims=True),
                                            jnp.max(sn, axis=-1, keepdims=True)), sink)
                pc = jnp.exp2(sc - m)
                pn = jnp.exp2(sn - m)
                inv.append(1.0 / (jnp.sum(pc, axis=-1, keepdims=True) + jnp.sum(pn, axis=-1, keepdims=True)
                                  + jnp.exp2(sink - m)))
                pn_rows.append(pn)
                for s in range(nseq):
                    blocks = [pc[s * DEC_SEQ:(s + 1) * DEC_SEQ, :] if s2 == s else zero_blk for s2 in range(nseq)]
                    pc_rows.append(jnp.concatenate(blocks, axis=1))
            pv = (_dot_nt(jnp.concatenate(pc_rows, axis=0).astype(BF16), stack(vt).astype(BF16))
                  + _dot(jnp.concatenate(pn_rows, axis=0).astype(BF16), vn_p))
            pv = jnp.concatenate([pv[pair * rows:(pair + 1) * rows, :] * inv[pair] for pair in range(pairs)], axis=0)
            out = pv if out is None else out + pv
        for pair in range(pairs):
            tile = g * pairs + pair
            ob_ref[:, tile * LANES:(tile + 1) * LANES] = out[pair * rows:(pair + 1) * rows, :].astype(BF16)


def _swa_sample(proj, buf_k, buf_v, sinks, cos_t, sin_t, g_q, g_k):
    t = proj.shape[0]
    nseq = SWA_SAMPLE_BATCH
    rows = nseq * DEC_SEQ
    mean_mat = jnp.asarray(_head_mean_matrix(), BF16)
    qb, kb, vb = COL_SQ // SWA_Q, COL_SK // SWA_KV, COL_SV // SWA_KV
    grid_spec = pltpu.PrefetchScalarGridSpec(
        num_scalar_prefetch=1,
        grid=(t // rows,),
        in_specs=[pl.BlockSpec((rows, SWA_Q), lambda n, s: (n, qb)),
                  pl.BlockSpec((rows, SWA_KV), lambda n, s: (n, kb)),
                  pl.BlockSpec((rows, SWA_KV), lambda n, s: (n, vb)),
                  pl.BlockSpec((nseq, SWA_KV, WINDOW), lambda n, s: (n, 0, 0)),
                  pl.BlockSpec((nseq, SWA_KV, WINDOW), lambda n, s: (n, 0, 0)),
                  pl.BlockSpec((DEC_SEQ, LANES), lambda n, s: (0, 0)),
                  pl.BlockSpec((DEC_SEQ, LANES), lambda n, s: (0, 0)),
                  pl.BlockSpec((1, LANES), lambda n, s: (0, 0)),
                  pl.BlockSpec((1, LANES), lambda n, s: (0, 0)),
                  pl.BlockSpec((MXU_DIM, MXU_DIM), lambda n, s: (0, 0))],
        out_specs=[pl.BlockSpec((rows, SWA_Q), lambda n, s: (n, 0)),
                   pl.BlockSpec((nseq, SWA_KV, WINDOW), lambda n, s: (n, 0, 0)),
                   pl.BlockSpec((nseq, SWA_KV, WINDOW), lambda n, s: (n, 0, 0))],
    )
    return pl.pallas_call(
        _swa_sample_body,
        grid_spec=grid_spec,
        out_shape=[jax.ShapeDtypeStruct((t, SWA_Q), BF16),
                   jax.ShapeDtypeStruct(buf_k.shape, F32),
                   jax.ShapeDtypeStruct(buf_v.shape, F32)],
        compiler_params=_params("arbitrary"),
        name="swa_sample",
    )(sinks, proj, proj, proj, buf_k, buf_v, cos_t, sin_t, g_q, g_k, mean_mat)


def _merge_body(oa_ref, ob_ref, wpa_ref, wpb_ref, ga_ref, gb_ref, m_ref):
    ga = _sigmoid(ga_ref[...].astype(F32))
    gb = _sigmoid(gb_ref[...].astype(F32))
    m_ref[...] = (ga * _dot(oa_ref[...], wpa_ref[...]) + gb * _dot(ob_ref[...], wpb_ref[...])).astype(BF16)


def _merge(oa, ob, proj, w_pa, w_pb, tm=1024, tn=512):
    t, d = oa.shape
    ga_blk, gb_blk = COL_BG // tn, (COL_BG + D_MODEL) // tn
    return pl.pallas_call(
        _merge_body,
        grid=(t // tm, d // tn),
        in_specs=[pl.BlockSpec((tm, d), lambda i, j: (i, 0)),
                  pl.BlockSpec((tm, d), lambda i, j: (i, 0)),
                  pl.BlockSpec((d, tn), lambda i, j: (0, j)),
                  pl.BlockSpec((d, tn), lambda i, j: (0, j)),
                  pl.BlockSpec((tm, tn), lambda i, j: (i, ga_blk + j)),
                  pl.BlockSpec((tm, tn), lambda i, j: (i, gb_blk + j))],
        out_specs=pl.BlockSpec((tm, tn), lambda i, j: (i, j)),
        out_shape=jax.ShapeDtypeStruct((t, d), BF16),
        compiler_params=_params("arbitrary", "arbitrary"),
        name="merge",
    )(oa, ob, w_pa, w_pb, proj, proj)


OPROJ_ROW_CHUNK = 256


def _oproj_body(m_ref, w_ref, x_ref, gate_ref, gn_ref, sc_ref, sh_ref, x1_ref, h2_ref):
    for row0 in range(0, x_ref.shape[0], OPROJ_ROW_CHUNK):
        sl = slice(row0, row0 + OPROJ_ROW_CHUNK)
        y = _dot(m_ref[sl, :], w_ref[...])
        x1 = x_ref[sl, :] + _mod_rows(gate_ref, row0, OPROJ_ROW_CHUNK) * y
        x1_ref[sl, :] = x1
        ms = jnp.mean(x1 * x1, axis=-1, keepdims=True)
        h = (x1 * lax.rsqrt(ms + EPS)) * gn_ref[...]
        h2_ref[sl, :] = (h * (1.0 + _mod_rows(sc_ref, row0, OPROJ_ROW_CHUNK))
                         + _mod_rows(sh_ref, row0, OPROJ_ROW_CHUNK)).astype(BF16)


def _oproj(merged, w_o, x, gate, gn, scale, shift, tm=512):
    t, d = x.shape
    rows = lambda i: (i, 0)
    whole = lambda i: (0, 0)
    mod = lambda m: (pl.BlockSpec((1, d), whole) if m.shape[0] == 1
                     else pl.BlockSpec((tm // DEC_SEQ, d), rows))
    return pl.pallas_call(
        _oproj_body,
        grid=(t // tm,),
        in_specs=[pl.BlockSpec((tm, d), rows),
                  pl.BlockSpec(w_o.shape, whole, pipeline_mode=pl.Buffered(1)),
                  pl.BlockSpec((tm, d), rows),
                  mod(gate),
                  pl.BlockSpec((1, d), whole),
                  mod(scale),
                  mod(shift)],
        out_specs=[pl.BlockSpec((tm, d), rows), pl.BlockSpec((tm, d), rows)],
        out_shape=[jax.ShapeDtypeStruct(x.shape, F32), jax.ShapeDtypeStruct(x.shape, BF16)],
        compiler_params=_params("arbitrary"),
        name="oproj",
    )(merged, w_o, x, gate, gn, scale, shift)


def _ffn_body(h_ref, x_ref, gate_ref, wg_ref, wu_ref, wd_ref, o_ref):
    f = pl.program_id(1)

    @pl.when(f == 0)
    def _():
        def rows(row0):
            o_ref[pl.ds(row0, ROW_CHUNK), :] = jnp.zeros((ROW_CHUNK, o_ref.shape[1]), F32)
        _row_loop(x_ref.shape[0], rows)

    h = h_ref[...]
    a = _silu(_dot(h, wg_ref[...])) * _dot(h, wu_ref[...])
    o_ref[...] += _dot(a.astype(BF16), wd_ref[...])

    @pl.when(f == pl.num_programs(1) - 1)
    def _():
        def rows(row0):
            sl = pl.ds(row0, ROW_CHUNK)
            o_ref[sl, :] = x_ref[sl, :] + _mod_rows(gate_ref, row0, ROW_CHUNK) * o_ref[sl, :]
        _row_loop(x_ref.shape[0], rows)


def _ffn(h, x, gate, w_gate, w_up, w_down, tm=1024, tf=512):
    t, d = x.shape
    dff = w_gate.shape[1]
    zero = lambda j: 0
    return pl.pallas_call(
        _ffn_body,
        grid=(t // tm, dff // tf),
        in_specs=[pl.BlockSpec((tm, d), lambda i, f: (i, 0)),
                  pl.BlockSpec((tm, d), lambda i, f: (i, 0)),
                  _mod_spec(gate, tm, d, zero),
                  pl.BlockSpec((d, tf), lambda i, f: (0, f)),
                  pl.BlockSpec((d, tf), lambda i, f: (0, f)),
                  pl.BlockSpec((tf, d), lambda i, f: (f, 0))],
        out_specs=pl.BlockSpec((tm, d), lambda i, f: (i, 0)),
        out_shape=jax.ShapeDtypeStruct(x.shape, F32),
        compiler_params=_params("arbitrary", "arbitrary"),
        name="ffn",
    )(h, x, gate, w_gate, w_up, w_down)


def _mods(ada):
    return tuple(ada[:, i * D_MODEL:(i + 1) * D_MODEL] for i in range(6))


def _layer_after_inproj(x, proj, glog, mods, pos, state, buf_k, buf_v, wts):
    g_q, g_k, sinks, g_out, w_pa, w_pb, w_o, gn_ffn, w_fg, w_fu, w_fd = wts
    bsz, tlen, d = x.shape
    x2 = x.reshape(bsz * tlen, d)
    _, _, gate_m, shift_f, scale_f, gate_f = mods
    if state is None:
        pa, s_new = _gla_prompt(proj, glog, g_out, w_pa)
        s_new = s_new[None]
        merged, k_new, v_new = _swa_prompt(proj, pa, sinks, g_q, g_k, w_pb)
        k_new = k_new.reshape(bsz, WINDOW, SWA_KV_HEADS, SWA_HEAD_DIM)
        v_new = v_new.reshape(bsz, WINDOW, SWA_KV_HEADS, SWA_HEAD_DIM)
    else:
        to_dev = lambda c: jnp.transpose(c, (0, 2, 3, 1)).reshape(bsz, SWA_KV, WINDOW)
        from_dev = lambda c: jnp.transpose(c.reshape(bsz, SWA_KV_HEADS, SWA_HEAD_DIM, WINDOW), (0, 3, 1, 2))
        cos_t, sin_t = _rope_tables(pos)
        oa, s_new = _gla_sample(proj, glog, state, g_out)
        ob, k_new, v_new = _swa_sample(proj, to_dev(buf_k), to_dev(buf_v), sinks, cos_t, sin_t, g_q, g_k)
        k_new, v_new = from_dev(k_new), from_dev(v_new)
        merged = _merge(oa, ob, proj, w_pa, w_pb)
    x1, h2 = _oproj(merged, w_o, x2, gate_m, gn_ffn, scale_f, shift_f)
    y = _ffn(h2, x1, gate_f, w_fg, w_fu, w_fd)
    return y.reshape(bsz, tlen, d), s_new, k_new, v_new


def kernel(x_prompt, x_sample, c_prompt, c_sample, state_gla, cache_swa_k, cache_swa_v, w_ada, b_ada, g_norm_mix, w_in, w_gk2, b_gk2, g_qnorm, g_knorm, sinks, g_gla_out, w_pa, w_pb, w_o, g_norm_ffn, w_ffn_gate, w_ffn_up, w_ffn_down):
    assert w_in.shape[0] == 1, "single trunk layer"
    pos_p = jnp.arange(SEQ, dtype=jnp.int32)
    pos_s = PAST_LEN + jnp.arange(DEC_SEQ, dtype=jnp.int32)
    l = 0
    n_c = 1 + DEC_BATCH
    pad = (-n_c) % (2 * SUBLANES)
    c_all = jnp.pad(jnp.concatenate([c_prompt, c_sample], axis=0), ((0, pad), (0, 0)))
    ada, wt = _ada(c_all, w_ada[l], b_ada[l].reshape(1, -1), jnp.transpose(w_in[l]))
    mods_p, mods_s = _mods(ada[0:1]), _mods(ada[1:n_c])

    wt_lr = jnp.pad(wt[W_IN_GLR:W_IN_GLR + GLA_RANK], ((0, LANES - GLA_RANK), (0, 0)))
    w_gk = jnp.pad(w_gk2[l], ((0, LANES - GLA_RANK), (0, 0))).astype(BF16)
    in_wts = (wt, wt_lr, w_gk, b_gk2[l].reshape(1, GLA_KEY))
    gn_mix = g_norm_mix[l].reshape(1, D_MODEL)

    later_weights = (w_pa[l], w_pb[l], w_o[l], w_ffn_gate[l], w_ffn_up[l], w_ffn_down[l])
    proj_p, glog_p, *later_bf16 = _inproj(x_prompt.reshape(-1, D_MODEL), gn_mix, mods_p[1], mods_p[0], *in_wts,
                                          cast_weights=later_weights)
    proj_s, glog_s = _inproj(x_sample.reshape(-1, D_MODEL), gn_mix, mods_s[1], mods_s[0], *in_wts)
    w_pa_b, w_pb_b, w_o_b, w_fg_b, w_fu_b, w_fd_b = later_bf16
    wts = (jnp.tile(g_qnorm[l], 2).reshape(1, LANES), jnp.tile(g_knorm[l], 2).reshape(1, LANES),
           sinks[l], g_gla_out[l].reshape(1, GLA_DV), w_pa_b, w_pb_b, w_o_b,
           g_norm_ffn[l].reshape(1, D_MODEL), w_fg_b, w_fu_b, w_fd_b)

    yp, sp, kp, vp = _layer_after_inproj(x_prompt, proj_p, glog_p, mods_p, pos_p, None, None, None, wts)
    ys, ss, ks, vs = _layer_after_inproj(x_sample, proj_s, glog_s, mods_s, pos_s, state_gla[l], cache_swa_k[l],
                                         cache_swa_v[l], wts)
    return (yp, ys, sp[None], kp[None], vp[None], ss[None], ks[None], vs[None])
```

```python
import functools

import numpy as np
import jax
import jax.numpy as jnp
from jax import lax
from jax.experimental import pallas as pl
from jax.experimental.pallas import tpu as pltpu

F32 = jnp.float32
BF16 = jnp.bfloat16

D_MODEL = 2048
SEQ = 16384
DEC_BATCH = 128
DEC_SEQ = 8
PAST_LEN = 16384
GLA_HEADS = 4
GLA_DK = 256
GLA_DV = 512
GLA_KEY = GLA_HEADS * GLA_DK
GLA_VAL = GLA_HEADS * GLA_DV
GLA_RANK = 16
GLA_TAU = 16.0
SWA_HEAD_DIM = 64
SWA_HEADS = 32
SWA_KV_HEADS = 4
SWA_GROUP = SWA_HEADS // SWA_KV_HEADS
SWA_Q = SWA_HEADS * SWA_HEAD_DIM
SWA_KV = SWA_KV_HEADS * SWA_HEAD_DIM
WINDOW = 128
ROPE_THETA = 10000.0
D_FF = 5632
EPS = 1e-6
NEG_INF = -1e30
LOG2_E = 1.4426950408889634
SWA_SCORE_SCALE = (SWA_HEAD_DIM ** -0.5) * LOG2_E

SUBLANES = 8
LANES = 128
MXU_DIM = 256
VMEM_LIMIT_BYTES = 58 * 1024 * 1024

COL_GQ = 0
COL_GK = COL_GQ + GLA_KEY
COL_GV = COL_GK + GLA_KEY
COL_GR = COL_GV + GLA_VAL
COL_SQ = COL_GR + GLA_VAL
COL_SK = COL_SQ + SWA_Q
COL_SV = COL_SK + SWA_KV
COL_BG = COL_SV + SWA_KV
PROJ_COLS = COL_BG + 2 * D_MODEL
W_IN_GLR = COL_GR + GLA_VAL

GLA_CHUNK = 256
GLA_ONE_SIDED_MAX_DECAY = 80.0
GLA_SAMPLE_BATCH = 16
SWA_SAMPLE_BATCH = 4
SWA_PROMPT_BLOCKS_PER_STEP = 1
SWA_PAIRS_PER_CHAIN = 4
ROW_CHUNK = 512


def _dot(a, b):
    return jnp.dot(a, b, preferred_element_type=F32)


def _dot_nt(a, b):
    return lax.dot_general(a, b, (((1,), (1,)), ((), ())), preferred_element_type=F32)


def _dot_tn(a, b):
    return lax.dot_general(a, b, (((0,), (0,)), ((), ())), preferred_element_type=F32)


def _sigmoid(x):
    return 1.0 / (1.0 + jnp.exp(-x))


def _silu(x):
    return x * _sigmoid(x)


def _split3(x):
    hi = x.astype(BF16)
    r1 = x - hi.astype(F32)
    mid = r1.astype(BF16)
    lo = (r1 - mid.astype(F32)).astype(BF16)
    return hi, mid, lo


def _params(*sem):
    return pltpu.CompilerParams(dimension_semantics=sem, vmem_limit_bytes=VMEM_LIMIT_BYTES)


def _ada_body(c_ref, w_ref, b_ref, wt_ref, o_ref, wtb_ref):
    a = _silu(c_ref[...]).astype(BF16)
    o_ref[...] = _dot(a, w_ref[...].astype(BF16)) + b_ref[...]
    wtb_ref[...] = wt_ref[...].astype(BF16)


def _ada(c_all, w_ada, b_ada, wt, tn=512):
    m, d = c_all.shape
    n = w_ada.shape[1]
    steps = n // tn
    rows = wt.shape[0]
    align = 2 * SUBLANES
    slab = -(-rows // (steps * align)) * align
    assert (steps - 1) * slab < rows <= steps * slab
    slab_spec = pl.BlockSpec((slab, wt.shape[1]), lambda j: (j, 0))
    return pl.pallas_call(
        _ada_body,
        grid=(steps,),
        in_specs=[pl.BlockSpec((m, d), lambda j: (0, 0)),
                  pl.BlockSpec((d, tn), lambda j: (0, j)),
                  pl.BlockSpec((1, tn), lambda j: (0, j)),
                  slab_spec],
        out_specs=[pl.BlockSpec((m, tn), lambda j: (0, j)), slab_spec],
        out_shape=[jax.ShapeDtypeStruct((m, n), F32), jax.ShapeDtypeStruct(wt.shape, BF16)],
        compiler_params=_params("arbitrary"),
        name="ada",
    )(c_all, w_ada, b_ada, wt)


def _mod_spec(mod, tm, width, col):
    if mod.shape[0] == 1:
        return pl.BlockSpec((1, width), lambda i, j: (0, col(j)))
    return pl.BlockSpec((tm // DEC_SEQ, width), lambda i, j: (i, col(j)))


def _mod_rows(ref, row0, nrows):
    if ref.shape[0] == 1:
        return ref[...]
    seq0 = row0 // DEC_SEQ
    return jnp.concatenate(
        [jnp.broadcast_to(ref[pl.ds(seq0 + s, 1), :], (DEC_SEQ, ref.shape[1])) for s in range(nrows // DEC_SEQ)],
        axis=0)


def _modulated_norm_rows(x_ref, gn_ref, sc_ref, sh_ref, row0, nrows):
    x = x_ref[pl.ds(row0, nrows), :]
    ms = jnp.mean(x * x, axis=-1, keepdims=True)
    y = (x * lax.rsqrt(ms + EPS)) * gn_ref[...]
    return y * (1.0 + _mod_rows(sc_ref, row0, nrows)) + _mod_rows(sh_ref, row0, nrows)


def _row_loop(total_rows, body):
    assert total_rows % ROW_CHUNK == 0, (total_rows, ROW_CHUNK)

    def step(r, carry):
        body(pl.multiple_of(r * ROW_CHUNK, ROW_CHUNK))
        return carry
    lax.fori_loop(0, total_rows // ROW_CHUNK, step, 0)


def _inproj_body(x_ref, gn_ref, sc_ref, sh_ref, wt_ref, wlr_ref, wgk_ref, bgk_ref, *rest, n_casts):
    cast_src = rest[:n_casts]
    proj_ref, glog_ref = rest[n_casts:n_casts + 2]
    cast_dst = rest[n_casts + 2:2 * n_casts + 2]
    h_scr = rest[-1]
    for src, dst in zip(cast_src, cast_dst):
        dst[...] = src[...].astype(BF16)

    @pl.when(pl.program_id(1) == 0)
    def _():
        def rows(row0):
            hb = _modulated_norm_rows(x_ref, gn_ref, sc_ref, sh_ref, row0, ROW_CHUNK).astype(BF16)
            h_scr[pl.ds(row0, ROW_CHUNK), :] = hb
            glr = _dot_nt(hb, wlr_ref[...])
            z = _dot(glr.astype(BF16), wgk_ref[...]) + bgk_ref[...]
            log_sig = jnp.minimum(z, 0.0) - jnp.log1p(jnp.exp(-jnp.abs(z)))
            glog_ref[pl.ds(row0, ROW_CHUNK), :] = log_sig * (1.0 / GLA_TAU)
        _row_loop(x_ref.shape[0], rows)

    proj_ref[...] = _dot_nt(h_scr[...], wt_ref[...]).astype(BF16)


def _cast_block_spec(shape, row_tiles, col_steps):
    rows, cols = shape
    br = rows // row_tiles
    assert br * row_tiles == rows and br % (2 * SUBLANES) == 0
    ncb = max(c for c in range(1, col_steps + 1) if cols % c == 0 and (cols // c) % LANES == 0)
    return pl.BlockSpec((br, cols // ncb), lambda i, j: (i, jnp.minimum(j, ncb - 1)))


def _inproj(x, gn, scale, shift, wt, wt_lr, w_gk, b_gk, cast_weights=(), tm=2048, tn=512):
    t, d = x.shape
    tm = min(tm, t)
    n = wt.shape[0] - GLA_RANK
    n_first = W_IN_GLR // tn
    grid = (t // tm, n // tn)
    zero = lambda j: 0
    wt_rows = lambda i, j: (pl.multiple_of(j * tn + jnp.where(j >= n_first, GLA_RANK, 0), GLA_RANK), 0)
    cast_specs = [_cast_block_spec(w.shape, *grid) for w in cast_weights]
    return pl.pallas_call(
        functools.partial(_inproj_body, n_casts=len(cast_weights)),
        grid=grid,
        in_specs=[pl.BlockSpec((tm, d), lambda i, j: (i, 0), pipeline_mode=pl.Buffered(1)),
                  pl.BlockSpec((1, d), lambda i, j: (0, 0)),
                  _mod_spec(scale, tm, d, zero),
                  _mod_spec(shift, tm, d, zero),
                  pl.BlockSpec((pl.Element(tn), pl.Element(d)), wt_rows),
                  pl.BlockSpec(wt_lr.shape, lambda i, j: (0, 0)),
                  pl.BlockSpec(w_gk.shape, lambda i, j: (0, 0)),
                  pl.BlockSpec(b_gk.shape, lambda i, j: (0, 0))] + cast_specs,
        out_specs=[pl.BlockSpec((tm, tn), lambda i, j: (i, j)),
                   pl.BlockSpec((tm, GLA_KEY), lambda i, j: (i, 0))] + cast_specs,
        out_shape=[jax.ShapeDtypeStruct((t, n), BF16),
                   jax.ShapeDtypeStruct((t, GLA_KEY), F32)]
                  + [jax.ShapeDtypeStruct(w.shape, BF16) for w in cast_weights],
        scratch_shapes=[pltpu.VMEM((tm, d), BF16)],
        compiler_params=_params("arbitrary", "arbitrary"),
        name="inproj",
    )(x, gn, scale, shift, wt, wt_lr, w_gk, b_gk, *cast_weights)


def _gla_level_table(c, group):
    n_levels = int(np.log2(group))
    i = np.arange(c)[:, None]
    j = np.arange(c)[None, :]
    x = np.bitwise_xor(i, j)
    lvl = np.floor(np.log2(np.maximum(x, 1))).astype(np.int32)
    lvl = np.where(i == j, n_levels, lvl)
    valid = (i >= j) & (i // group == j // group)
    return np.where(valid, lvl, -1).astype(np.int32), n_levels


def _gla_tril(c, group):
    i = np.arange(c)[:, None]
    j = np.arange(c)[None, :]
    return ((i >= j) & (i // group == j // group)).astype(np.float32)


def _level_reference(b_scr, rows, level):
    s = 2 ** (level + 1)
    width = b_scr.shape[1]
    pieces = []
    if s >= SUBLANES:
        for blk in range(rows // s):
            mid = blk * s + s // 2
            pieces.append(jnp.broadcast_to(b_scr[mid:mid + 1, :], (s, width)))
    else:
        p = lax.broadcasted_iota(jnp.int32, (SUBLANES, width), 0)
        for tile in range(rows // SUBLANES):
            base = tile * SUBLANES
            mids = [base + q * s + s // 2 for q in range(SUBLANES // s)]
            r = jnp.broadcast_to(b_scr[mids[-1]:mids[-1] + 1, :], (SUBLANES, width))
            for q in range(SUBLANES // s - 2, -1, -1):
                row = jnp.broadcast_to(b_scr[mids[q]:mids[q] + 1, :], (SUBLANES, width))
                r = jnp.where(p < (q + 1) * s, row, r)
            pieces.append(r)
    return jnp.concatenate(pieces, axis=0) if len(pieces) > 1 else pieces[0]


def _gla_intra(q, k, k_bf, b, b_scr, lvl, n_levels):
    rows = q.shape[0]
    a = jnp.where(lvl == n_levels, _dot_nt(q.astype(BF16), k_bf), 0.0)
    for level in range(n_levels):
        f = jnp.exp(-jnp.abs(b - _level_reference(b_scr, rows, level)))
        p = _dot_nt((q * f).astype(BF16), (k * f).astype(BF16))
        a = jnp.where(lvl == level, p, a)
    return a


def _gla_cumsum(g, tril_bf):
    hi, mid, lo = _split3(g)
    return _dot(tril_bf, hi) + _dot(tril_bf, mid) + _dot(tril_bf, lo)


def _gla_out_gate(o, gout, gr):
    ms = jnp.mean(o * o, axis=-1, keepdims=True)
    y = (o * lax.rsqrt(ms + EPS)) * gout
    return (y * _silu(gr.astype(F32))).astype(BF16)


def _gla_prompt_body(q_ref, k_ref, v_ref, gr_ref, g_ref, tril_ref, lvl_ref, gout_ref, wpa_ref,
                     pa_ref, st_ref, st_scr, b_scr, a_scr, oa_scr, *, n_levels, heads):
    c = pl.program_id(0)
    n_chunks = pl.num_programs(0) - 1
    rows = q_ref.shape[0]
    cur = lax.rem(c, 2)
    prev = 1 - cur
    half = pa_ref.shape[1] // 2

    @pl.when(c == 0)
    def _():
        st_scr[...] = jnp.zeros_like(st_scr)
        oa_scr[...] = jnp.zeros_like(oa_scr)

    tril, lvl = tril_ref[...], lvl_ref[...]
    per_head = []
    mild = None
    for h in range(heads):
        ksl = slice(h * GLA_DK, (h + 1) * GLA_DK)
        b = _gla_cumsum(g_ref[:, ksl], tril)
        b_scr[h] = b
        b_end = b_scr[h, rows - 1:rows, :]
        q = q_ref[:, ksl].astype(F32) * (GLA_DK ** -0.5)
        k_bf = k_ref[:, ksl]
        k = k_bf.astype(F32)
        st = st_scr[h]
        qd = (q * jnp.exp(b)).astype(BF16)
        o_inter = _dot_nt(qd, st.astype(BF16))
        head_mild = jnp.min(b_end) > -GLA_ONE_SIDED_MAX_DECAY
        mild = head_mild if mild is None else jnp.logical_and(mild, head_mild)
        per_head.append((b, b_end, q, k, k_bf, st, qd, o_inter))

    pa_ref[:, :half] = _dot(oa_scr[prev], wpa_ref[:, :half])

    @pl.when(mild)
    def _():
        for h, (b, _, _, k, _, _, qd, _) in enumerate(per_head):
            a_scr[h] = jnp.where(lvl >= 0, _dot_nt(qd, (k * jnp.exp(-b)).astype(BF16)), 0.0)

    @pl.when(jnp.logical_not(mild))
    def _():
        for h, (b, _, q, k, k_bf, _, _, _) in enumerate(per_head):
            a_scr[h] = _gla_intra(q, k, k_bf, b, b_scr.at[h], lvl, n_levels)

    outs = []
    for h, (b, b_end, _, k, _, st, _, o_inter) in enumerate(per_head):
        v = v_ref[:, h * GLA_DV:(h + 1) * GLA_DV]
        outs.append(o_inter + _dot(a_scr[h].astype(BF16), v))
        k_dec = (k * jnp.exp(b_end - b)).astype(BF16)
        st_scr[h] = st * jnp.exp(b_end) + _dot_tn(v, k_dec)
    pa_ref[:, half:] = _dot(oa_scr[prev], wpa_ref[:, half:])
    for h, o in enumerate(outs):
        vsl = slice(h * GLA_DV, (h + 1) * GLA_DV)
        oa_scr[cur, :, vsl] = _gla_out_gate(o, gout_ref[...], gr_ref[:, vsl])

    @pl.when(c == n_chunks - 1)
    def _():
        for h in range(heads):
            st_ref[h] = st_scr[h].T


def _gla_prompt(proj, glog, g_out, w_pa):
    t = proj.shape[0]
    c = GLA_CHUNK
    n_chunks = t // c
    heads = GLA_HEADS
    lvl, n_levels = _gla_level_table(c, c)
    tril = jnp.asarray(_gla_tril(c, c), BF16)
    lvl = jnp.asarray(lvl)
    qb, kb, vb, rb = COL_GQ // GLA_KEY, COL_GK // GLA_KEY, COL_GV // GLA_VAL, COL_GR // GLA_VAL
    chunk = lambda i: jnp.minimum(i, n_chunks - 1)
    whole = lambda i: (0, 0)
    return pl.pallas_call(
        functools.partial(_gla_prompt_body, n_levels=n_levels, heads=heads),
        grid=(n_chunks + 1,),
        in_specs=[pl.BlockSpec((c, GLA_KEY), lambda i: (chunk(i), qb)),
                  pl.BlockSpec((c, GLA_KEY), lambda i: (chunk(i), kb)),
                  pl.BlockSpec((c, GLA_VAL), lambda i: (chunk(i), vb)),
                  pl.BlockSpec((c, GLA_VAL), lambda i: (chunk(i), rb)),
                  pl.BlockSpec((c, GLA_KEY), lambda i: (chunk(i), 0)),
                  pl.BlockSpec((c, c), whole),
                  pl.BlockSpec((c, c), whole),
                  pl.BlockSpec((1, GLA_DV), whole),
                  pl.BlockSpec(w_pa.shape, whole, pipeline_mode=pl.Buffered(1))],
        out_specs=[pl.BlockSpec((c, w_pa.shape[1]), lambda i: (jnp.maximum(i - 1, 0), 0)),
                   pl.BlockSpec((heads, GLA_DK, GLA_DV), lambda i: (0, 0, 0))],
        out_shape=[jax.ShapeDtypeStruct((t, w_pa.shape[1]), F32),
                   jax.ShapeDtypeStruct((heads, GLA_DK, GLA_DV), F32)],
        scratch_shapes=[pltpu.VMEM((heads, GLA_DV, GLA_DK), F32), pltpu.VMEM((heads, c, GLA_DK), F32),
                        pltpu.VMEM((heads, c, c), F32), pltpu.VMEM((2, c, GLA_VAL), BF16)],
        compiler_params=_params("arbitrary"),
        name="gla_prompt",
    )(proj, proj, proj, proj, glog, tril, lvl, g_out, w_pa)


def _gla_sample_body(q_ref, k_ref, v_ref, gr_ref, g_ref, s0_ref, tril_ref, lvl_ref, gout_ref,
                     oa_ref, s1_ref, b_scr, *, n_levels):
    rows = q_ref.shape[0]
    nseq = rows // DEC_SEQ
    b = _gla_cumsum(g_ref[...], tril_ref[...])
    b_scr[...] = b
    b_end = jnp.concatenate(
        [jnp.broadcast_to(b_scr[(s + 1) * DEC_SEQ - 1:(s + 1) * DEC_SEQ, :], (DEC_SEQ, GLA_DK))
         for s in range(nseq)], axis=0)
    q = q_ref[...].astype(F32) * (GLA_DK ** -0.5)
    k_bf = k_ref[...]
    k = k_bf.astype(F32)
    v = v_ref[...]
    qd = (q * jnp.exp(b)).astype(BF16)
    a = _gla_intra(q, k, k_bf, b, b_scr, lvl_ref[...], n_levels)
    o_intra = _dot(a.astype(BF16), v)

    stacked = jnp.concatenate([k * jnp.exp(b_end - b), jnp.exp(b_end)], axis=0)
    stacked_t = stacked.T
    kd_t = stacked_t[:, :rows]
    lane = lax.broadcasted_iota(jnp.int32, kd_t.shape, 1)
    outs = []
    for s in range(nseq):
        s0 = s0_ref[s, 0]
        outs.append(_dot(qd[s * DEC_SEQ:(s + 1) * DEC_SEQ, :], s0.astype(BF16)))
        in_seq = (lane >= s * DEC_SEQ) & (lane < (s + 1) * DEC_SEQ)
        kd_s = jnp.where(in_seq, kd_t, 0.0).astype(BF16)
        col = rows + s * DEC_SEQ
        decay = stacked_t[:, col:col + 1]
        s1_ref[s, 0] = s0 * decay + _dot(kd_s, v)
    o = jnp.concatenate(outs, axis=0) + o_intra
    oa_ref[...] = _gla_out_gate(o, gout_ref[...], gr_ref[...])


def _gla_sample(proj, glog, state, g_out):
    t = proj.shape[0]
    nseq = GLA_SAMPLE_BATCH
    rows = nseq * DEC_SEQ
    lvl, n_levels = _gla_level_table(rows, DEC_SEQ)
    tril = jnp.asarray(_gla_tril(rows, DEC_SEQ), BF16)
    lvl = jnp.asarray(lvl)
    qb, kb, vb, rb = COL_GQ // GLA_DK, COL_GK // GLA_DK, COL_GV // GLA_DV, COL_GR // GLA_DV
    return pl.pallas_call(
        functools.partial(_gla_sample_body, n_levels=n_levels),
        grid=(t // rows, GLA_HEADS),
        in_specs=[pl.BlockSpec((rows, GLA_DK), lambda i, h: (i, qb + h)),
                  pl.BlockSpec((rows, GLA_DK), lambda i, h: (i, kb + h)),
                  pl.BlockSpec((rows, GLA_DV), lambda i, h: (i, vb + h)),
                  pl.BlockSpec((rows, GLA_DV), lambda i, h: (i, rb + h)),
                  pl.BlockSpec((rows, GLA_DK), lambda i, h: (i, h)),
                  pl.BlockSpec((nseq, 1, GLA_DK, GLA_DV), lambda i, h: (i, h, 0, 0)),
                  pl.BlockSpec((rows, rows), lambda i, h: (0, 0)),
                  pl.BlockSpec((rows, rows), lambda i, h: (0, 0)),
                  pl.BlockSpec((1, GLA_DV), lambda i, h: (0, 0))],
        out_specs=[pl.BlockSpec((rows, GLA_DV), lambda i, h: (i, h)),
                   pl.BlockSpec((nseq, 1, GLA_DK, GLA_DV), lambda i, h: (i, h, 0, 0))],
        out_shape=[jax.ShapeDtypeStruct((t, GLA_VAL), BF16),
                   jax.ShapeDtypeStruct(state.shape, F32)],
        scratch_shapes=[pltpu.VMEM((rows, GLA_DK), F32)],
        compiler_params=_params("arbitrary", "arbitrary"),
        name="gla_sample",
    )(proj, proj, proj, proj, glog, state, tril, lvl, g_out)


def _rope_tables(pos):
    half = SWA_HEAD_DIM // 2
    inv_freq = ROPE_THETA ** (-(jnp.arange(half, dtype=F32) * 2.0) / SWA_HEAD_DIM)
    ang = pos.astype(F32)[:, None] * inv_freq[None, :]
    cos, sin = jnp.cos(ang), jnp.sin(ang)
    cos_t = jnp.concatenate([cos, cos, cos, cos], axis=-1)
    sin_t = jnp.concatenate([-sin, sin, -sin, sin], axis=-1)
    return cos_t, sin_t


def _head_mean_matrix():
    i = np.arange(MXU_DIM)[:, None] // SWA_HEAD_DIM
    j = np.arange(MXU_DIM)[None, :] // SWA_HEAD_DIM
    return (i == j).astype(np.float32) / SWA_HEAD_DIM


def _qk_norm_rope(x, gain, cos_t, sin_t, mean_mat):
    return _norm_rope_tiles(x, _head_mean_squares(x, mean_mat), gain, cos_t, sin_t)


def _head_mean_squares(x, mean_mat):
    sq = (x * x).astype(BF16)
    return [_dot(sq[:, c * MXU_DIM:(c + 1) * MXU_DIM], mean_mat) for c in range(x.shape[1] // MXU_DIM)]


def _norm_rope_tiles(x, mean_sq, gain, cos_t, sin_t):
    rows = x.shape[0]
    lane = lax.broadcasted_iota(jnp.int32, (rows, LANES), 1)
    first_half = (lane % SWA_HEAD_DIM) < (SWA_HEAD_DIM // 2)
    shift = SWA_HEAD_DIM // 2
    tiles = []
    for c, ms in enumerate(mean_sq):
        for t in range(MXU_DIM // LANES):
            lo = c * MXU_DIM + t * LANES
            y = (x[:, lo:lo + LANES] * lax.rsqrt(ms[:, t * LANES:(t + 1) * LANES] + EPS)) * gain
            rot = jnp.where(first_half, pltpu.roll(y, LANES - shift, 1), pltpu.roll(y, shift, 1))
            tiles.append(y * cos_t + rot * sin_t)
    return tiles


def _attend_all_heads(q_tiles, k_all, v_all, valid, sinks_ref, store):
    pairs = SWA_GROUP // 2
    ppc = SWA_PAIRS_PER_CHAIN
    rows = valid.shape[0]
    low = lax.broadcasted_iota(jnp.int32, (k_all.shape[0], LANES), 1) < SWA_HEAD_DIM
    chains = [(g, sub, parity) for g in range(SWA_KV_HEADS) for sub in range(pairs // ppc) for parity in range(2)]
    q_group = {}

    def scores(g, sub, parity):
        lo = (g // 2) * LANES
        have_low, want_low = g % 2 == 0, parity == 0

        def place(x):
            src = x if have_low == want_low else pltpu.roll(x, SWA_HEAD_DIM, 1)
            return jnp.where(low if want_low else jnp.logical_not(low), src, 0.0).astype(BF16)
        if (g, sub) not in q_group:
            first = g * pairs + sub * ppc
            q_group[(g, sub)] = jnp.concatenate(q_tiles[first:first + ppc], axis=0).astype(BF16)
        return _dot_nt(q_group[(g, sub)], place(k_all[:, lo:lo + LANES])), place(v_all[:, lo:lo + LANES])

    def softmax(g, sub, parity, s_all):
        p_list, inv_list = [], []
        for i in range(ppc):
            sink = sinks_ref[g * SWA_GROUP + 2 * (sub * ppc + i) + parity] * LOG2_E
            s = jnp.where(valid, s_all[i * rows:(i + 1) * rows, :], NEG_INF)
            m = jnp.maximum(jnp.max(s, axis=-1, keepdims=True), sink)
            p = jnp.exp2(s - m)
            inv_list.append(1.0 / (jnp.sum(p, axis=-1, keepdims=True) + jnp.exp2(sink - m)))
            p_list.append(p.astype(BF16))
        return jnp.concatenate(p_list, axis=0), inv_list

    partial = None
    for g, sub, parity in chains:
        s_all, vv = scores(g, sub, parity)
        p_all, inv_list = softmax(g, sub, parity, s_all)
        pv = _dot(p_all, vv)
        outs = [pv[i * rows:(i + 1) * rows, :] * inv_list[i] for i in range(ppc)]
        if parity == 0:
            partial = outs
        else:
            for i in range(ppc):
                store(g * pairs + sub * ppc + i, (partial[i] + outs[i]).astype(BF16))


def _swa_prompt_body(sinks_ref, q_ref, k_ref, v_ref, cb_ref, sb_ref, cr_ref, sr_ref, crs_ref, srs_ref,
                     gq_ref, gk_ref, mm_ref, pa_ref, ga_ref, gb_ref, wpb_ref,
                     m_ref, knew_ref, vnew_ref, kprev_scr, vprev_scr, ob_scr):
    step = pl.program_id(0)
    n = jnp.minimum(step, pl.num_programs(0) - 2)
    blk = WINDOW
    nsub = q_ref.shape[0] // blk
    cur = lax.rem(step, 2)
    prev = 1 - cur

    @pl.when(step == 0)
    def _():
        kprev_scr[...] = jnp.zeros_like(kprev_scr)
        vprev_scr[...] = jnp.zeros_like(vprev_scr)
        ob_scr[...] = jnp.zeros_like(ob_scr)

    mean_mat = mm_ref[...]
    qi = lax.broadcasted_iota(jnp.int32, (blk, 2 * blk), 0)
    kj = lax.broadcasted_iota(jnp.int32, (blk, 2 * blk), 1)
    dist = blk + qi - kj
    in_window = (dist >= 0) & (dist <= WINDOW)
    k_prev, v_prev = kprev_scr[...], vprev_scr[...]
    for sub in range(nsub):
        rsl = slice(sub * blk, (sub + 1) * blk)
        b = n * nsub + sub
        cb, sb = cb_ref[pl.ds(b, 1), :], sb_ref[pl.ds(b, 1), :]
        cos_t = cb * cr_ref[...] - sb * sr_ref[...]
        sin_t = sb * crs_ref[...] + cb * srs_ref[...]
        xq, xk = q_ref[rsl, :].astype(F32), k_ref[rsl, :].astype(F32)
        ms_q, ms_k = _head_mean_squares(xq, mean_mat), _head_mean_squares(xk, mean_mat)
        if sub == 0:
            m_ref[...] = (_sigmoid(ga_ref[...].astype(F32)) * pa_ref[...]
                          + _sigmoid(gb_ref[...].astype(F32)) * _dot(ob_scr[prev], wpb_ref[...])).astype(BF16)
        q_tiles = [t * SWA_SCORE_SCALE for t in _norm_rope_tiles(xq, ms_q, gq_ref[...], cos_t, sin_t)]
        kn = jnp.concatenate(_norm_rope_tiles(xk, ms_k, gk_ref[...], cos_t, sin_t), axis=1)
        vn = v_ref[rsl, :].astype(F32)
        k_all = jnp.concatenate([k_prev, kn], axis=0)
        v_all = jnp.concatenate([v_prev, vn], axis=0)
        valid = in_window & ((b - 1) * blk + kj >= 0)

        def store(tile, value, rsl=rsl):
            ob_scr[cur, rsl, tile * LANES:(tile + 1) * LANES] = value
        _attend_all_heads(q_tiles, k_all, v_all, valid, sinks_ref, store)
        k_prev, v_prev = kn, vn

    knew_ref[...] = k_prev
    vnew_ref[...] = v_prev
    kprev_scr[...] = k_prev
    vprev_scr[...] = v_prev


def _rope_block_tables(nblocks, blk):
    half = SWA_HEAD_DIM // 2
    inv_freq = ROPE_THETA ** (-(jnp.arange(half, dtype=F32) * 2.0) / SWA_HEAD_DIM)
    lane_freq = jnp.tile(inv_freq, LANES // half)[None, :]
    sign = jnp.tile(jnp.concatenate([-jnp.ones((half,), F32), jnp.ones((half,), F32)]), LANES // SWA_HEAD_DIM)[None, :]
    ang_b = (jnp.arange(nblocks, dtype=jnp.int32) * blk).astype(F32)[:, None] * lane_freq
    ang_r = jnp.arange(blk, dtype=jnp.int32).astype(F32)[:, None] * lane_freq
    cr, sr = jnp.cos(ang_r), jnp.sin(ang_r)
    return jnp.cos(ang_b), jnp.sin(ang_b), cr, sr, sign * cr, sign * sr


def _swa_prompt(proj, pa, sinks, g_q, g_k, w_pb):
    t = proj.shape[0]
    d = w_pb.shape[1]
    blk = WINDOW
    rows = SWA_PROMPT_BLOCKS_PER_STEP * blk
    nblocks, nsteps = t // blk, t // rows
    mean_mat = jnp.asarray(_head_mean_matrix(), BF16)
    tables = _rope_block_tables(nblocks, blk)
    qb, kb, vb = COL_SQ // SWA_Q, COL_SK // SWA_KV, COL_SV // SWA_KV
    whole = lambda n, s: (0, 0)
    cur = lambda n: jnp.minimum(n, nsteps - 1)
    prev = lambda n: jnp.maximum(n - 1, 0)
    gate_rows = lambda n: pl.multiple_of(prev(n) * rows, rows)
    grid_spec = pltpu.PrefetchScalarGridSpec(
        num_scalar_prefetch=1,
        grid=(nsteps + 1,),
        in_specs=[pl.BlockSpec((rows, SWA_Q), lambda n, s: (cur(n), qb)),
                  pl.BlockSpec((rows, SWA_KV), lambda n, s: (cur(n), kb)),
                  pl.BlockSpec((rows, SWA_KV), lambda n, s: (cur(n), vb)),
                  pl.BlockSpec((nblocks, LANES), whole),
                  pl.BlockSpec((nblocks, LANES), whole),
                  pl.BlockSpec((blk, LANES), whole),
                  pl.BlockSpec((blk, LANES), whole),
                  pl.BlockSpec((blk, LANES), whole),
                  pl.BlockSpec((blk, LANES), whole),
                  pl.BlockSpec((1, LANES), whole),
                  pl.BlockSpec((1, LANES), whole),
                  pl.BlockSpec((MXU_DIM, MXU_DIM), whole),
                  pl.BlockSpec((rows, d), lambda n, s: (prev(n), 0)),
                  pl.BlockSpec((pl.Element(rows), pl.Element(d)), lambda n, s: (gate_rows(n), COL_BG)),
                  pl.BlockSpec((pl.Element(rows), pl.Element(d)), lambda n, s: (gate_rows(n), COL_BG + d)),
                  pl.BlockSpec(w_pb.shape, whole, pipeline_mode=pl.Buffered(1))],
        out_specs=[pl.BlockSpec((rows, d), lambda n, s: (prev(n), 0)),
                   pl.BlockSpec((blk, SWA_KV), whole),
                   pl.BlockSpec((blk, SWA_KV), whole)],
        scratch_shapes=[pltpu.VMEM((blk, SWA_KV), F32), pltpu.VMEM((blk, SWA_KV), F32),
                        pltpu.VMEM((2, rows, SWA_Q), BF16)],
    )
    return pl.pallas_call(
        _swa_prompt_body,
        grid_spec=grid_spec,
        out_shape=[jax.ShapeDtypeStruct((t, d), BF16),
                   jax.ShapeDtypeStruct((blk, SWA_KV), F32),
                   jax.ShapeDtypeStruct((blk, SWA_KV), F32)],
        compiler_params=_params("arbitrary"),
        name="swa_prompt",
    )(sinks, proj, proj, proj, *tables, g_q, g_k, mean_mat, pa, proj, proj, w_pb)


def _swa_sample_body(sinks_ref, q_ref, k_ref, v_ref, bk_ref, bv_ref, cos_ref, sin_ref, gq_ref, gk_ref, mm_ref,
                     ob_ref, kout_ref, vout_ref):
    nseq = bk_ref.shape[0]
    rows = nseq * DEC_SEQ
    cos_t = jnp.concatenate([cos_ref[...]] * nseq, axis=0)
    sin_t = jnp.concatenate([sin_ref[...]] * nseq, axis=0)
    mean_mat = mm_ref[...]
    q_tiles = [t * SWA_SCORE_SCALE
               for t in _qk_norm_rope(q_ref[...].astype(F32), gq_ref[...], cos_t, sin_t, mean_mat)]
    kn = jnp.concatenate(_qk_norm_rope(k_ref[...].astype(F32), gk_ref[...], cos_t, sin_t, mean_mat), axis=1)
    vn = v_ref[...].astype(F32)

    lane_w = lax.broadcasted_iota(jnp.int32, (SWA_KV, WINDOW), 1)
    pad_rows = jnp.zeros((WINDOW - DEC_SEQ, SWA_KV), F32)
    for s in range(nseq):
        rsl = slice(s * DEC_SEQ, (s + 1) * DEC_SEQ)
        for buf_ref, new, out_ref in ((bk_ref, kn, kout_ref), (bv_ref, vn, vout_ref)):
            new_t = jnp.concatenate([pad_rows, new[rsl, :]], axis=0).T
            out_ref[s] = jnp.where(lane_w < WINDOW - DEC_SEQ, pltpu.roll(buf_ref[s], WINDOW - DEC_SEQ, 1), new_t)

    tok = lax.broadcasted_iota(jnp.int32, (rows, WINDOW), 0) % DEC_SEQ
    valid_c = lax.broadcasted_iota(jnp.int32, (rows, WINDOW), 1) >= tok
    r_n = lax.broadcasted_iota(jnp.int32, (rows, LANES), 0)
    c_n = lax.broadcasted_iota(jnp.int32, (rows, LANES), 1)
    valid_n = (r_n // DEC_SEQ == c_n // DEC_SEQ) & (c_n % DEC_SEQ <= r_n % DEC_SEQ)
    pad_new = jnp.zeros((LANES - rows, LANES), F32)

    pairs = SWA_GROUP // 2
    zeros_dk = jnp.zeros((SWA_HEAD_DIM, nseq * WINDOW), F32)
    zero_blk = jnp.zeros((DEC_SEQ, WINDOW), F32)
    low = lax.broadcasted_iota(jnp.int32, (rows, LANES), 1) < SWA_HEAD_DIM
    for g in range(SWA_KV_HEADS):
        dsl = slice(g * SWA_HEAD_DIM, (g + 1) * SWA_HEAD_DIM)
        kt = jnp.concatenate([bk_ref[s, dsl, :] for s in range(nseq)], axis=1)
        vt = jnp.concatenate([bv_ref[s, dsl, :] for s in range(nseq)], axis=1)
        lo = (g // 2) * LANES
        kn_tile, vn_tile = kn[:, lo:lo + LANES], vn[:, lo:lo + LANES]
        kv_low = g % 2 == 0
        q_g = jnp.concatenate(q_tiles[g * pairs:(g + 1) * pairs], axis=0).astype(BF16)
        out = None
        for parity in range(2):
            want_low = parity == 0
            stack = (lambda a: jnp.concatenate([a, zeros_dk], axis=0)) if want_low else \
                    (lambda a: jnp.concatenate([zeros_dk, a], axis=0))
            keep = low if want_low else jnp.logical_not(low)
            align = (lambda a: a) if kv_low == want_low else (lambda a: pltpu.roll(a, SWA_HEAD_DIM, 1))
            place = lambda a: jnp.concatenate([jnp.where(keep, align(a), 0.0), pad_new], axis=0).astype(BF16)
            kn_p, vn_p = place(kn_tile), place(vn_tile)
            s_c = _dot(q_g, stack(kt).astype(BF16))
            s_n = _dot_nt(q_g, kn_p)
            pc_rows, pn_rows, inv = [], [], []
            for pair in range(pairs):
                sink = sinks_ref[g * SWA_GROUP + 2 * pair + parity] * LOG2_E
                psl = slice(pair * rows, (pair + 1) * rows)
                sc = jnp.concatenate(
                    [s_c[pair * rows + s * DEC_SEQ: pair * rows + (s + 1) * DEC_SEQ, s * WINDOW:(s + 1) * WINDOW]
                     for s in range(nseq)], axis=0)
                sc = jnp.where(valid_c, sc, NEG_INF)
                sn = jnp.where(valid_n, s_n[psl, :], NEG_INF)
                m = jnp.maximum(jnp.maximum(jnp.max(sc, axis=-1, keepdims=True),
                                            jnp.max(sn, axis=-1, keepdims=True)), sink)
                pc = jnp.exp2(sc - m)
                pn = jnp.exp2(sn - m)
                inv.append(1.0 / (jnp.sum(pc, axis=-1, keepdims=True) + jnp.sum(pn, axis=-1, keepdims=True)
                                  + jnp.exp2(sink - m)))
                pn_rows.append(pn)
                for s in range(nseq):
                    blocks = [pc[s * DEC_SEQ:(s + 1) * DEC_SEQ, :] if s2 == s else zero_blk for s2 in range(nseq)]
                    pc_rows.append(jnp.concatenate(blocks, axis=1))
            pv = (_dot_nt(jnp.concatenate(pc_rows, axis=0).astype(BF16), stack(vt).astype(BF16))
                  + _dot(jnp.concatenate(pn_rows, axis=0).astype(BF16), vn_p))
            pv = jnp.concatenate([pv[pair * rows:(pair + 1) * rows, :] * inv[pair] for pair in range(pairs)], axis=0)
            out = pv if out is None else out + pv
        for pair in range(pairs):
            tile = g * pairs + pair
            ob_ref[:, tile * LANES:(tile + 1) * LANES] = out[pair * rows:(pair + 1) * rows, :].astype(BF16)


def _swa_sample(proj, buf_k, buf_v, sinks, cos_t, sin_t, g_q, g_k):
    t = proj.shape[0]
    nseq = SWA_SAMPLE_BATCH
    rows = nseq * DEC_SEQ
    mean_mat = jnp.asarray(_head_mean_matrix(), BF16)
    qb, kb, vb = COL_SQ // SWA_Q, COL_SK // SWA_KV, COL_SV // SWA_KV
    grid_spec = pltpu.PrefetchScalarGridSpec(
        num_scalar_prefetch=1,
        grid=(t // rows,),
        in_specs=[pl.BlockSpec((rows, SWA_Q), lambda n, s: (n, qb)),
                  pl.BlockSpec((rows, SWA_KV), lambda n, s: (n, kb)),
                  pl.BlockSpec((rows, SWA_KV), lambda n, s: (n, vb)),
                  pl.BlockSpec((nseq, SWA_KV, WINDOW), lambda n, s: (n, 0, 0)),
                  pl.BlockSpec((nseq, SWA_KV, WINDOW), lambda n, s: (n, 0, 0)),
                  pl.BlockSpec((DEC_SEQ, LANES), lambda n, s: (0, 0)),
                  pl.BlockSpec((DEC_SEQ, LANES), lambda n, s: (0, 0)),
                  pl.BlockSpec((1, LANES), lambda n, s: (0, 0)),
                  pl.BlockSpec((1, LANES), lambda n, s: (0, 0)),
                  pl.BlockSpec((MXU_DIM, MXU_DIM), lambda n, s: (0, 0))],
        out_specs=[pl.BlockSpec((rows, SWA_Q), lambda n, s: (n, 0)),
                   pl.BlockSpec((nseq, SWA_KV, WINDOW), lambda n, s: (n, 0, 0)),
                   pl.BlockSpec((nseq, SWA_KV, WINDOW), lambda n, s: (n, 0, 0))],
    )
    return pl.pallas_call(
        _swa_sample_body,
        grid_spec=grid_spec,
        out_shape=[jax.ShapeDtypeStruct((t, SWA_Q), BF16),
                   jax.ShapeDtypeStruct(buf_k.shape, F32),
                   jax.ShapeDtypeStruct(buf_v.shape, F32)],
        compiler_params=_params("arbitrary"),
        name="swa_sample",
    )(sinks, proj, proj, proj, buf_k, buf_v, cos_t, sin_t, g_q, g_k, mean_mat)


def _merge_body(oa_ref, ob_ref, wpa_ref, wpb_ref, ga_ref, gb_ref, m_ref):
    ga = _sigmoid(ga_ref[...].astype(F32))
    gb = _sigmoid(gb_ref[...].astype(F32))
    m_ref[...] = (ga * _dot(oa_ref[...], wpa_ref[...]) + gb * _dot(ob_ref[...], wpb_ref[...])).astype(BF16)


def _merge(oa, ob, proj, w_pa, w_pb, tm=1024, tn=512):
    t, d = oa.shape
    ga_blk, gb_blk = COL_BG // tn, (COL_BG + D_MODEL) // tn
    return pl.pallas_call(
        _merge_body,
        grid=(t // tm, d // tn),
        in_specs=[pl.BlockSpec((tm, d), lambda i, j: (i, 0)),
                  pl.BlockSpec((tm, d), lambda i, j: (i, 0)),
                  pl.BlockSpec((d, tn), lambda i, j: (0, j)),
                  pl.BlockSpec((d, tn), lambda i, j: (0, j)),
                  pl.BlockSpec((tm, tn), lambda i, j: (i, ga_blk + j)),
                  pl.BlockSpec((tm, tn), lambda i, j: (i, gb_blk + j))],
        out_specs=pl.BlockSpec((tm, tn), lambda i, j: (i, j)),
        out_shape=jax.ShapeDtypeStruct((t, d), BF16),
        compiler_params=_params("arbitrary", "arbitrary"),
        name="merge",
    )(oa, ob, w_pa, w_pb, proj, proj)


OPROJ_ROW_CHUNK = 256


def _oproj_body(m_ref, w_ref, x_ref, gate_ref, gn_ref, sc_ref, sh_ref, x1_ref, h2_ref):
    for row0 in range(0, x_ref.shape[0], OPROJ_ROW_CHUNK):
        sl = slice(row0, row0 + OPROJ_ROW_CHUNK)
        y = _dot(m_ref[sl, :], w_ref[...])
        x1 = x_ref[sl, :] + _mod_rows(gate_ref, row0, OPROJ_ROW_CHUNK) * y
        x1_ref[sl, :] = x1
        ms = jnp.mean(x1 * x1, axis=-1, keepdims=True)
        h = (x1 * lax.rsqrt(ms + EPS)) * gn_ref[...]
        h2_ref[sl, :] = (h * (1.0 + _mod_rows(sc_ref, row0, OPROJ_ROW_CHUNK))
                         + _mod_rows(sh_ref, row0, OPROJ_ROW_CHUNK)).astype(BF16)


def _oproj(merged, w_o, x, gate, gn, scale, shift, tm=512):
    t, d = x.shape
    rows = lambda i: (i, 0)
    whole = lambda i: (0, 0)
    mod = lambda m: (pl.BlockSpec((1, d), whole) if m.shape[0] == 1
                     else pl.BlockSpec((tm // DEC_SEQ, d), rows))
    return pl.pallas_call(
        _oproj_body,
        grid=(t // tm,),
        in_specs=[pl.BlockSpec((tm, d), rows),
                  pl.BlockSpec(w_o.shape, whole, pipeline_mode=pl.Buffered(1)),
                  pl.BlockSpec((tm, d), rows),
                  mod(gate),
                  pl.BlockSpec((1, d), whole),
                  mod(scale),
                  mod(shift)],
        out_specs=[pl.BlockSpec((tm, d), rows), pl.BlockSpec((tm, d), rows)],
        out_shape=[jax.ShapeDtypeStruct(x.shape, F32), jax.ShapeDtypeStruct(x.shape, BF16)],
        compiler_params=_params("arbitrary"),
        name="oproj",
    )(merged, w_o, x, gate, gn, scale, shift)


def _ffn_body(h_ref, x_ref, gate_ref, wg_ref, wu_ref, wd_ref, o_ref):
    f = pl.program_id(1)

    @pl.when(f == 0)
    def _():
        def rows(row0):
            o_ref[pl.ds(row0, ROW_CHUNK), :] = jnp.zeros((ROW_CHUNK, o_ref.shape[1]), F32)
        _row_loop(x_ref.shape[0], rows)

    h = h_ref[...]
    a = _silu(_dot(h, wg_ref[...])) * _dot(h, wu_ref[...])
    o_ref[...] += _dot(a.astype(BF16), wd_ref[...])

    @pl.when(f == pl.num_programs(1) - 1)
    def _():
        def rows(row0):
            sl = pl.ds(row0, ROW_CHUNK)
            o_ref[sl, :] = x_ref[sl, :] + _mod_rows(gate_ref, row0, ROW_CHUNK) * o_ref[sl, :]
        _row_loop(x_ref.shape[0], rows)


def _ffn(h, x, gate, w_gate, w_up, w_down, tm=1024, tf=512):
    t, d = x.shape
    dff = w_gate.shape[1]
    zero = lambda j: 0
    return pl.pallas_call(
        _ffn_body,
        grid=(t // tm, dff // tf),
        in_specs=[pl.BlockSpec((tm, d), lambda i, f: (i, 0)),
                  pl.BlockSpec((tm, d), lambda i, f: (i, 0)),
                  _mod_spec(gate, tm, d, zero),
                  pl.BlockSpec((d, tf), lambda i, f: (0, f)),
                  pl.BlockSpec((d, tf), lambda i, f: (0, f)),
                  pl.BlockSpec((tf, d), lambda i, f: (f, 0))],
        out_specs=pl.BlockSpec((tm, d), lambda i, f: (i, 0)),
        out_shape=jax.ShapeDtypeStruct(x.shape, F32),
        compiler_params=_params("arbitrary", "arbitrary"),
        name="ffn",
    )(h, x, gate, w_gate, w_up, w_down)


def _mods(ada):
    return tuple(ada[:, i * D_MODEL:(i + 1) * D_MODEL] for i in range(6))


def _layer_after_inproj(x, proj, glog, mods, pos, state, buf_k, buf_v, wts):
    g_q, g_k, sinks, g_out, w_pa, w_pb, w_o, gn_ffn, w_fg, w_fu, w_fd = wts
    bsz, tlen, d = x.shape
    x2 = x.reshape(bsz * tlen, d)
    _, _, gate_m, shift_f, scale_f, gate_f = mods
    if state is None:
        pa, s_new = _gla_prompt(proj, glog, g_out, w_pa)
        s_new = s_new[None]
        merged, k_new, v_new = _swa_prompt(proj, pa, sinks, g_q, g_k, w_pb)
        k_new = k_new.reshape(bsz, WINDOW, SWA_KV_HEADS, SWA_HEAD_DIM)
        v_new = v_new.reshape(bsz, WINDOW, SWA_KV_HEADS, SWA_HEAD_DIM)
    else:
        to_dev = lambda c: jnp.transpose(c, (0, 2, 3, 1)).reshape(bsz, SWA_KV, WINDOW)
        from_dev = lambda c: jnp.transpose(c.reshape(bsz, SWA_KV_HEADS, SWA_HEAD_DIM, WINDOW), (0, 3, 1, 2))
        cos_t, sin_t = _rope_tables(pos)
        oa, s_new = _gla_sample(proj, glog, state, g_out)
        ob, k_new, v_new = _swa_sample(proj, to_dev(buf_k), to_dev(buf_v), sinks, cos_t, sin_t, g_q, g_k)
        k_new, v_new = from_dev(k_new), from_dev(v_new)
        merged = _merge(oa, ob, proj, w_pa, w_pb)
    x1, h2 = _oproj(merged, w_o, x2, gate_m, gn_ffn, scale_f, shift_f)
    y = _ffn(h2, x1, gate_f, w_fg, w_fu, w_fd)
    return y.reshape(bsz, tlen, d), s_new, k_new, v_new


def kernel(x_prompt, x_sample, c_prompt, c_sample, state_gla, cache_swa_k, cache_swa_v, w_ada, b_ada, g_norm_mix, w_in, w_gk2, b_gk2, g_qnorm, g_knorm, sinks, g_gla_out, w_pa, w_pb, w_o, g_norm_ffn, w_ffn_gate, w_ffn_up, w_ffn_down):
    assert w_in.shape[0] == 1, "single trunk layer"
    pos_p = jnp.arange(SEQ, dtype=jnp.int32)
    pos_s = PAST_LEN + jnp.arange(DEC_SEQ, dtype=jnp.int32)
    l = 0
    n_c = 1 + DEC_BATCH
    pad = (-n_c) % (2 * SUBLANES)
    c_all = jnp.pad(jnp.concatenate([c_prompt, c_sample], axis=0), ((0, pad), (0, 0)))
    ada, wt = _ada(c_all, w_ada[l], b_ada[l].reshape(1, -1), jnp.transpose(w_in[l]))
    mods_p, mods_s = _mods(ada[0:1]), _mods(ada[1:n_c])

    wt_lr = jnp.pad(wt[W_IN_GLR:W_IN_GLR + GLA_RANK], ((0, LANES - GLA_RANK), (0, 0)))
    w_gk = jnp.pad(w_gk2[l], ((0, LANES - GLA_RANK), (0, 0))).astype(BF16)
    in_wts = (wt, wt_lr, w_gk, b_gk2[l].reshape(1, GLA_KEY))
    gn_mix = g_norm_mix[l].reshape(1, D_MODEL)

    later_weights = (w_pa[l], w_pb[l], w_o[l], w_ffn_gate[l], w_ffn_up[l], w_ffn_down[l])
    proj_p, glog_p, *later_bf16 = _inproj(x_prompt.reshape(-1, D_MODEL), gn_mix, mods_p[1], mods_p[0], *in_wts,
                                          cast_weights=later_weights)
    proj_s, glog_s = _inproj(x_sample.reshape(-1, D_MODEL), gn_mix, mods_s[1], mods_s[0], *in_wts)
    w_pa_b, w_pb_b, w_o_b, w_fg_b, w_fu_b, w_fd_b = later_bf16
    wts = (jnp.tile(g_qnorm[l], 2).reshape(1, LANES), jnp.tile(g_knorm[l], 2).reshape(1, LANES),
           sinks[l], g_gla_out[l].reshape(1, GLA_DV), w_pa_b, w_pb_b, w_o_b,
           g_norm_ffn[l].reshape(1, D_MODEL), w_fg_b, w_fu_b, w_fd_b)

    yp, sp, kp, vp = _layer_after_inproj(x_prompt, proj_p, glog_p, mods_p, pos_p, None, None, None, wts)
    ys, ss, ks, vs = _layer_after_inproj(x_sample, proj_s, glog_s, mods_s, pos_s, state_gla[l], cache_swa_k[l],
                                         cache_swa_v[l], wts)
    return (yp, ys, sp[None], kp[None], vp[None], ss[None], ks[None], vs[None])
```

```python
import functools

import numpy as np
import jax
import jax.numpy as jnp
from jax import lax
from jax.experimental import pallas as pl
from jax.experimental.pallas import tpu as pltpu

F32 = jnp.float32
BF16 = jnp.bfloat16

D_MODEL = 2048
SEQ = 16384
DEC_BATCH = 128
DEC_SEQ = 8
PAST_LEN = 16384
GLA_HEADS = 4
GLA_DK = 256
GLA_DV = 512
GLA_KEY = GLA_HEADS * GLA_DK
GLA_VAL = GLA_HEADS * GLA_DV
GLA_RANK = 16
GLA_TAU = 16.0
SWA_HEAD_DIM = 64
SWA_HEADS = 32
SWA_KV_HEADS = 4
SWA_GROUP = SWA_HEADS // SWA_KV_HEADS
SWA_Q = SWA_HEADS * SWA_HEAD_DIM
SWA_KV = SWA_KV_HEADS * SWA_HEAD_DIM
WINDOW = 128
ROPE_THETA = 10000.0
D_FF = 5632
EPS = 1e-6
NEG_INF = -1e30
LOG2_E = 1.4426950408889634
SWA_SCORE_SCALE = (SWA_HEAD_DIM ** -0.5) * LOG2_E

SUBLANES = 8
LANES = 128
MXU_DIM = 256
VMEM_LIMIT_BYTES = 58 * 1024 * 1024

COL_GQ = 0
COL_GK = COL_GQ + GLA_KEY
COL_GV = COL_GK + GLA_KEY
COL_GR = COL_GV + GLA_VAL
COL_SQ = COL_GR + GLA_VAL
COL_SK = COL_SQ + SWA_Q
COL_SV = COL_SK + SWA_KV
COL_BG = COL_SV + SWA_KV
PROJ_COLS = COL_BG + 2 * D_MODEL
W_IN_GLR = COL_GR + GLA_VAL

GLA_CHUNK = 256
GLA_ONE_SIDED_MAX_DECAY = 80.0
GLA_SAMPLE_BATCH = 16
SWA_SAMPLE_BATCH = 8
SWA_PROMPT_BLOCKS_PER_STEP = 1
SWA_PAIRS_PER_CHAIN = 4
ROW_CHUNK = 512


def _dot(a, b):
    return jnp.dot(a, b, preferred_element_type=F32)


def _dot_nt(a, b):
    return lax.dot_general(a, b, (((1,), (1,)), ((), ())), preferred_element_type=F32)


def _dot_tn(a, b):
    return lax.dot_general(a, b, (((0,), (0,)), ((), ())), preferred_element_type=F32)


def _sigmoid(x):
    return 1.0 / (1.0 + jnp.exp(-x))


def _silu(x):
    return x * _sigmoid(x)


def _split3(x):
    hi = x.astype(BF16)
    r1 = x - hi.astype(F32)
    mid = r1.astype(BF16)
    lo = (r1 - mid.astype(F32)).astype(BF16)
    return hi, mid, lo


def _params(*sem):
    return pltpu.CompilerParams(dimension_semantics=sem, vmem_limit_bytes=VMEM_LIMIT_BYTES)


def _ada_body(c_ref, w_ref, b_ref, wt_ref, o_ref, wtb_ref):
    a = _silu(c_ref[...]).astype(BF16)
    o_ref[...] = _dot(a, w_ref[...].astype(BF16)) + b_ref[...]
    wtb_ref[...] = wt_ref[...].astype(BF16)


def _ada(c_all, w_ada, b_ada, wt, tn=512):
    m, d = c_all.shape
    n = w_ada.shape[1]
    steps = n // tn
    rows = wt.shape[0]
    align = 2 * SUBLANES
    slab = -(-rows // (steps * align)) * align
    assert (steps - 1) * slab < rows <= steps * slab
    slab_spec = pl.BlockSpec((slab, wt.shape[1]), lambda j: (j, 0))
    return pl.pallas_call(
        _ada_body,
        grid=(steps,),
        in_specs=[pl.BlockSpec((m, d), lambda j: (0, 0)),
                  pl.BlockSpec((d, tn), lambda j: (0, j)),
                  pl.BlockSpec((1, tn), lambda j: (0, j)),
                  slab_spec],
        out_specs=[pl.BlockSpec((m, tn), lambda j: (0, j)), slab_spec],
        out_shape=[jax.ShapeDtypeStruct((m, n), F32), jax.ShapeDtypeStruct(wt.shape, BF16)],
        compiler_params=_params("arbitrary"),
        name="ada",
    )(c_all, w_ada, b_ada, wt)


def _mod_spec(mod, tm, width, col):
    if mod.shape[0] == 1:
        return pl.BlockSpec((1, width), lambda i, j: (0, col(j)))
    return pl.BlockSpec((tm // DEC_SEQ, width), lambda i, j: (i, col(j)))


def _mod_rows(ref, row0, nrows):
    if ref.shape[0] == 1:
        return ref[...]
    seq0 = row0 // DEC_SEQ
    return jnp.concatenate(
        [jnp.broadcast_to(ref[pl.ds(seq0 + s, 1), :], (DEC_SEQ, ref.shape[1])) for s in range(nrows // DEC_SEQ)],
        axis=0)


def _modulated_norm_rows(x_ref, gn_ref, sc_ref, sh_ref, row0, nrows):
    x = x_ref[pl.ds(row0, nrows), :]
    ms = jnp.mean(x * x, axis=-1, keepdims=True)
    y = (x * lax.rsqrt(ms + EPS)) * gn_ref[...]
    return y * (1.0 + _mod_rows(sc_ref, row0, nrows)) + _mod_rows(sh_ref, row0, nrows)


def _row_loop(total_rows, body):
    assert total_rows % ROW_CHUNK == 0, (total_rows, ROW_CHUNK)

    def step(r, carry):
        body(pl.multiple_of(r * ROW_CHUNK, ROW_CHUNK))
        return carry
    lax.fori_loop(0, total_rows // ROW_CHUNK, step, 0)


def _inproj_body(x_ref, gn_ref, sc_ref, sh_ref, wt_ref, wlr_ref, wgk_ref, bgk_ref, *rest, n_casts):
    cast_src = rest[:n_casts]
    proj_ref, glog_ref = rest[n_casts:n_casts + 2]
    cast_dst = rest[n_casts + 2:2 * n_casts + 2]
    h_scr = rest[-1]
    for src, dst in zip(cast_src, cast_dst):
        dst[...] = src[...].astype(BF16)

    @pl.when(pl.program_id(1) == 0)
    def _():
        def rows(row0):
            hb = _modulated_norm_rows(x_ref, gn_ref, sc_ref, sh_ref, row0, ROW_CHUNK).astype(BF16)
            h_scr[pl.ds(row0, ROW_CHUNK), :] = hb
            glr = _dot_nt(hb, wlr_ref[...])
            z = _dot(glr.astype(BF16), wgk_ref[...]) + bgk_ref[...]
            log_sig = jnp.minimum(z, 0.0) - jnp.log1p(jnp.exp(-jnp.abs(z)))
            glog_ref[pl.ds(row0, ROW_CHUNK), :] = log_sig * (1.0 / GLA_TAU)
        _row_loop(x_ref.shape[0], rows)

    proj_ref[...] = _dot_nt(h_scr[...], wt_ref[...]).astype(BF16)


def _cast_block_spec(shape, row_tiles, col_steps):
    rows, cols = shape
    br = rows // row_tiles
    assert br * row_tiles == rows and br % (2 * SUBLANES) == 0
    ncb = max(c for c in range(1, col_steps + 1) if cols % c == 0 and (cols // c) % LANES == 0)
    return pl.BlockSpec((br, cols // ncb), lambda i, j: (i, jnp.minimum(j, ncb - 1)))


def _inproj(x, gn, scale, shift, wt, wt_lr, w_gk, b_gk, cast_weights=(), tm=2048, tn=512):
    t, d = x.shape
    tm = min(tm, t)
    n = wt.shape[0] - GLA_RANK
    n_first = W_IN_GLR // tn
    grid = (t // tm, n // tn)
    zero = lambda j: 0
    wt_rows = lambda i, j: (pl.multiple_of(j * tn + jnp.where(j >= n_first, GLA_RANK, 0), GLA_RANK), 0)
    cast_specs = [_cast_block_spec(w.shape, *grid) for w in cast_weights]
    return pl.pallas_call(
        functools.partial(_inproj_body, n_casts=len(cast_weights)),
        grid=grid,
        in_specs=[pl.BlockSpec((tm, d), lambda i, j: (i, 0), pipeline_mode=pl.Buffered(1)),
                  pl.BlockSpec((1, d), lambda i, j: (0, 0)),
                  _mod_spec(scale, tm, d, zero),
                  _mod_spec(shift, tm, d, zero),
                  pl.BlockSpec((pl.Element(tn), pl.Element(d)), wt_rows),
                  pl.BlockSpec(wt_lr.shape, lambda i, j: (0, 0)),
                  pl.BlockSpec(w_gk.shape, lambda i, j: (0, 0)),
                  pl.BlockSpec(b_gk.shape, lambda i, j: (0, 0))] + cast_specs,
        out_specs=[pl.BlockSpec((tm, tn), lambda i, j: (i, j)),
                   pl.BlockSpec((tm, GLA_KEY), lambda i, j: (i, 0))] + cast_specs,
        out_shape=[jax.ShapeDtypeStruct((t, n), BF16),
                   jax.ShapeDtypeStruct((t, GLA_KEY), F32)]
                  + [jax.ShapeDtypeStruct(w.shape, BF16) for w in cast_weights],
        scratch_shapes=[pltpu.VMEM((tm, d), BF16)],
        compiler_params=_params("arbitrary", "arbitrary"),
        name="inproj",
    )(x, gn, scale, shift, wt, wt_lr, w_gk, b_gk, *cast_weights)


def _gla_level_table(c, group):
    n_levels = int(np.log2(group))
    i = np.arange(c)[:, None]
    j = np.arange(c)[None, :]
    x = np.bitwise_xor(i, j)
    lvl = np.floor(np.log2(np.maximum(x, 1))).astype(np.int32)
    lvl = np.where(i == j, n_levels, lvl)
    valid = (i >= j) & (i // group == j // group)
    return np.where(valid, lvl, -1).astype(np.int32), n_levels


def _gla_tril(c, group):
    i = np.arange(c)[:, None]
    j = np.arange(c)[None, :]
    return ((i >= j) & (i // group == j // group)).astype(np.float32)


def _level_reference(b_scr, rows, level):
    s = 2 ** (level + 1)
    width = b_scr.shape[1]
    pieces = []
    if s >= SUBLANES:
        for blk in range(rows // s):
            mid = blk * s + s // 2
            pieces.append(jnp.broadcast_to(b_scr[mid:mid + 1, :], (s, width)))
    else:
        p = lax.broadcasted_iota(jnp.int32, (SUBLANES, width), 0)
        for tile in range(rows // SUBLANES):
            base = tile * SUBLANES
            mids = [base + q * s + s // 2 for q in range(SUBLANES // s)]
            r = jnp.broadcast_to(b_scr[mids[-1]:mids[-1] + 1, :], (SUBLANES, width))
            for q in range(SUBLANES // s - 2, -1, -1):
                row = jnp.broadcast_to(b_scr[mids[q]:mids[q] + 1, :], (SUBLANES, width))
                r = jnp.where(p < (q + 1) * s, row, r)
            pieces.append(r)
    return jnp.concatenate(pieces, axis=0) if len(pieces) > 1 else pieces[0]


def _gla_intra(q, k, k_bf, b, b_scr, lvl, n_levels):
    rows = q.shape[0]
    a = jnp.where(lvl == n_levels, _dot_nt(q.astype(BF16), k_bf), 0.0)
    for level in range(n_levels):
        f = jnp.exp(-jnp.abs(b - _level_reference(b_scr, rows, level)))
        p = _dot_nt((q * f).astype(BF16), (k * f).astype(BF16))
        a = jnp.where(lvl == level, p, a)
    return a


def _gla_cumsum(g, tril_bf):
    hi, mid, lo = _split3(g)
    return _dot(tril_bf, hi) + _dot(tril_bf, mid) + _dot(tril_bf, lo)


def _gla_out_gate(o, gout, gr):
    ms = jnp.mean(o * o, axis=-1, keepdims=True)
    y = (o * lax.rsqrt(ms + EPS)) * gout
    return (y * _silu(gr.astype(F32))).astype(BF16)


def _gla_prompt_body(q_ref, k_ref, v_ref, gr_ref, g_ref, tril_ref, lvl_ref, gout_ref, wpa_ref,
                     pa_ref, st_ref, st_scr, b_scr, a_scr, oa_scr, *, n_levels, heads):
    c = pl.program_id(0)
    n_chunks = pl.num_programs(0) - 1
    rows = q_ref.shape[0]
    cur = lax.rem(c, 2)
    prev = 1 - cur
    half = pa_ref.shape[1] // 2

    @pl.when(c == 0)
    def _():
        st_scr[...] = jnp.zeros_like(st_scr)
        oa_scr[...] = jnp.zeros_like(oa_scr)

    tril, lvl = tril_ref[...], lvl_ref[...]
    per_head = []
    mild = None
    for h in range(heads):
        ksl = slice(h * GLA_DK, (h + 1) * GLA_DK)
        b = _gla_cumsum(g_ref[:, ksl], tril)
        b_scr[h] = b
        b_end = b_scr[h, rows - 1:rows, :]
        q = q_ref[:, ksl].astype(F32) * (GLA_DK ** -0.5)
        k_bf = k_ref[:, ksl]
        k = k_bf.astype(F32)
        st = st_scr[h]
        qd = (q * jnp.exp(b)).astype(BF16)
        o_inter = _dot_nt(qd, st.astype(BF16))
        head_mild = jnp.min(b_end) > -GLA_ONE_SIDED_MAX_DECAY
        mild = head_mild if mild is None else jnp.logical_and(mild, head_mild)
        per_head.append((b, b_end, q, k, k_bf, st, qd, o_inter))

    pa_ref[:, :half] = _dot(oa_scr[prev], wpa_ref[:, :half])

    @pl.when(mild)
    def _():
        for h, (b, _, _, k, _, _, qd, _) in enumerate(per_head):
            a_scr[h] = jnp.where(lvl >= 0, _dot_nt(qd, (k * jnp.exp(-b)).astype(BF16)), 0.0)

    @pl.when(jnp.logical_not(mild))
    def _():
        for h, (b, _, q, k, k_bf, _, _, _) in enumerate(per_head):
            a_scr[h] = _gla_intra(q, k, k_bf, b, b_scr.at[h], lvl, n_levels)

    outs = []
    for h, (b, b_end, _, k, _, st, _, o_inter) in enumerate(per_head):
        v = v_ref[:, h * GLA_DV:(h + 1) * GLA_DV]
        outs.append(o_inter + _dot(a_scr[h].astype(BF16), v))
        k_dec = (k * jnp.exp(b_end - b)).astype(BF16)
        st_scr[h] = st * jnp.exp(b_end) + _dot_tn(v, k_dec)
    pa_ref[:, half:] = _dot(oa_scr[prev], wpa_ref[:, half:])
    for h, o in enumerate(outs):
        vsl = slice(h * GLA_DV, (h + 1) * GLA_DV)
        oa_scr[cur, :, vsl] = _gla_out_gate(o, gout_ref[...], gr_ref[:, vsl])

    @pl.when(c == n_chunks - 1)
    def _():
        for h in range(heads):
            st_ref[h] = st_scr[h].T


def _gla_prompt(proj, glog, g_out, w_pa):
    t = proj.shape[0]
    c = GLA_CHUNK
    n_chunks = t // c
    heads = GLA_HEADS
    lvl, n_levels = _gla_level_table(c, c)
    tril = jnp.asarray(_gla_tril(c, c), BF16)
    lvl = jnp.asarray(lvl)
    qb, kb, vb, rb = COL_GQ // GLA_KEY, COL_GK // GLA_KEY, COL_GV // GLA_VAL, COL_GR // GLA_VAL
    chunk = lambda i: jnp.minimum(i, n_chunks - 1)
    whole = lambda i: (0, 0)
    return pl.pallas_call(
        functools.partial(_gla_prompt_body, n_levels=n_levels, heads=heads),
        grid=(n_chunks + 1,),
        in_specs=[pl.BlockSpec((c, GLA_KEY), lambda i: (chunk(i), qb)),
                  pl.BlockSpec((c, GLA_KEY), lambda i: (chunk(i), kb)),
                  pl.BlockSpec((c, GLA_VAL), lambda i: (chunk(i), vb)),
                  pl.BlockSpec((c, GLA_VAL), lambda i: (chunk(i), rb)),
                  pl.BlockSpec((c, GLA_KEY), lambda i: (chunk(i), 0)),
                  pl.BlockSpec((c, c), whole),
                  pl.BlockSpec((c, c), whole),
                  pl.BlockSpec((1, GLA_DV), whole),
                  pl.BlockSpec(w_pa.shape, whole, pipeline_mode=pl.Buffered(1))],
        out_specs=[pl.BlockSpec((c, w_pa.shape[1]), lambda i: (jnp.maximum(i - 1, 0), 0)),
                   pl.BlockSpec((heads, GLA_DK, GLA_DV), lambda i: (0, 0, 0))],
        out_shape=[jax.ShapeDtypeStruct((t, w_pa.shape[1]), F32),
                   jax.ShapeDtypeStruct((heads, GLA_DK, GLA_DV), F32)],
        scratch_shapes=[pltpu.VMEM((heads, GLA_DV, GLA_DK), F32), pltpu.VMEM((heads, c, GLA_DK), F32),
                        pltpu.VMEM((heads, c, c), F32), pltpu.VMEM((2, c, GLA_VAL), BF16)],
        compiler_params=_params("arbitrary"),
        name="gla_prompt",
    )(proj, proj, proj, proj, glog, tril, lvl, g_out, w_pa)


def _gla_sample_body(q_ref, k_ref, v_ref, gr_ref, g_ref, s0_ref, tril_ref, lvl_ref, gout_ref,
                     oa_ref, s1_ref, b_scr, *, n_levels):
    rows = q_ref.shape[0]
    nseq = rows // DEC_SEQ
    b = _gla_cumsum(g_ref[...], tril_ref[...])
    b_scr[...] = b
    b_end = jnp.concatenate(
        [jnp.broadcast_to(b_scr[(s + 1) * DEC_SEQ - 1:(s + 1) * DEC_SEQ, :], (DEC_SEQ, GLA_DK))
         for s in range(nseq)], axis=0)
    q = q_ref[...].astype(F32) * (GLA_DK ** -0.5)
    k_bf = k_ref[...]
    k = k_bf.astype(F32)
    v = v_ref[...]
    qd = (q * jnp.exp(b)).astype(BF16)
    a = _gla_intra(q, k, k_bf, b, b_scr, lvl_ref[...], n_levels)
    o_intra = _dot(a.astype(BF16), v)

    stacked = jnp.concatenate([k * jnp.exp(b_end - b), jnp.exp(b_end)], axis=0)
    stacked_t = stacked.T
    kd_t = stacked_t[:, :rows]
    lane = lax.broadcasted_iota(jnp.int32, kd_t.shape, 1)
    outs = []
    for s in range(nseq):
        s0 = s0_ref[s, 0]
        outs.append(_dot(qd[s * DEC_SEQ:(s + 1) * DEC_SEQ, :], s0.astype(BF16)))
        in_seq = (lane >= s * DEC_SEQ) & (lane < (s + 1) * DEC_SEQ)
        kd_s = jnp.where(in_seq, kd_t, 0.0).astype(BF16)
        col = rows + s * DEC_SEQ
        decay = stacked_t[:, col:col + 1]
        s1_ref[s, 0] = s0 * decay + _dot(kd_s, v)
    o = jnp.concatenate(outs, axis=0) + o_intra
    oa_ref[...] = _gla_out_gate(o, gout_ref[...], gr_ref[...])


def _gla_sample(proj, glog, state, g_out):
    t = proj.shape[0]
    nseq = GLA_SAMPLE_BATCH
    rows = nseq * DEC_SEQ
    lvl, n_levels = _gla_level_table(rows, DEC_SEQ)
    tril = jnp.asarray(_gla_tril(rows, DEC_SEQ), BF16)
    lvl = jnp.asarray(lvl)
    qb, kb, vb, rb = COL_GQ // GLA_DK, COL_GK // GLA_DK, COL_GV // GLA_DV, COL_GR // GLA_DV
    return pl.pallas_call(
        functools.partial(_gla_sample_body, n_levels=n_levels),
        grid=(t // rows, GLA_HEADS),
        in_specs=[pl.BlockSpec((rows, GLA_DK), lambda i, h: (i, qb + h)),
                  pl.BlockSpec((rows, GLA_DK), lambda i, h: (i, kb + h)),
                  pl.BlockSpec((rows, GLA_DV), lambda i, h: (i, vb + h)),
                  pl.BlockSpec((rows, GLA_DV), lambda i, h: (i, rb + h)),
                  pl.BlockSpec((rows, GLA_DK), lambda i, h: (i, h)),
                  pl.BlockSpec((nseq, 1, GLA_DK, GLA_DV), lambda i, h: (i, h, 0, 0)),
                  pl.BlockSpec((rows, rows), lambda i, h: (0, 0)),
                  pl.BlockSpec((rows, rows), lambda i, h: (0, 0)),
                  pl.BlockSpec((1, GLA_DV), lambda i, h: (0, 0))],
        out_specs=[pl.BlockSpec((rows, GLA_DV), lambda i, h: (i, h)),
                   pl.BlockSpec((nseq, 1, GLA_DK, GLA_DV), lambda i, h: (i, h, 0, 0))],
        out_shape=[jax.ShapeDtypeStruct((t, GLA_VAL), BF16),
                   jax.ShapeDtypeStruct(state.shape, F32)],
        scratch_shapes=[pltpu.VMEM((rows, GLA_DK), F32)],
        compiler_params=_params("arbitrary", "arbitrary"),
        name="gla_sample",
    )(proj, proj, proj, proj, glog, state, tril, lvl, g_out)


def _rope_tables(pos):
    half = SWA_HEAD_DIM // 2
    inv_freq = ROPE_THETA ** (-(jnp.arange(half, dtype=F32) * 2.0) / SWA_HEAD_DIM)
    ang = pos.astype(F32)[:, None] * inv_freq[None, :]
    cos, sin = jnp.cos(ang), jnp.sin(ang)
    cos_t = jnp.concatenate([cos, cos, cos, cos], axis=-1)
    sin_t = jnp.concatenate([-sin, sin, -sin, sin], axis=-1)
    return cos_t, sin_t


def _head_mean_matrix():
    i = np.arange(MXU_DIM)[:, None] // SWA_HEAD_DIM
    j = np.arange(MXU_DIM)[None, :] // SWA_HEAD_DIM
    return (i == j).astype(np.float32) / SWA_HEAD_DIM


def _qk_norm_rope(x, gain, cos_t, sin_t, mean_mat):
    return _norm_rope_tiles(x, _head_mean_squares(x, mean_mat), gain, cos_t, sin_t)


def _head_mean_squares(x, mean_mat):
    sq = (x * x).astype(BF16)
    return [_dot(sq[:, c * MXU_DIM:(c + 1) * MXU_DIM], mean_mat) for c in range(x.shape[1] // MXU_DIM)]


def _norm_rope_tiles(x, mean_sq, gain, cos_t, sin_t):
    rows = x.shape[0]
    lane = lax.broadcasted_iota(jnp.int32, (rows, LANES), 1)
    first_half = (lane % SWA_HEAD_DIM) < (SWA_HEAD_DIM // 2)
    shift = SWA_HEAD_DIM // 2
    tiles = []
    for c, ms in enumerate(mean_sq):
        for t in range(MXU_DIM // LANES):
            lo = c * MXU_DIM + t * LANES
            y = (x[:, lo:lo + LANES] * lax.rsqrt(ms[:, t * LANES:(t + 1) * LANES] + EPS)) * gain
            rot = jnp.where(first_half, pltpu.roll(y, LANES - shift, 1), pltpu.roll(y, shift, 1))
            tiles.append(y * cos_t + rot * sin_t)
    return tiles


def _attend_all_heads(q_tiles, k_all, v_all, valid, sinks_ref, store):
    pairs = SWA_GROUP // 2
    ppc = SWA_PAIRS_PER_CHAIN
    rows = valid.shape[0]
    low = lax.broadcasted_iota(jnp.int32, (k_all.shape[0], LANES), 1) < SWA_HEAD_DIM
    chains = [(g, sub, parity) for g in range(SWA_KV_HEADS) for sub in range(pairs // ppc) for parity in range(2)]
    q_group = {}

    def scores(g, sub, parity):
        lo = (g // 2) * LANES
        have_low, want_low = g % 2 == 0, parity == 0

        def place(x):
            src = x if have_low == want_low else pltpu.roll(x, SWA_HEAD_DIM, 1)
            return jnp.where(low if want_low else jnp.logical_not(low), src, 0.0).astype(BF16)
        if (g, sub) not in q_group:
            first = g * pairs + sub * ppc
            q_group[(g, sub)] = jnp.concatenate(q_tiles[first:first + ppc], axis=0).astype(BF16)
        return _dot_nt(q_group[(g, sub)], place(k_all[:, lo:lo + LANES])), place(v_all[:, lo:lo + LANES])

    def softmax(g, sub, parity, s_all):
        p_list, inv_list = [], []
        for i in range(ppc):
            sink = sinks_ref[g * SWA_GROUP + 2 * (sub * ppc + i) + parity] * LOG2_E
            s = jnp.where(valid, s_all[i * rows:(i + 1) * rows, :], NEG_INF)
            m = jnp.maximum(jnp.max(s, axis=-1, keepdims=True), sink)
            p = jnp.exp2(s - m)
            inv_list.append(1.0 / (jnp.sum(p, axis=-1, keepdims=True) + jnp.exp2(sink - m)))
            p_list.append(p.astype(BF16))
        return jnp.concatenate(p_list, axis=0), inv_list

    partial = None
    for g, sub, parity in chains:
        s_all, vv = scores(g, sub, parity)
        p_all, inv_list = softmax(g, sub, parity, s_all)
        pv = _dot(p_all, vv)
        outs = [pv[i * rows:(i + 1) * rows, :] * inv_list[i] for i in range(ppc)]
        if parity == 0:
            partial = outs
        else:
            for i in range(ppc):
                store(g * pairs + sub * ppc + i, (partial[i] + outs[i]).astype(BF16))


def _swa_prompt_body(sinks_ref, q_ref, k_ref, v_ref, cb_ref, sb_ref, cr_ref, sr_ref, crs_ref, srs_ref,
                     gq_ref, gk_ref, mm_ref, pa_ref, ga_ref, gb_ref, wpb_ref,
                     m_ref, knew_ref, vnew_ref, kprev_scr, vprev_scr, ob_scr):
    step = pl.program_id(0)
    n = jnp.minimum(step, pl.num_programs(0) - 2)
    blk = WINDOW
    nsub = q_ref.shape[0] // blk
    cur = lax.rem(step, 2)
    prev = 1 - cur

    @pl.when(step == 0)
    def _():
        kprev_scr[...] = jnp.zeros_like(kprev_scr)
        vprev_scr[...] = jnp.zeros_like(vprev_scr)
        ob_scr[...] = jnp.zeros_like(ob_scr)

    mean_mat = mm_ref[...]
    qi = lax.broadcasted_iota(jnp.int32, (blk, 2 * blk), 0)
    kj = lax.broadcasted_iota(jnp.int32, (blk, 2 * blk), 1)
    dist = blk + qi - kj
    in_window = (dist >= 0) & (dist <= WINDOW)
    k_prev, v_prev = kprev_scr[...], vprev_scr[...]
    for sub in range(nsub):
        rsl = slice(sub * blk, (sub + 1) * blk)
        b = n * nsub + sub
        cb, sb = cb_ref[pl.ds(b, 1), :], sb_ref[pl.ds(b, 1), :]
        cos_t = cb * cr_ref[...] - sb * sr_ref[...]
        sin_t = sb * crs_ref[...] + cb * srs_ref[...]
        xq, xk = q_ref[rsl, :].astype(F32), k_ref[rsl, :].astype(F32)
        ms_q, ms_k = _head_mean_squares(xq, mean_mat), _head_mean_squares(xk, mean_mat)
        if sub == 0:
            m_ref[...] = (_sigmoid(ga_ref[...].astype(F32)) * pa_ref[...]
                          + _sigmoid(gb_ref[...].astype(F32)) * _dot(ob_scr[prev], wpb_ref[...])).astype(BF16)
        q_tiles = [t * SWA_SCORE_SCALE for t in _norm_rope_tiles(xq, ms_q, gq_ref[...], cos_t, sin_t)]
        kn = jnp.concatenate(_norm_rope_tiles(xk, ms_k, gk_ref[...], cos_t, sin_t), axis=1)
        vn = v_ref[rsl, :].astype(F32)
        k_all = jnp.concatenate([k_prev, kn], axis=0)
        v_all = jnp.concatenate([v_prev, vn], axis=0)
        valid = in_window & ((b - 1) * blk + kj >= 0)

        def store(tile, value, rsl=rsl):
            ob_scr[cur, rsl, tile * LANES:(tile + 1) * LANES] = value
        _attend_all_heads(q_tiles, k_all, v_all, valid, sinks_ref, store)
        k_prev, v_prev = kn, vn

    knew_ref[...] = k_prev
    vnew_ref[...] = v_prev
    kprev_scr[...] = k_prev
    vprev_scr[...] = v_prev


def _rope_block_tables(nblocks, blk):
    half = SWA_HEAD_DIM // 2
    inv_freq = ROPE_THETA ** (-(jnp.arange(half, dtype=F32) * 2.0) / SWA_HEAD_DIM)
    lane_freq = jnp.tile(inv_freq, LANES // half)[None, :]
    sign = jnp.tile(jnp.concatenate([-jnp.ones((half,), F32), jnp.ones((half,), F32)]), LANES // SWA_HEAD_DIM)[None, :]
    ang_b = (jnp.arange(nblocks, dtype=jnp.int32) * blk).astype(F32)[:, None] * lane_freq
    ang_r = jnp.arange(blk, dtype=jnp.int32).astype(F32)[:, None] * lane_freq
    cr, sr = jnp.cos(ang_r), jnp.sin(ang_r)
    return jnp.cos(ang_b), jnp.sin(ang_b), cr, sr, sign * cr, sign * sr


def _swa_prompt(proj, pa, sinks, g_q, g_k, w_pb):
    t = proj.shape[0]
    d = w_pb.shape[1]
    blk = WINDOW
    rows = SWA_PROMPT_BLOCKS_PER_STEP * blk
    nblocks, nsteps = t // blk, t // rows
    mean_mat = jnp.asarray(_head_mean_matrix(), BF16)
    tables = _rope_block_tables(nblocks, blk)
    qb, kb, vb = COL_SQ // SWA_Q, COL_SK // SWA_KV, COL_SV // SWA_KV
    whole = lambda n, s: (0, 0)
    cur = lambda n: jnp.minimum(n, nsteps - 1)
    prev = lambda n: jnp.maximum(n - 1, 0)
    gate_rows = lambda n: pl.multiple_of(prev(n) * rows, rows)
    grid_spec = pltpu.PrefetchScalarGridSpec(
        num_scalar_prefetch=1,
        grid=(nsteps + 1,),
        in_specs=[pl.BlockSpec((rows, SWA_Q), lambda n, s: (cur(n), qb)),
                  pl.BlockSpec((rows, SWA_KV), lambda n, s: (cur(n), kb)),
                  pl.BlockSpec((rows, SWA_KV), lambda n, s: (cur(n), vb)),
                  pl.BlockSpec((nblocks, LANES), whole),
                  pl.BlockSpec((nblocks, LANES), whole),
                  pl.BlockSpec((blk, LANES), whole),
                  pl.BlockSpec((blk, LANES), whole),
                  pl.BlockSpec((blk, LANES), whole),
                  pl.BlockSpec((blk, LANES), whole),
                  pl.BlockSpec((1, LANES), whole),
                  pl.BlockSpec((1, LANES), whole),
                  pl.BlockSpec((MXU_DIM, MXU_DIM), whole),
                  pl.BlockSpec((rows, d), lambda n, s: (prev(n), 0)),
                  pl.BlockSpec((pl.Element(rows), pl.Element(d)), lambda n, s: (gate_rows(n), COL_BG)),
                  pl.BlockSpec((pl.Element(rows), pl.Element(d)), lambda n, s: (gate_rows(n), COL_BG + d)),
                  pl.BlockSpec(w_pb.shape, whole, pipeline_mode=pl.Buffered(1))],
        out_specs=[pl.BlockSpec((rows, d), lambda n, s: (prev(n), 0)),
                   pl.BlockSpec((blk, SWA_KV), whole),
                   pl.BlockSpec((blk, SWA_KV), whole)],
        scratch_shapes=[pltpu.VMEM((blk, SWA_KV), F32), pltpu.VMEM((blk, SWA_KV), F32),
                        pltpu.VMEM((2, rows, SWA_Q), BF16)],
    )
    return pl.pallas_call(
        _swa_prompt_body,
        grid_spec=grid_spec,
        out_shape=[jax.ShapeDtypeStruct((t, d), BF16),
                   jax.ShapeDtypeStruct((blk, SWA_KV), F32),
                   jax.ShapeDtypeStruct((blk, SWA_KV), F32)],
        compiler_params=_params("arbitrary"),
        name="swa_prompt",
    )(sinks, proj, proj, proj, *tables, g_q, g_k, mean_mat, pa, proj, proj, w_pb)


def _swa_sample_body(sinks_ref, q_ref, k_ref, v_ref, bk_ref, bv_ref, cos_ref, sin_ref, gq_ref, gk_ref, mm_ref,
                     ob_ref, kout_ref, vout_ref):
    nseq = bk_ref.shape[0]
    rows = nseq * DEC_SEQ
    cos_t = jnp.concatenate([cos_ref[...]] * nseq, axis=0)
    sin_t = jnp.concatenate([sin_ref[...]] * nseq, axis=0)
    mean_mat = mm_ref[...]
    q_tiles = [t * SWA_SCORE_SCALE
               for t in _qk_norm_rope(q_ref[...].astype(F32), gq_ref[...], cos_t, sin_t, mean_mat)]
    kn = jnp.concatenate(_qk_norm_rope(k_ref[...].astype(F32), gk_ref[...], cos_t, sin_t, mean_mat), axis=1)
    vn = v_ref[...].astype(F32)

    lane_w = lax.broadcasted_iota(jnp.int32, (SWA_KV, WINDOW), 1)
    pad_rows = jnp.zeros((WINDOW - DEC_SEQ, SWA_KV), F32)
    for s in range(nseq):
        rsl = slice(s * DEC_SEQ, (s + 1) * DEC_SEQ)
        for buf_ref, new, out_ref in ((bk_ref, kn, kout_ref), (bv_ref, vn, vout_ref)):
            new_t = jnp.concatenate([pad_rows, new[rsl, :]], axis=0).T
            out_ref[s] = jnp.where(lane_w < WINDOW - DEC_SEQ, pltpu.roll(buf_ref[s], WINDOW - DEC_SEQ, 1), new_t)

    tok = lax.broadcasted_iota(jnp.int32, (rows, WINDOW), 0) % DEC_SEQ
    valid_c = lax.broadcasted_iota(jnp.int32, (rows, WINDOW), 1) >= tok
    r_n = lax.broadcasted_iota(jnp.int32, (rows, LANES), 0)
    c_n = lax.broadcasted_iota(jnp.int32, (rows, LANES), 1)
    valid_n = (r_n // DEC_SEQ == c_n // DEC_SEQ) & (c_n % DEC_SEQ <= r_n % DEC_SEQ)
    pad_new = jnp.zeros((LANES - rows, LANES), F32)

    pairs = SWA_GROUP // 2
    zeros_dk = jnp.zeros((SWA_HEAD_DIM, nseq * WINDOW), F32)
    zero_blk = jnp.zeros((DEC_SEQ, WINDOW), F32)
    low = lax.broadcasted_iota(jnp.int32, (rows, LANES), 1) < SWA_HEAD_DIM
    for g in range(SWA_KV_HEADS):
        dsl = slice(g * SWA_HEAD_DIM, (g + 1) * SWA_HEAD_DIM)
        kt = jnp.concatenate([bk_ref[s, dsl, :] for s in range(nseq)], axis=1)
        vt = jnp.concatenate([bv_ref[s, dsl, :] for s in range(nseq)], axis=1)
        lo = (g // 2) * LANES
        kn_tile, vn_tile = kn[:, lo:lo + LANES], vn[:, lo:lo + LANES]
        kv_low = g % 2 == 0
        q_g = jnp.concatenate(q_tiles[g * pairs:(g + 1) * pairs], axis=0).astype(BF16)
        out = None
        for parity in range(2):
            want_low = parity == 0
            stack = (lambda a: jnp.concatenate([a, zeros_dk], axis=0)) if want_low else \
                    (lambda a: jnp.concatenate([zeros_dk, a], axis=0))
            keep = low if want_low else jnp.logical_not(low)
            align = (lambda a: a) if kv_low == want_low else (lambda a: pltpu.roll(a, SWA_HEAD_DIM, 1))
            place = lambda a: jnp.concatenate([jnp.where(keep, align(a), 0.0), pad_new], axis=0).astype(BF16)
            kn_p, vn_p = place(kn_tile), place(vn_tile)
            s_c = _dot(q_g, stack(kt).astype(BF16))
            s_n = _dot_nt(q_g, kn_p)
            pc_rows, pn_rows, inv = [], [], []
            for pair in range(pairs):
                sink = sinks_ref[g * SWA_GROUP + 2 * pair + parity] * LOG2_E
                psl = slice(pair * rows, (pair + 1) * rows)
                sc = jnp.concatenate(
                    [s_c[pair * rows + s * DEC_SEQ: pair * rows + (s + 1) * DEC_SEQ, s * WINDOW:(s + 1) * WINDOW]
                     for s in range(nseq)], axis=0)
                sc = jnp.where(valid_c, sc, NEG_INF)
                sn = jnp.where(valid_n, s_n[psl, :], NEG_INF)
                m = jnp.maximum(jnp.maximum(jnp.max(sc, axis=-1, keepdims=True),
                                            jnp.max(sn, axis=-1, keepdims=True)), sink)
                pc = jnp.exp2(sc - m)
                pn = jnp.exp2(sn - m)
                inv.append(1.0 / (jnp.sum(pc, axis=-1, keepdims=True) + jnp.sum(pn, axis=-1, keepdims=True)
                                  + jnp.exp2(sink - m)))
                pn_rows.append(pn)
                for s in range(nseq):
                    blocks = [pc[s * DEC_SEQ:(s + 1) * DEC_SEQ, :] if s2 == s else zero_blk for s2 in range(nseq)]
                    pc_rows.append(jnp.concatenate(blocks, axis=1))
            pv = (_dot_nt(jnp.concatenate(pc_rows, axis=0).astype(BF16), stack(vt).astype(BF16))
                  + _dot(jnp.concatenate(pn_rows, axis=0).astype(BF16), vn_p))
            pv = jnp.concatenate([pv[pair * rows:(pair + 1) * rows, :] * inv[pair] for pair in range(pairs)], axis=0)
            out = pv if out is None else out + pv
        for pair in range(pairs):
            tile = g * pairs + pair
            ob_ref[:, tile * LANES:(tile + 1) * LANES] = out[pair * rows:(pair + 1) * rows, :].astype(BF16)


def _swa_sample(proj, buf_k, buf_v, sinks, cos_t, sin_t, g_q, g_k):
    t = proj.shape[0]
    nseq = SWA_SAMPLE_BATCH
    rows = nseq * DEC_SEQ
    mean_mat = jnp.asarray(_head_mean_matrix(), BF16)
    qb, kb, vb = COL_SQ // SWA_Q, COL_SK // SWA_KV, COL_SV // SWA_KV
    grid_spec = pltpu.PrefetchScalarGridSpec(
        num_scalar_prefetch=1,
        grid=(t // rows,),
        in_specs=[pl.BlockSpec((rows, SWA_Q), lambda n, s: (n, qb)),
                  pl.BlockSpec((rows, SWA_KV), lambda n, s: (n, kb)),
                  pl.BlockSpec((rows, SWA_KV), lambda n, s: (n, vb)),
                  pl.BlockSpec((nseq, SWA_KV, WINDOW), lambda n, s: (n, 0, 0)),
                  pl.BlockSpec((nseq, SWA_KV, WINDOW), lambda n, s: (n, 0, 0)),
                  pl.BlockSpec((DEC_SEQ, LANES), lambda n, s: (0, 0)),
                  pl.BlockSpec((DEC_SEQ, LANES), lambda n, s: (0, 0)),
                  pl.BlockSpec((1, LANES), lambda n, s: (0, 0)),
                  pl.BlockSpec((1, LANES), lambda n, s: (0, 0)),
                  pl.BlockSpec((MXU_DIM, MXU_DIM), lambda n, s: (0, 0))],
        out_specs=[pl.BlockSpec((rows, SWA_Q), lambda n, s: (n, 0)),
                   pl.BlockSpec((nseq, SWA_KV, WINDOW), lambda n, s: (n, 0, 0)),
                   pl.BlockSpec((nseq, SWA_KV, WINDOW), lambda n, s: (n, 0, 0))],
    )
    return pl.pallas_call(
        _swa_sample_body,
        grid_spec=grid_spec,
        out_shape=[jax.ShapeDtypeStruct((t, SWA_Q), BF16),
                   jax.ShapeDtypeStruct(buf_k.shape, F32),
                   jax.ShapeDtypeStruct(buf_v.shape, F32)],
        compiler_params=_params("arbitrary"),
        name="swa_sample",
    )(sinks, proj, proj, proj, buf_k, buf_v, cos_t, sin_t, g_q, g_k, mean_mat)


def _merge_body(oa_ref, ob_ref, wpa_ref, wpb_ref, ga_ref, gb_ref, m_ref):
    ga = _sigmoid(ga_ref[...].astype(F32))
    gb = _sigmoid(gb_ref[...].astype(F32))
    m_ref[...] = (ga * _dot(oa_ref[...], wpa_ref[...]) + gb * _dot(ob_ref[...], wpb_ref[...])).astype(BF16)


def _merge(oa, ob, proj, w_pa, w_pb, tm=1024, tn=512):
    t, d = oa.shape
    ga_blk, gb_blk = COL_BG // tn, (COL_BG + D_MODEL) // tn
    return pl.pallas_call(
        _merge_body,
        grid=(t // tm, d // tn),
        in_specs=[pl.BlockSpec((tm, d), lambda i, j: (i, 0)),
                  pl.BlockSpec((tm, d), lambda i, j: (i, 0)),
                  pl.BlockSpec((d, tn), lambda i, j: (0, j)),
                  pl.BlockSpec((d, tn), lambda i, j: (0, j)),
                  pl.BlockSpec((tm, tn), lambda i, j: (i, ga_blk + j)),
                  pl.BlockSpec((tm, tn), lambda i, j: (i, gb_blk + j))],
        out_specs=pl.BlockSpec((tm, tn), lambda i, j: (i, j)),
        out_shape=jax.ShapeDtypeStruct((t, d), BF16),
        compiler_params=_params("arbitrary", "arbitrary"),
        name="merge",
    )(oa, ob, w_pa, w_pb, proj, proj)


OPROJ_ROW_CHUNK = 256


def _oproj_body(m_ref, w_ref, x_ref, gate_ref, gn_ref, sc_ref, sh_ref, x1_ref, h2_ref):
    for row0 in range(0, x_ref.shape[0], OPROJ_ROW_CHUNK):
        sl = slice(row0, row0 + OPROJ_ROW_CHUNK)
        y = _dot(m_ref[sl, :], w_ref[...])
        x1 = x_ref[sl, :] + _mod_rows(gate_ref, row0, OPROJ_ROW_CHUNK) * y
        x1_ref[sl, :] = x1
        ms = jnp.mean(x1 * x1, axis=-1, keepdims=True)
        h = (x1 * lax.rsqrt(ms + EPS)) * gn_ref[...]
        h2_ref[sl, :] = (h * (1.0 + _mod_rows(sc_ref, row0, OPROJ_ROW_CHUNK))
                         + _mod_rows(sh_ref, row0, OPROJ_ROW_CHUNK)).astype(BF16)


def _oproj(merged, w_o, x, gate, gn, scale, shift, tm=512):
    t, d = x.shape
    rows = lambda i: (i, 0)
    whole = lambda i: (0, 0)
    mod = lambda m: (pl.BlockSpec((1, d), whole) if m.shape[0] == 1
                     else pl.BlockSpec((tm // DEC_SEQ, d), rows))
    return pl.pallas_call(
        _oproj_body,
        grid=(t // tm,),
        in_specs=[pl.BlockSpec((tm, d), rows),
                  pl.BlockSpec(w_o.shape, whole, pipeline_mode=pl.Buffered(1)),
                  pl.BlockSpec((tm, d), rows),
                  mod(gate),
                  pl.BlockSpec((1, d), whole),
                  mod(scale),
                  mod(shift)],
        out_specs=[pl.BlockSpec((tm, d), rows), pl.BlockSpec((tm, d), rows)],
        out_shape=[jax.ShapeDtypeStruct(x.shape, F32), jax.ShapeDtypeStruct(x.shape, BF16)],
        compiler_params=_params("arbitrary"),
        name="oproj",
    )(merged, w_o, x, gate, gn, scale, shift)


def _ffn_body(h_ref, x_ref, gate_ref, wg_ref, wu_ref, wd_ref, o_ref):
    f = pl.program_id(1)

    @pl.when(f == 0)
    def _():
        def rows(row0):
            o_ref[pl.ds(row0, ROW_CHUNK), :] = jnp.zeros((ROW_CHUNK, o_ref.shape[1]), F32)
        _row_loop(x_ref.shape[0], rows)

    h = h_ref[...]
    a = _silu(_dot(h, wg_ref[...])) * _dot(h, wu_ref[...])
    o_ref[...] += _dot(a.astype(BF16), wd_ref[...])

    @pl.when(f == pl.num_programs(1) - 1)
    def _():
        def rows(row0):
            sl = pl.ds(row0, ROW_CHUNK)
            o_ref[sl, :] = x_ref[sl, :] + _mod_rows(gate_ref, row0, ROW_CHUNK) * o_ref[sl, :]
        _row_loop(x_ref.shape[0], rows)


def _ffn(h, x, gate, w_gate, w_up, w_down, tm=1024, tf=512):
    t, d = x.shape
    dff = w_gate.shape[1]
    zero = lambda j: 0
    return pl.pallas_call(
        _ffn_body,
        grid=(t // tm, dff // tf),
        in_specs=[pl.BlockSpec((tm, d), lambda i, f: (i, 0)),
                  pl.BlockSpec((tm, d), lambda i, f: (i, 0)),
                  _mod_spec(gate, tm, d, zero),
                  pl.BlockSpec((d, tf), lambda i, f: (0, f)),
                  pl.BlockSpec((d, tf), lambda i, f: (0, f)),
                  pl.BlockSpec((tf, d), lambda i, f: (f, 0))],
        out_specs=pl.BlockSpec((tm, d), lambda i, f: (i, 0)),
        out_shape=jax.ShapeDtypeStruct(x.shape, F32),
        compiler_params=_params("arbitrary", "arbitrary"),
        name="ffn",
    )(h, x, gate, w_gate, w_up, w_down)


def _mods(ada):
    return tuple(ada[:, i * D_MODEL:(i + 1) * D_MODEL] for i in range(6))


def _layer_after_inproj(x, proj, glog, mods, pos, state, buf_k, buf_v, wts):
    g_q, g_k, sinks, g_out, w_pa, w_pb, w_o, gn_ffn, w_fg, w_fu, w_fd = wts
    bsz, tlen, d = x.shape
    x2 = x.reshape(bsz * tlen, d)
    _, _, gate_m, shift_f, scale_f, gate_f = mods
    if state is None:
        pa, s_new = _gla_prompt(proj, glog, g_out, w_pa)
        s_new = s_new[None]
        merged, k_new, v_new = _swa_prompt(proj, pa, sinks, g_q, g_k, w_pb)
        k_new = k_new.reshape(bsz, WINDOW, SWA_KV_HEADS, SWA_HEAD_DIM)
        v_new = v_new.reshape(bsz, WINDOW, SWA_KV_HEADS, SWA_HEAD_DIM)
    else:
        to_dev = lambda c: jnp.transpose(c, (0, 2, 3, 1)).reshape(bsz, SWA_KV, WINDOW)
        from_dev = lambda c: jnp.transpose(c.reshape(bsz, SWA_KV_HEADS, SWA_HEAD_DIM, WINDOW), (0, 3, 1, 2))
        cos_t, sin_t = _rope_tables(pos)
        oa, s_new = _gla_sample(proj, glog, state, g_out)
        ob, k_new, v_new = _swa_sample(proj, to_dev(buf_k), to_dev(buf_v), sinks, cos_t, sin_t, g_q, g_k)
        k_new, v_new = from_dev(k_new), from_dev(v_new)
        merged = _merge(oa, ob, proj, w_pa, w_pb)
    x1, h2 = _oproj(merged, w_o, x2, gate_m, gn_ffn, scale_f, shift_f)
    y = _ffn(h2, x1, gate_f, w_fg, w_fu, w_fd)
    return y.reshape(bsz, tlen, d), s_new, k_new, v_new


def kernel(x_prompt, x_sample, c_prompt, c_sample, state_gla, cache_swa_k, cache_swa_v, w_ada, b_ada, g_norm_mix, w_in, w_gk2, b_gk2, g_qnorm, g_knorm, sinks, g_gla_out, w_pa, w_pb, w_o, g_norm_ffn, w_ffn_gate, w_ffn_up, w_ffn_down):
    assert w_in.shape[0] == 1, "single trunk layer"
    pos_p = jnp.arange(SEQ, dtype=jnp.int32)
    pos_s = PAST_LEN + jnp.arange(DEC_SEQ, dtype=jnp.int32)
    l = 0
    n_c = 1 + DEC_BATCH
    pad = (-n_c) % (2 * SUBLANES)
    c_all = jnp.pad(jnp.concatenate([c_prompt, c_sample], axis=0), ((0, pad), (0, 0)))
    ada, wt = _ada(c_all, w_ada[l], b_ada[l].reshape(1, -1), jnp.transpose(w_in[l]))
    mods_p, mods_s = _mods(ada[0:1]), _mods(ada[1:n_c])

    wt_lr = jnp.pad(wt[W_IN_GLR:W_IN_GLR + GLA_RANK], ((0, LANES - GLA_RANK), (0, 0)))
    w_gk = jnp.pad(w_gk2[l], ((0, LANES - GLA_RANK), (0, 0))).astype(BF16)
    in_wts = (wt, wt_lr, w_gk, b_gk2[l].reshape(1, GLA_KEY))
    gn_mix = g_norm_mix[l].reshape(1, D_MODEL)

    later_weights = (w_pa[l], w_pb[l], w_o[l], w_ffn_gate[l], w_ffn_up[l], w_ffn_down[l])
    proj_p, glog_p, *later_bf16 = _inproj(x_prompt.reshape(-1, D_MODEL), gn_mix, mods_p[1], mods_p[0], *in_wts,
                                          cast_weights=later_weights)
    proj_s, glog_s = _inproj(x_sample.reshape(-1, D_MODEL), gn_mix, mods_s[1], mods_s[0], *in_wts)
    w_pa_b, w_pb_b, w_o_b, w_fg_b, w_fu_b, w_fd_b = later_bf16
    wts = (jnp.tile(g_qnorm[l], 2).reshape(1, LANES), jnp.tile(g_knorm[l], 2).reshape(1, LANES),
           sinks[l], g_gla_out[l].reshape(1, GLA_DV), w_pa_b, w_pb_b, w_o_b,
           g_norm_ffn[l].reshape(1, D_MODEL), w_fg_b, w_fu_b, w_fd_b)

    yp, sp, kp, vp = _layer_after_inproj(x_prompt, proj_p, glog_p, mods_p, pos_p, None, None, None, wts)
    ys, ss, ks, vs = _layer_after_inproj(x_sample, proj_s, glog_s, mods_s, pos_s, state_gla[l], cache_swa_k[l],
                                         cache_swa_v[l], wts)
    return (yp, ys, sp[None], kp[None], vp[None], ss[None], ks[None], vs[None])
```

```python
import functools

import numpy as np
import jax
import jax.numpy as jnp
from jax import lax
from jax.experimental import pallas as pl
from jax.experimental.pallas import tpu as pltpu

F32 = jnp.float32
BF16 = jnp.bfloat16

D_MODEL = 2048
SEQ = 16384
DEC_BATCH = 128
DEC_SEQ = 8
PAST_LEN = 16384
GLA_HEADS = 4
GLA_DK = 256
GLA_DV = 512
GLA_KEY = GLA_HEADS * GLA_DK
GLA_VAL = GLA_HEADS * GLA_DV
GLA_RANK = 16
GLA_TAU = 16.0
SWA_HEAD_DIM = 64
SWA_HEADS = 32
SWA_KV_HEADS = 4
SWA_GROUP = SWA_HEADS // SWA_KV_HEADS
SWA_Q = SWA_HEADS * SWA_HEAD_DIM
SWA_KV = SWA_KV_HEADS * SWA_HEAD_DIM
WINDOW = 128
ROPE_THETA = 10000.0
D_FF = 5632
EPS = 1e-6
NEG_INF = -1e30
LOG2_E = 1.4426950408889634
SWA_SCORE_SCALE = (SWA_HEAD_DIM ** -0.5) * LOG2_E

SUBLANES = 8
LANES = 128
MXU_DIM = 256
VMEM_LIMIT_BYTES = 58 * 1024 * 1024

COL_GQ = 0
COL_GK = COL_GQ + GLA_KEY
COL_GV = COL_GK + GLA_KEY
COL_GR = COL_GV + GLA_VAL
COL_SQ = COL_GR + GLA_VAL
COL_SK = COL_SQ + SWA_Q
COL_SV = COL_SK + SWA_KV
COL_BG = COL_SV + SWA_KV
PROJ_COLS = COL_BG + 2 * D_MODEL
W_IN_GLR = COL_GR + GLA_VAL

GLA_CHUNK = 256
GLA_ONE_SIDED_MAX_DECAY = 80.0
GLA_SAMPLE_BATCH = 16
SWA_SAMPLE_BATCH = 8
SWA_PROMPT_BLOCKS_PER_STEP = 1
SWA_PAIRS_PER_CHAIN = 4
ROW_CHUNK = 512


def _dot(a, b):
    return jnp.dot(a, b, preferred_element_type=F32)


def _dot_nt(a, b):
    return lax.dot_general(a, b, (((1,), (1,)), ((), ())), preferred_element_type=F32)


def _dot_tn(a, b):
    return lax.dot_general(a, b, (((0,), (0,)), ((), ())), preferred_element_type=F32)


def _sigmoid(x):
    return 1.0 / (1.0 + jnp.exp(-x))


def _silu(x):
    return x * _sigmoid(x)


def _split3(x):
    hi = x.astype(BF16)
    r1 = x - hi.astype(F32)
    mid = r1.astype(BF16)
    lo = (r1 - mid.astype(F32)).astype(BF16)
    return hi, mid, lo


def _params(*sem):
    return pltpu.CompilerParams(dimension_semantics=sem, vmem_limit_bytes=VMEM_LIMIT_BYTES)


def _ada_body(c_ref, w_ref, b_ref, wt_ref, o_ref, wtb_ref):
    a = _silu(c_ref[...]).astype(BF16)
    o_ref[...] = _dot(a, w_ref[...].astype(BF16)) + b_ref[...]
    wtb_ref[...] = wt_ref[...].astype(BF16)


def _ada(c_all, w_ada, b_ada, wt, tn=512):
    m, d = c_all.shape
    n = w_ada.shape[1]
    steps = n // tn
    rows = wt.shape[0]
    align = 2 * SUBLANES
    slab = -(-rows // (steps * align)) * align
    assert (steps - 1) * slab < rows <= steps * slab
    slab_spec = pl.BlockSpec((slab, wt.shape[1]), lambda j: (j, 0))
    return pl.pallas_call(
        _ada_body,
        grid=(steps,),
        in_specs=[pl.BlockSpec((m, d), lambda j: (0, 0)),
                  pl.BlockSpec((d, tn), lambda j: (0, j)),
                  pl.BlockSpec((1, tn), lambda j: (0, j)),
                  slab_spec],
        out_specs=[pl.BlockSpec((m, tn), lambda j: (0, j)), slab_spec],
        out_shape=[jax.ShapeDtypeStruct((m, n), F32), jax.ShapeDtypeStruct(wt.shape, BF16)],
        compiler_params=_params("arbitrary"),
        name="ada",
    )(c_all, w_ada, b_ada, wt)


def _mod_spec(mod, tm, width, col):
    if mod.shape[0] == 1:
        return pl.BlockSpec((1, width), lambda i, j: (0, col(j)))
    return pl.BlockSpec((tm // DEC_SEQ, width), lambda i, j: (i, col(j)))


def _mod_rows(ref, row0, nrows):
    if ref.shape[0] == 1:
        return ref[...]
    seq0 = row0 // DEC_SEQ
    return jnp.concatenate(
        [jnp.broadcast_to(ref[pl.ds(seq0 + s, 1), :], (DEC_SEQ, ref.shape[1])) for s in range(nrows // DEC_SEQ)],
        axis=0)


def _modulated_norm_rows(x, gn_ref, sc_ref, sh_ref, row0, nrows):
    ms = jnp.mean(x * x, axis=-1, keepdims=True)
    y = (x * lax.rsqrt(ms + EPS)) * gn_ref[...]
    return y * (1.0 + _mod_rows(sc_ref, row0, nrows)) + _mod_rows(sh_ref, row0, nrows)


def _row_loop(total_rows, body):
    assert total_rows % ROW_CHUNK == 0, (total_rows, ROW_CHUNK)

    def step(r, carry):
        body(pl.multiple_of(r * ROW_CHUNK, ROW_CHUNK))
        return carry
    lax.fori_loop(0, total_rows // ROW_CHUNK, step, 0)


def _inproj_body(x_ref, gn_ref, sc_ref, sh_ref, wt_ref, wlr_ref, wgk_ref, bgk_ref, *rest, n_casts):
    cast_src = rest[:n_casts]
    proj_ref, glog_ref = rest[n_casts:n_casts + 2]
    cast_dst = rest[n_casts + 2:2 * n_casts + 2]
    h_scr, x_buf, x_sem = rest[-3:]
    for src, dst in zip(cast_src, cast_dst):
        dst[...] = src[...].astype(BF16)
    tm = h_scr.shape[0]
    n_chunks = tm // ROW_CHUNK
    tile_row0 = pl.program_id(0) * tm

    def x_copy(chunk):
        row = pl.multiple_of(tile_row0 + chunk * ROW_CHUNK, ROW_CHUNK)
        return pltpu.make_async_copy(x_ref.at[pl.ds(row, ROW_CHUNK), :], x_buf.at[chunk % 2], x_sem.at[chunk % 2])

    @pl.when(pl.program_id(1) == 0)
    def _():
        x_copy(0).start()
        for chunk in range(n_chunks):
            if chunk + 1 < n_chunks:
                x_copy(chunk + 1).start()
            x_copy(chunk).wait()
            row0 = chunk * ROW_CHUNK
            hb = _modulated_norm_rows(x_buf[chunk % 2], gn_ref, sc_ref, sh_ref, row0, ROW_CHUNK).astype(BF16)
            h_scr[pl.ds(row0, ROW_CHUNK), :] = hb
            glr = _dot_nt(hb, wlr_ref[...])
            z = _dot(glr.astype(BF16), wgk_ref[...]) + bgk_ref[...]
            log_sig = jnp.minimum(z, 0.0) - jnp.log1p(jnp.exp(-jnp.abs(z)))
            glog_ref[pl.ds(row0, ROW_CHUNK), :] = log_sig * (1.0 / GLA_TAU)

    proj_ref[...] = _dot_nt(h_scr[...], wt_ref[...]).astype(BF16)


def _cast_block_spec(shape, row_tiles, col_steps):
    rows, cols = shape
    br = rows // row_tiles
    assert br * row_tiles == rows and br % (2 * SUBLANES) == 0
    ncb = max(c for c in range(1, col_steps + 1) if cols % c == 0 and (cols // c) % LANES == 0)
    return pl.BlockSpec((br, cols // ncb), lambda i, j: (i, jnp.minimum(j, ncb - 1)))


def _inproj(x, gn, scale, shift, wt, wt_lr, w_gk, b_gk, cast_weights=(), tm=2048, tn=512):
    t, d = x.shape
    tm = min(tm, t)
    assert tm % ROW_CHUNK == 0 and t % tm == 0, (t, tm)
    n = wt.shape[0] - GLA_RANK
    n_first = W_IN_GLR // tn
    grid = (t // tm, n // tn)
    zero = lambda j: 0
    wt_rows = lambda i, j: (pl.multiple_of(j * tn + jnp.where(j >= n_first, GLA_RANK, 0), GLA_RANK), 0)
    cast_specs = [_cast_block_spec(w.shape, *grid) for w in cast_weights]
    return pl.pallas_call(
        functools.partial(_inproj_body, n_casts=len(cast_weights)),
        grid=grid,
        in_specs=[pl.BlockSpec(memory_space=pl.ANY),
                  pl.BlockSpec((1, d), lambda i, j: (0, 0)),
                  _mod_spec(scale, tm, d, zero),
                  _mod_spec(shift, tm, d, zero),
                  pl.BlockSpec((pl.Element(tn), pl.Element(d)), wt_rows),
                  pl.BlockSpec(wt_lr.shape, lambda i, j: (0, 0)),
                  pl.BlockSpec(w_gk.shape, lambda i, j: (0, 0)),
                  pl.BlockSpec(b_gk.shape, lambda i, j: (0, 0))] + cast_specs,
        out_specs=[pl.BlockSpec((tm, tn), lambda i, j: (i, j)),
                   pl.BlockSpec((tm, GLA_KEY), lambda i, j: (i, 0))] + cast_specs,
        out_shape=[jax.ShapeDtypeStruct((t, n), BF16),
                   jax.ShapeDtypeStruct((t, GLA_KEY), F32)]
                  + [jax.ShapeDtypeStruct(w.shape, BF16) for w in cast_weights],
        scratch_shapes=[pltpu.VMEM((tm, d), BF16), pltpu.VMEM((2, ROW_CHUNK, d), F32),
                        pltpu.SemaphoreType.DMA((2,))],
        compiler_params=_params("arbitrary", "arbitrary"),
        name="inproj",
    )(x, gn, scale, shift, wt, wt_lr, w_gk, b_gk, *cast_weights)


def _gla_level_table(c, group):
    n_levels = int(np.log2(group))
    i = np.arange(c)[:, None]
    j = np.arange(c)[None, :]
    x = np.bitwise_xor(i, j)
    lvl = np.floor(np.log2(np.maximum(x, 1))).astype(np.int32)
    lvl = np.where(i == j, n_levels, lvl)
    valid = (i >= j) & (i // group == j // group)
    return np.where(valid, lvl, -1).astype(np.int32), n_levels


def _gla_tril(c, group):
    i = np.arange(c)[:, None]
    j = np.arange(c)[None, :]
    return ((i >= j) & (i // group == j // group)).astype(np.float32)


def _level_reference(b_scr, rows, level):
    s = 2 ** (level + 1)
    width = b_scr.shape[1]
    pieces = []
    if s >= SUBLANES:
        for blk in range(rows // s):
            mid = blk * s + s // 2
            pieces.append(jnp.broadcast_to(b_scr[mid:mid + 1, :], (s, width)))
    else:
        p = lax.broadcasted_iota(jnp.int32, (SUBLANES, width), 0)
        for tile in range(rows // SUBLANES):
            base = tile * SUBLANES
            mids = [base + q * s + s // 2 for q in range(SUBLANES // s)]
            r = jnp.broadcast_to(b_scr[mids[-1]:mids[-1] + 1, :], (SUBLANES, width))
            for q in range(SUBLANES // s - 2, -1, -1):
                row = jnp.broadcast_to(b_scr[mids[q]:mids[q] + 1, :], (SUBLANES, width))
                r = jnp.where(p < (q + 1) * s, row, r)
            pieces.append(r)
    return jnp.concatenate(pieces, axis=0) if len(pieces) > 1 else pieces[0]


def _gla_intra(q, k, k_bf, b, b_scr, lvl, n_levels):
    rows = q.shape[0]
    a = jnp.where(lvl == n_levels, _dot_nt(q.astype(BF16), k_bf), 0.0)
    for level in range(n_levels):
        f = jnp.exp(-jnp.abs(b - _level_reference(b_scr, rows, level)))
        p = _dot_nt((q * f).astype(BF16), (k * f).astype(BF16))
        a = jnp.where(lvl == level, p, a)
    return a


def _gla_cumsum(g, tril_bf):
    hi, mid, lo = _split3(g)
    return _dot(tril_bf, hi) + _dot(tril_bf, mid) + _dot(tril_bf, lo)


def _gla_out_gate(o, gout, gr):
    ms = jnp.mean(o * o, axis=-1, keepdims=True)
    y = (o * lax.rsqrt(ms + EPS)) * gout
    return (y * _silu(gr.astype(F32))).astype(BF16)


def _gla_prompt_body(q_ref, k_ref, v_ref, gr_ref, g_ref, tril_ref, lvl_ref, gout_ref, wpa_ref,
                     pa_ref, st_ref, st_scr, b_scr, a_scr, oa_scr, *, n_levels, heads):
    c = pl.program_id(0)
    n_chunks = pl.num_programs(0) - 1
    rows = q_ref.shape[0]
    cur = lax.rem(c, 2)
    prev = 1 - cur
    half = pa_ref.shape[1] // 2

    @pl.when(c == 0)
    def _():
        st_scr[...] = jnp.zeros_like(st_scr)
        oa_scr[...] = jnp.zeros_like(oa_scr)

    tril, lvl = tril_ref[...], lvl_ref[...]
    per_head = []
    mild = None
    for h in range(heads):
        ksl = slice(h * GLA_DK, (h + 1) * GLA_DK)
        b = _gla_cumsum(g_ref[:, ksl], tril)
        b_scr[h] = b
        b_end = b_scr[h, rows - 1:rows, :]
        q = q_ref[:, ksl].astype(F32) * (GLA_DK ** -0.5)
        k_bf = k_ref[:, ksl]
        k = k_bf.astype(F32)
        st = st_scr[h]
        qd = (q * jnp.exp(b)).astype(BF16)
        o_inter = _dot_nt(qd, st.astype(BF16))
        head_mild = jnp.min(b_end) > -GLA_ONE_SIDED_MAX_DECAY
        mild = head_mild if mild is None else jnp.logical_and(mild, head_mild)
        per_head.append((b, b_end, q, k, k_bf, st, qd, o_inter))

    pa_ref[:, :half] = _dot(oa_scr[prev], wpa_ref[:, :half])

    @pl.when(mild)
    def _():
        for h, (b, _, _, k, _, _, qd, _) in enumerate(per_head):
            a_scr[h] = jnp.where(lvl >= 0, _dot_nt(qd, (k * jnp.exp(-b)).astype(BF16)), 0.0)

    @pl.when(jnp.logical_not(mild))
    def _():
        for h, (b, _, q, k, k_bf, _, _, _) in enumerate(per_head):
            a_scr[h] = _gla_intra(q, k, k_bf, b, b_scr.at[h], lvl, n_levels)

    outs = []
    for h, (b, b_end, _, k, _, st, _, o_inter) in enumerate(per_head):
        v = v_ref[:, h * GLA_DV:(h + 1) * GLA_DV]
        outs.append(o_inter + _dot(a_scr[h].astype(BF16), v))
        k_dec = (k * jnp.exp(b_end - b)).astype(BF16)
        st_scr[h] = st * jnp.exp(b_end) + _dot_tn(v, k_dec)
    pa_ref[:, half:] = _dot(oa_scr[prev], wpa_ref[:, half:])
    for h, o in enumerate(outs):
        vsl = slice(h * GLA_DV, (h + 1) * GLA_DV)
        oa_scr[cur, :, vsl] = _gla_out_gate(o, gout_ref[...], gr_ref[:, vsl])

    @pl.when(c == n_chunks - 1)
    def _():
        for h in range(heads):
            st_ref[h] = st_scr[h].T


def _gla_prompt(proj, glog, g_out, w_pa):
    t = proj.shape[0]
    c = GLA_CHUNK
    n_chunks = t // c
    heads = GLA_HEADS
    lvl, n_levels = _gla_level_table(c, c)
    tril = jnp.asarray(_gla_tril(c, c), BF16)
    lvl = jnp.asarray(lvl)
    qb, kb, vb, rb = COL_GQ // GLA_KEY, COL_GK // GLA_KEY, COL_GV // GLA_VAL, COL_GR // GLA_VAL
    chunk = lambda i: jnp.minimum(i, n_chunks - 1)
    whole = lambda i: (0, 0)
    return pl.pallas_call(
        functools.partial(_gla_prompt_body, n_levels=n_levels, heads=heads),
        grid=(n_chunks + 1,),
        in_specs=[pl.BlockSpec((c, GLA_KEY), lambda i: (chunk(i), qb)),
                  pl.BlockSpec((c, GLA_KEY), lambda i: (chunk(i), kb)),
                  pl.BlockSpec((c, GLA_VAL), lambda i: (chunk(i), vb)),
                  pl.BlockSpec((c, GLA_VAL), lambda i: (chunk(i), rb)),
                  pl.BlockSpec((c, GLA_KEY), lambda i: (chunk(i), 0)),
                  pl.BlockSpec((c, c), whole),
                  pl.BlockSpec((c, c), whole),
                  pl.BlockSpec((1, GLA_DV), whole),
                  pl.BlockSpec(w_pa.shape, whole, pipeline_mode=pl.Buffered(1))],
        out_specs=[pl.BlockSpec((c, w_pa.shape[1]), lambda i: (jnp.maximum(i - 1, 0), 0)),
                   pl.BlockSpec((heads, GLA_DK, GLA_DV), lambda i: (0, 0, 0))],
        out_shape=[jax.ShapeDtypeStruct((t, w_pa.shape[1]), F32),
                   jax.ShapeDtypeStruct((heads, GLA_DK, GLA_DV), F32)],
        scratch_shapes=[pltpu.VMEM((heads, GLA_DV, GLA_DK), F32), pltpu.VMEM((heads, c, GLA_DK), F32),
                        pltpu.VMEM((heads, c, c), F32), pltpu.VMEM((2, c, GLA_VAL), BF16)],
        compiler_params=_params("arbitrary"),
        name="gla_prompt",
    )(proj, proj, proj, proj, glog, tril, lvl, g_out, w_pa)


def _gla_sample_body(q_ref, k_ref, v_ref, gr_ref, g_ref, s0_ref, tril_ref, lvl_ref, gout_ref,
                     oa_ref, s1_ref, b_scr, *, n_levels):
    rows = q_ref.shape[0]
    nseq = rows // DEC_SEQ
    b = _gla_cumsum(g_ref[...], tril_ref[...])
    b_scr[...] = b
    b_end = jnp.concatenate(
        [jnp.broadcast_to(b_scr[(s + 1) * DEC_SEQ - 1:(s + 1) * DEC_SEQ, :], (DEC_SEQ, GLA_DK))
         for s in range(nseq)], axis=0)
    q = q_ref[...].astype(F32) * (GLA_DK ** -0.5)
    k_bf = k_ref[...]
    k = k_bf.astype(F32)
    v = v_ref[...]
    qd = (q * jnp.exp(b)).astype(BF16)
    a = _gla_intra(q, k, k_bf, b, b_scr, lvl_ref[...], n_levels)
    o_intra = _dot(a.astype(BF16), v)

    stacked = jnp.concatenate([k * jnp.exp(b_end - b), jnp.exp(b_end)], axis=0)
    stacked_t = stacked.T
    kd_t = stacked_t[:, :rows]
    lane = lax.broadcasted_iota(jnp.int32, kd_t.shape, 1)
    outs = []
    for s in range(nseq):
        s0 = s0_ref[s, 0]
        outs.append(_dot(qd[s * DEC_SEQ:(s + 1) * DEC_SEQ, :], s0.astype(BF16)))
        in_seq = (lane >= s * DEC_SEQ) & (lane < (s + 1) * DEC_SEQ)
        kd_s = jnp.where(in_seq, kd_t, 0.0).astype(BF16)
        col = rows + s * DEC_SEQ
        decay = stacked_t[:, col:col + 1]
        s1_ref[s, 0] = s0 * decay + _dot(kd_s, v)
    o = jnp.concatenate(outs, axis=0) + o_intra
    oa_ref[...] = _gla_out_gate(o, gout_ref[...], gr_ref[...])


def _gla_sample(proj, glog, state, g_out):
    t = proj.shape[0]
    nseq = GLA_SAMPLE_BATCH
    rows = nseq * DEC_SEQ
    lvl, n_levels = _gla_level_table(rows, DEC_SEQ)
    tril = jnp.asarray(_gla_tril(rows, DEC_SEQ), BF16)
    lvl = jnp.asarray(lvl)
    qb, kb, vb, rb = COL_GQ // GLA_DK, COL_GK // GLA_DK, COL_GV // GLA_DV, COL_GR // GLA_DV
    return pl.pallas_call(
        functools.partial(_gla_sample_body, n_levels=n_levels),
        grid=(t // rows, GLA_HEADS),
        in_specs=[pl.BlockSpec((rows, GLA_DK), lambda i, h: (i, qb + h)),
                  pl.BlockSpec((rows, GLA_DK), lambda i, h: (i, kb + h)),
                  pl.BlockSpec((rows, GLA_DV), lambda i, h: (i, vb + h)),
                  pl.BlockSpec((rows, GLA_DV), lambda i, h: (i, rb + h)),
                  pl.BlockSpec((rows, GLA_DK), lambda i, h: (i, h)),
                  pl.BlockSpec((nseq, 1, GLA_DK, GLA_DV), lambda i, h: (i, h, 0, 0)),
                  pl.BlockSpec((rows, rows), lambda i, h: (0, 0)),
                  pl.BlockSpec((rows, rows), lambda i, h: (0, 0)),
                  pl.BlockSpec((1, GLA_DV), lambda i, h: (0, 0))],
        out_specs=[pl.BlockSpec((rows, GLA_DV), lambda i, h: (i, h)),
                   pl.BlockSpec((nseq, 1, GLA_DK, GLA_DV), lambda i, h: (i, h, 0, 0))],
        out_shape=[jax.ShapeDtypeStruct((t, GLA_VAL), BF16),
                   jax.ShapeDtypeStruct(state.shape, F32)],
        scratch_shapes=[pltpu.VMEM((rows, GLA_DK), F32)],
        compiler_params=_params("arbitrary", "arbitrary"),
        name="gla_sample",
    )(proj, proj, proj, proj, glog, state, tril, lvl, g_out)


def _rope_tables(pos):
    half = SWA_HEAD_DIM // 2
    inv_freq = ROPE_THETA ** (-(jnp.arange(half, dtype=F32) * 2.0) / SWA_HEAD_DIM)
    ang = pos.astype(F32)[:, None] * inv_freq[None, :]
    cos, sin = jnp.cos(ang), jnp.sin(ang)
    cos_t = jnp.concatenate([cos, cos, cos, cos], axis=-1)
    sin_t = jnp.concatenate([-sin, sin, -sin, sin], axis=-1)
    return cos_t, sin_t


def _head_mean_matrix():
    i = np.arange(MXU_DIM)[:, None] // SWA_HEAD_DIM
    j = np.arange(MXU_DIM)[None, :] // SWA_HEAD_DIM
    return (i == j).astype(np.float32) / SWA_HEAD_DIM


def _qk_norm_rope(x, gain, cos_t, sin_t, mean_mat):
    return _norm_rope_tiles(x, _head_mean_squares(x, mean_mat), gain, cos_t, sin_t)


def _head_mean_squares(x, mean_mat):
    sq = (x * x).astype(BF16)
    return [_dot(sq[:, c * MXU_DIM:(c + 1) * MXU_DIM], mean_mat) for c in range(x.shape[1] // MXU_DIM)]


def _norm_rope_tiles(x, mean_sq, gain, cos_t, sin_t):
    rows = x.shape[0]
    lane = lax.broadcasted_iota(jnp.int32, (rows, LANES), 1)
    first_half = (lane % SWA_HEAD_DIM) < (SWA_HEAD_DIM // 2)
    shift = SWA_HEAD_DIM // 2
    tiles = []
    for c, ms in enumerate(mean_sq):
        for t in range(MXU_DIM // LANES):
            lo = c * MXU_DIM + t * LANES
            y = (x[:, lo:lo + LANES] * lax.rsqrt(ms[:, t * LANES:(t + 1) * LANES] + EPS)) * gain
            rot = jnp.where(first_half, pltpu.roll(y, LANES - shift, 1), pltpu.roll(y, shift, 1))
            tiles.append(y * cos_t + rot * sin_t)
    return tiles


def _attend_all_heads(q_tiles, k_all, v_all, valid, sinks_ref, store):
    pairs = SWA_GROUP // 2
    ppc = SWA_PAIRS_PER_CHAIN
    rows = valid.shape[0]
    low = lax.broadcasted_iota(jnp.int32, (k_all.shape[0], LANES), 1) < SWA_HEAD_DIM
    chains = [(g, sub, parity) for g in range(SWA_KV_HEADS) for sub in range(pairs // ppc) for parity in range(2)]
    q_group = {}

    def scores(g, sub, parity):
        lo = (g // 2) * LANES
        have_low, want_low = g % 2 == 0, parity == 0

        def place(x):
            src = x if have_low == want_low else pltpu.roll(x, SWA_HEAD_DIM, 1)
            return jnp.where(low if want_low else jnp.logical_not(low), src, 0.0).astype(BF16)
        if (g, sub) not in q_group:
            first = g * pairs + sub * ppc
            q_group[(g, sub)] = jnp.concatenate(q_tiles[first:first + ppc], axis=0).astype(BF16)
        return _dot_nt(q_group[(g, sub)], place(k_all[:, lo:lo + LANES])), place(v_all[:, lo:lo + LANES])

    def softmax(g, sub, parity, s_all):
        p_list, inv_list = [], []
        for i in range(ppc):
            sink = sinks_ref[g * SWA_GROUP + 2 * (sub * ppc + i) + parity] * LOG2_E
            s = jnp.where(valid, s_all[i * rows:(i + 1) * rows, :], NEG_INF)
            m = jnp.maximum(jnp.max(s, axis=-1, keepdims=True), sink)
            p = jnp.exp2(s - m)
            inv_list.append(1.0 / (jnp.sum(p, axis=-1, keepdims=True) + jnp.exp2(sink - m)))
            p_list.append(p.astype(BF16))
        return jnp.concatenate(p_list, axis=0), inv_list

    partial = None
    for g, sub, parity in chains:
        s_all, vv = scores(g, sub, parity)
        p_all, inv_list = softmax(g, sub, parity, s_all)
        pv = _dot(p_all, vv)
        outs = [pv[i * rows:(i + 1) * rows, :] * inv_list[i] for i in range(ppc)]
        if parity == 0:
            partial = outs
        else:
            for i in range(ppc):
                store(g * pairs + sub * ppc + i, (partial[i] + outs[i]).astype(BF16))


def _swa_prompt_body(sinks_ref, q_ref, k_ref, v_ref, cb_ref, sb_ref, cr_ref, sr_ref, crs_ref, srs_ref,
                     gq_ref, gk_ref, mm_ref, pa_ref, ga_ref, gb_ref, wpb_ref,
                     m_ref, knew_ref, vnew_ref, kprev_scr, vprev_scr, ob_scr):
    step = pl.program_id(0)
    n = jnp.minimum(step, pl.num_programs(0) - 2)
    blk = WINDOW
    nsub = q_ref.shape[0] // blk
    cur = lax.rem(step, 2)
    prev = 1 - cur

    @pl.when(step == 0)
    def _():
        kprev_scr[...] = jnp.zeros_like(kprev_scr)
        vprev_scr[...] = jnp.zeros_like(vprev_scr)
        ob_scr[...] = jnp.zeros_like(ob_scr)

    mean_mat = mm_ref[...]
    qi = lax.broadcasted_iota(jnp.int32, (blk, 2 * blk), 0)
    kj = lax.broadcasted_iota(jnp.int32, (blk, 2 * blk), 1)
    dist = blk + qi - kj
    in_window = (dist >= 0) & (dist <= WINDOW)
    k_prev, v_prev = kprev_scr[...], vprev_scr[...]
    for sub in range(nsub):
        rsl = slice(sub * blk, (sub + 1) * blk)
        b = n * nsub + sub
        cb, sb = cb_ref[pl.ds(b, 1), :], sb_ref[pl.ds(b, 1), :]
        cos_t = cb * cr_ref[...] - sb * sr_ref[...]
        sin_t = sb * crs_ref[...] + cb * srs_ref[...]
        xq, xk = q_ref[rsl, :].astype(F32), k_ref[rsl, :].astype(F32)
        ms_q, ms_k = _head_mean_squares(xq, mean_mat), _head_mean_squares(xk, mean_mat)
        if sub == 0:
            m_ref[...] = (_sigmoid(ga_ref[...].astype(F32)) * pa_ref[...]
                          + _sigmoid(gb_ref[...].astype(F32)) * _dot(ob_scr[prev], wpb_ref[...])).astype(BF16)
        q_tiles = [t * SWA_SCORE_SCALE for t in _norm_rope_tiles(xq, ms_q, gq_ref[...], cos_t, sin_t)]
        kn = jnp.concatenate(_norm_rope_tiles(xk, ms_k, gk_ref[...], cos_t, sin_t), axis=1)
        vn = v_ref[rsl, :].astype(F32)
        k_all = jnp.concatenate([k_prev, kn], axis=0)
        v_all = jnp.concatenate([v_prev, vn], axis=0)
        valid = in_window & ((b - 1) * blk + kj >= 0)

        def store(tile, value, rsl=rsl):
            ob_scr[cur, rsl, tile * LANES:(tile + 1) * LANES] = value
        _attend_all_heads(q_tiles, k_all, v_all, valid, sinks_ref, store)
        k_prev, v_prev = kn, vn

    knew_ref[...] = k_prev
    vnew_ref[...] = v_prev
    kprev_scr[...] = k_prev
    vprev_scr[...] = v_prev


def _rope_block_tables(nblocks, blk):
    half = SWA_HEAD_DIM // 2
    inv_freq = ROPE_THETA ** (-(jnp.arange(half, dtype=F32) * 2.0) / SWA_HEAD_DIM)
    lane_freq = jnp.tile(inv_freq, LANES // half)[None, :]
    sign = jnp.tile(jnp.concatenate([-jnp.ones((half,), F32), jnp.ones((half,), F32)]), LANES // SWA_HEAD_DIM)[None, :]
    ang_b = (jnp.arange(nblocks, dtype=jnp.int32) * blk).astype(F32)[:, None] * lane_freq
    ang_r = jnp.arange(blk, dtype=jnp.int32).astype(F32)[:, None] * lane_freq
    cr, sr = jnp.cos(ang_r), jnp.sin(ang_r)
    return jnp.cos(ang_b), jnp.sin(ang_b), cr, sr, sign * cr, sign * sr


def _swa_prompt(proj, pa, sinks, g_q, g_k, w_pb):
    t = proj.shape[0]
    d = w_pb.shape[1]
    blk = WINDOW
    rows = SWA_PROMPT_BLOCKS_PER_STEP * blk
    nblocks, nsteps = t // blk, t // rows
    mean_mat = jnp.asarray(_head_mean_matrix(), BF16)
    tables = _rope_block_tables(nblocks, blk)
    qb, kb, vb = COL_SQ // SWA_Q, COL_SK // SWA_KV, COL_SV // SWA_KV
    whole = lambda n, s: (0, 0)
    cur = lambda n: jnp.minimum(n, nsteps - 1)
    prev = lambda n: jnp.maximum(n - 1, 0)
    gate_rows = lambda n: pl.multiple_of(prev(n) * rows, rows)
    grid_spec = pltpu.PrefetchScalarGridSpec(
        num_scalar_prefetch=1,
        grid=(nsteps + 1,),
        in_specs=[pl.BlockSpec((rows, SWA_Q), lambda n, s: (cur(n), qb)),
                  pl.BlockSpec((rows, SWA_KV), lambda n, s: (cur(n), kb)),
                  pl.BlockSpec((rows, SWA_KV), lambda n, s: (cur(n), vb)),
                  pl.BlockSpec((nblocks, LANES), whole),
                  pl.BlockSpec((nblocks, LANES), whole),
                  pl.BlockSpec((blk, LANES), whole),
                  pl.BlockSpec((blk, LANES), whole),
                  pl.BlockSpec((blk, LANES), whole),
                  pl.BlockSpec((blk, LANES), whole),
                  pl.BlockSpec((1, LANES), whole),
                  pl.BlockSpec((1, LANES), whole),
                  pl.BlockSpec((MXU_DIM, MXU_DIM), whole),
                  pl.BlockSpec((rows, d), lambda n, s: (prev(n), 0)),
                  pl.BlockSpec((pl.Element(rows), pl.Element(d)), lambda n, s: (gate_rows(n), COL_BG)),
                  pl.BlockSpec((pl.Element(rows), pl.Element(d)), lambda n, s: (gate_rows(n), COL_BG + d)),
                  pl.BlockSpec(w_pb.shape, whole, pipeline_mode=pl.Buffered(1))],
        out_specs=[pl.BlockSpec((rows, d), lambda n, s: (prev(n), 0)),
                   pl.BlockSpec((blk, SWA_KV), whole),
                   pl.BlockSpec((blk, SWA_KV), whole)],
        scratch_shapes=[pltpu.VMEM((blk, SWA_KV), F32), pltpu.VMEM((blk, SWA_KV), F32),
                        pltpu.VMEM((2, rows, SWA_Q), BF16)],
    )
    return pl.pallas_call(
        _swa_prompt_body,
        grid_spec=grid_spec,
        out_shape=[jax.ShapeDtypeStruct((t, d), BF16),
                   jax.ShapeDtypeStruct((blk, SWA_KV), F32),
                   jax.ShapeDtypeStruct((blk, SWA_KV), F32)],
        compiler_params=_params("arbitrary"),
        name="swa_prompt",
    )(sinks, proj, proj, proj, *tables, g_q, g_k, mean_mat, pa, proj, proj, w_pb)


def _swa_sample_body(sinks_ref, q_ref, k_ref, v_ref, bk_ref, bv_ref, cos_ref, sin_ref, gq_ref, gk_ref, mm_ref,
                     ob_ref, kout_ref, vout_ref):
    nseq = bk_ref.shape[0]
    rows = nseq * DEC_SEQ
    cos_t = jnp.concatenate([cos_ref[...]] * nseq, axis=0)
    sin_t = jnp.concatenate([sin_ref[...]] * nseq, axis=0)
    mean_mat = mm_ref[...]
    q_tiles = [t * SWA_SCORE_SCALE
               for t in _qk_norm_rope(q_ref[...].astype(F32), gq_ref[...], cos_t, sin_t, mean_mat)]
    kn = jnp.concatenate(_qk_norm_rope(k_ref[...].astype(F32), gk_ref[...], cos_t, sin_t, mean_mat), axis=1)
    vn = v_ref[...].astype(F32)

    lane_w = lax.broadcasted_iota(jnp.int32, (SWA_KV, WINDOW), 1)
    pad_rows = jnp.zeros((WINDOW - DEC_SEQ, SWA_KV), F32)
    for s in range(nseq):
        rsl = slice(s * DEC_SEQ, (s + 1) * DEC_SEQ)
        for buf_ref, new, out_ref in ((bk_ref, kn, kout_ref), (bv_ref, vn, vout_ref)):
            new_t = jnp.concatenate([pad_rows, new[rsl, :]], axis=0).T
            out_ref[s] = jnp.where(lane_w < WINDOW - DEC_SEQ, pltpu.roll(buf_ref[s], WINDOW - DEC_SEQ, 1), new_t)

    tok = lax.broadcasted_iota(jnp.int32, (rows, WINDOW), 0) % DEC_SEQ
    valid_c = lax.broadcasted_iota(jnp.int32, (rows, WINDOW), 1) >= tok
    r_n = lax.broadcasted_iota(jnp.int32, (rows, LANES), 0)
    c_n = lax.broadcasted_iota(jnp.int32, (rows, LANES), 1)
    valid_n = (r_n // DEC_SEQ == c_n // DEC_SEQ) & (c_n % DEC_SEQ <= r_n % DEC_SEQ)
    pad_new = jnp.zeros((LANES - rows, LANES), F32)

    pairs = SWA_GROUP // 2
    zeros_dk = jnp.zeros((SWA_HEAD_DIM, nseq * WINDOW), F32)
    zero_blk = jnp.zeros((DEC_SEQ, WINDOW), F32)
    low = lax.broadcasted_iota(jnp.int32, (rows, LANES), 1) < SWA_HEAD_DIM
    for g in range(SWA_KV_HEADS):
        dsl = slice(g * SWA_HEAD_DIM, (g + 1) * SWA_HEAD_DIM)
        kt = jnp.concatenate([bk_ref[s, dsl, :] for s in range(nseq)], axis=1)
        vt = jnp.concatenate([bv_ref[s, dsl, :] for s in range(nseq)], axis=1)
        lo = (g // 2) * LANES
        kn_tile, vn_tile = kn[:, lo:lo + LANES], vn[:, lo:lo + LANES]
        kv_low = g % 2 == 0
        q_g = jnp.concatenate(q_tiles[g * pairs:(g + 1) * pairs], axis=0).astype(BF16)
        out = None
        for parity in range(2):
            want_low = parity == 0
            stack = (lambda a: jnp.concatenate([a, zeros_dk], axis=0)) if want_low else \
                    (lambda a: jnp.concatenate([zeros_dk, a], axis=0))
            keep = low if want_low else jnp.logical_not(low)
            align = (lambda a: a) if kv_low == want_low else (lambda a: pltpu.roll(a, SWA_HEAD_DIM, 1))
            place = lambda a: jnp.concatenate([jnp.where(keep, align(a), 0.0), pad_new], axis=0).astype(BF16)
            kn_p, vn_p = place(kn_tile), place(vn_tile)
            s_c = _dot(q_g, stack(kt).astype(BF16))
            s_n = _dot_nt(q_g, kn_p)
            pc_rows, pn_rows, inv = [], [], []
            for pair in range(pairs):
                sink = sinks_ref[g * SWA_GROUP + 2 * pair + parity] * LOG2_E
                psl = slice(pair * rows, (pair + 1) * rows)
                sc = jnp.concatenate(
                    [s_c[pair * rows + s * DEC_SEQ: pair * rows + (s + 1) * DEC_SEQ, s * WINDOW:(s + 1) * WINDOW]
                     for s in range(nseq)], axis=0)
                sc = jnp.where(valid_c, sc, NEG_INF)
                sn = jnp.where(valid_n, s_n[psl, :], NEG_INF)
                m = jnp.maximum(jnp.maximum(jnp.max(sc, axis=-1, keepdims=True),
                                            jnp.max(sn, axis=-1, keepdims=True)), sink)
                pc = jnp.exp2(sc - m)
                pn = jnp.exp2(sn - m)
                inv.append(1.0 / (jnp.sum(pc, axis=-1, keepdims=True) + jnp.sum(pn, axis=-1, keepdims=True)
                                  + jnp.exp2(sink - m)))
                pn_rows.append(pn)
                for s in range(nseq):
                    blocks = [pc[s * DEC_SEQ:(s + 1) * DEC_SEQ, :] if s2 == s else zero_blk for s2 in range(nseq)]
                    pc_rows.append(jnp.concatenate(blocks, axis=1))
            pv = (_dot_nt(jnp.concatenate(pc_rows, axis=0).astype(BF16), stack(vt).astype(BF16))
                  + _dot(jnp.concatenate(pn_rows, axis=0).astype(BF16), vn_p))
            pv = jnp.concatenate([pv[pair * rows:(pair + 1) * rows, :] * inv[pair] for pair in range(pairs)], axis=0)
            out = pv if out is None else out + pv
        for pair in range(pairs):
            tile = g * pairs + pair
            ob_ref[:, tile * LANES:(tile + 1) * LANES] = out[pair * rows:(pair + 1) * rows, :].astype(BF16)


def _swa_sample(proj, buf_k, buf_v, sinks, cos_t, sin_t, g_q, g_k):
    t = proj.shape[0]
    nseq = SWA_SAMPLE_BATCH
    rows = nseq * DEC_SEQ
    mean_mat = jnp.asarray(_head_mean_matrix(), BF16)
    qb, kb, vb = COL_SQ // SWA_Q, COL_SK // SWA_KV, COL_SV // SWA_KV
    grid_spec = pltpu.PrefetchScalarGridSpec(
        num_scalar_prefetch=1,
        grid=(t // rows,),
        in_specs=[pl.BlockSpec((rows, SWA_Q), lambda n, s: (n, qb)),
                  pl.BlockSpec((rows, SWA_KV), lambda n, s: (n, kb)),
                  pl.BlockSpec((rows, SWA_KV), lambda n, s: (n, vb)),
                  pl.BlockSpec((nseq, SWA_KV, WINDOW), lambda n, s: (n, 0, 0)),
                  pl.BlockSpec((nseq, SWA_KV, WINDOW), lambda n, s: (n, 0, 0)),
                  pl.BlockSpec((DEC_SEQ, LANES), lambda n, s: (0, 0)),
                  pl.BlockSpec((DEC_SEQ, LANES), lambda n, s: (0, 0)),
                  pl.BlockSpec((1, LANES), lambda n, s: (0, 0)),
                  pl.BlockSpec((1, LANES), lambda n, s: (0, 0)),
                  pl.BlockSpec((MXU_DIM, MXU_DIM), lambda n, s: (0, 0))],
        out_specs=[pl.BlockSpec((rows, SWA_Q), lambda n, s: (n, 0)),
                   pl.BlockSpec((nseq, SWA_KV, WINDOW), lambda n, s: (n, 0, 0)),
                   pl.BlockSpec((nseq, SWA_KV, WINDOW), lambda n, s: (n, 0, 0))],
    )
    return pl.pallas_call(
        _swa_sample_body,
        grid_spec=grid_spec,
        out_shape=[jax.ShapeDtypeStruct((t, SWA_Q), BF16),
                   jax.ShapeDtypeStruct(buf_k.shape, F32),
                   jax.ShapeDtypeStruct(buf_v.shape, F32)],
        compiler_params=_params("arbitrary"),
        name="swa_sample",
    )(sinks, proj, proj, proj, buf_k, buf_v, cos_t, sin_t, g_q, g_k, mean_mat)


def _merge_body(oa_ref, ob_ref, wpa_ref, wpb_ref, ga_ref, gb_ref, m_ref):
    ga = _sigmoid(ga_ref[...].astype(F32))
    gb = _sigmoid(gb_ref[...].astype(F32))
    m_ref[...] = (ga * _dot(oa_ref[...], wpa_ref[...]) + gb * _dot(ob_ref[...], wpb_ref[...])).astype(BF16)


def _merge(oa, ob, proj, w_pa, w_pb, tm=1024, tn=512):
    t, d = oa.shape
    ga_blk, gb_blk = COL_BG // tn, (COL_BG + D_MODEL) // tn
    return pl.pallas_call(
        _merge_body,
        grid=(t // tm, d // tn),
        in_specs=[pl.BlockSpec((tm, d), lambda i, j: (i, 0)),
                  pl.BlockSpec((tm, d), lambda i, j: (i, 0)),
                  pl.BlockSpec((d, tn), lambda i, j: (0, j)),
                  pl.BlockSpec((d, tn), lambda i, j: (0, j)),
                  pl.BlockSpec((tm, tn), lambda i, j: (i, ga_blk + j)),
                  pl.BlockSpec((tm, tn), lambda i, j: (i, gb_blk + j))],
        out_specs=pl.BlockSpec((tm, tn), lambda i, j: (i, j)),
        out_shape=jax.ShapeDtypeStruct((t, d), BF16),
        compiler_params=_params("arbitrary", "arbitrary"),
        name="merge",
    )(oa, ob, w_pa, w_pb, proj, proj)


OPROJ_ROW_CHUNK = 256


def _oproj_body(m_ref, w_ref, x_ref, gate_ref, gn_ref, sc_ref, sh_ref, x1_ref, h2_ref):
    for row0 in range(0, x_ref.shape[0], OPROJ_ROW_CHUNK):
        sl = slice(row0, row0 + OPROJ_ROW_CHUNK)
        y = _dot(m_ref[sl, :], w_ref[...])
        x1 = x_ref[sl, :] + _mod_rows(gate_ref, row0, OPROJ_ROW_CHUNK) * y
        x1_ref[sl, :] = x1
        ms = jnp.mean(x1 * x1, axis=-1, keepdims=True)
        h = (x1 * lax.rsqrt(ms + EPS)) * gn_ref[...]
        h2_ref[sl, :] = (h * (1.0 + _mod_rows(sc_ref, row0, OPROJ_ROW_CHUNK))
                         + _mod_rows(sh_ref, row0, OPROJ_ROW_CHUNK)).astype(BF16)


def _oproj(merged, w_o, x, gate, gn, scale, shift, tm=512):
    t, d = x.shape
    rows = lambda i: (i, 0)
    whole = lambda i: (0, 0)
    mod = lambda m: (pl.BlockSpec((1, d), whole) if m.shape[0] == 1
                     else pl.BlockSpec((tm // DEC_SEQ, d), rows))
    return pl.pallas_call(
        _oproj_body,
        grid=(t // tm,),
        in_specs=[pl.BlockSpec((tm, d), rows),
                  pl.BlockSpec(w_o.shape, whole, pipeline_mode=pl.Buffered(1)),
                  pl.BlockSpec((tm, d), rows),
                  mod(gate),
                  pl.BlockSpec((1, d), whole),
                  mod(scale),
                  mod(shift)],
        out_specs=[pl.BlockSpec((tm, d), rows), pl.BlockSpec((tm, d), rows)],
        out_shape=[jax.ShapeDtypeStruct(x.shape, F32), jax.ShapeDtypeStruct(x.shape, BF16)],
        compiler_params=_params("arbitrary"),
        name="oproj",
    )(merged, w_o, x, gate, gn, scale, shift)


def _ffn_body(h_ref, x_ref, gate_ref, wg_ref, wu_ref, wd_ref, o_ref):
    f = pl.program_id(1)

    @pl.when(f == 0)
    def _():
        def rows(row0):
            o_ref[pl.ds(row0, ROW_CHUNK), :] = jnp.zeros((ROW_CHUNK, o_ref.shape[1]), F32)
        _row_loop(x_ref.shape[0], rows)

    h = h_ref[...]
    a = _silu(_dot(h, wg_ref[...])) * _dot(h, wu_ref[...])
    o_ref[...] += _dot(a.astype(BF16), wd_ref[...])

    @pl.when(f == pl.num_programs(1) - 1)
    def _():
        def rows(row0):
            sl = pl.ds(row0, ROW_CHUNK)
            o_ref[sl, :] = x_ref[sl, :] + _mod_rows(gate_ref, row0, ROW_CHUNK) * o_ref[sl, :]
        _row_loop(x_ref.shape[0], rows)


def _ffn(h, x, gate, w_gate, w_up, w_down, tm=1024, tf=512):
    t, d = x.shape
    dff = w_gate.shape[1]
    zero = lambda j: 0
    return pl.pallas_call(
        _ffn_body,
        grid=(t // tm, dff // tf),
        in_specs=[pl.BlockSpec((tm, d), lambda i, f: (i, 0)),
                  pl.BlockSpec((tm, d), lambda i, f: (i, 0)),
                  _mod_spec(gate, tm, d, zero),
                  pl.BlockSpec((d, tf), lambda i, f: (0, f)),
                  pl.BlockSpec((d, tf), lambda i, f: (0, f)),
                  pl.BlockSpec((tf, d), lambda i, f: (f, 0))],
        out_specs=pl.BlockSpec((tm, d), lambda i, f: (i, 0)),
        out_shape=jax.ShapeDtypeStruct(x.shape, F32),
        compiler_params=_params("arbitrary", "arbitrary"),
        name="ffn",
    )(h, x, gate, w_gate, w_up, w_down)


def _mods(ada):
    return tuple(ada[:, i * D_MODEL:(i + 1) * D_MODEL] for i in range(6))


def _layer_after_inproj(x, proj, glog, mods, pos, state, buf_k, buf_v, wts):
    g_q, g_k, sinks, g_out, w_pa, w_pb, w_o, gn_ffn, w_fg, w_fu, w_fd = wts
    bsz, tlen, d = x.shape
    x2 = x.reshape(bsz * tlen, d)
    _, _, gate_m, shift_f, scale_f, gate_f = mods
    if state is None:
        pa, s_new = _gla_prompt(proj, glog, g_out, w_pa)
        s_new = s_new[None]
        merged, k_new, v_new = _swa_prompt(proj, pa, sinks, g_q, g_k, w_pb)
        k_new = k_new.reshape(bsz, WINDOW, SWA_KV_HEADS, SWA_HEAD_DIM)
        v_new = v_new.reshape(bsz, WINDOW, SWA_KV_HEADS, SWA_HEAD_DIM)
    else:
        to_dev = lambda c: jnp.transpose(c, (0, 2, 3, 1)).reshape(bsz, SWA_KV, WINDOW)
        from_dev = lambda c: jnp.transpose(c.reshape(bsz, SWA_KV_HEADS, SWA_HEAD_DIM, WINDOW), (0, 3, 1, 2))
        cos_t, sin_t = _rope_tables(pos)
        oa, s_new = _gla_sample(proj, glog, state, g_out)
        ob, k_new, v_new = _swa_sample(proj, to_dev(buf_k), to_dev(buf_v), sinks, cos_t, sin_t, g_q, g_k)
        k_new, v_new = from_dev(k_new), from_dev(v_new)
        merged = _merge(oa, ob, proj, w_pa, w_pb)
    x1, h2 = _oproj(merged, w_o, x2, gate_m, gn_ffn, scale_f, shift_f)
    y = _ffn(h2, x1, gate_f, w_fg, w_fu, w_fd)
    return y.reshape(bsz, tlen, d), s_new, k_new, v_new


def kernel(x_prompt, x_sample, c_prompt, c_sample, state_gla, cache_swa_k, cache_swa_v, w_ada, b_ada, g_norm_mix, w_in, w_gk2, b_gk2, g_qnorm, g_knorm, sinks, g_gla_out, w_pa, w_pb, w_o, g_norm_ffn, w_ffn_gate, w_ffn_up, w_ffn_down):
    assert w_in.shape[0] == 1, "single trunk layer"
    pos_p = jnp.arange(SEQ, dtype=jnp.int32)
    pos_s = PAST_LEN + jnp.arange(DEC_SEQ, dtype=jnp.int32)
    l = 0
    n_c = 1 + DEC_BATCH
    pad = (-n_c) % (2 * SUBLANES)
    c_all = jnp.pad(jnp.concatenate([c_prompt, c_sample], axis=0), ((0, pad), (0, 0)))
    ada, wt = _ada(c_all, w_ada[l], b_ada[l].reshape(1, -1), jnp.transpose(w_in[l]))
    mods_p, mods_s = _mods(ada[0:1]), _mods(ada[1:n_c])

    wt_lr = jnp.pad(wt[W_IN_GLR:W_IN_GLR + GLA_RANK], ((0, LANES - GLA_RANK), (0, 0)))
    w_gk = jnp.pad(w_gk2[l], ((0, LANES - GLA_RANK), (0, 0))).astype(BF16)
    in_wts = (wt, wt_lr, w_gk, b_gk2[l].reshape(1, GLA_KEY))
    gn_mix = g_norm_mix[l].reshape(1, D_MODEL)

    later_weights = (w_pa[l], w_pb[l], w_o[l], w_ffn_gate[l], w_ffn_up[l], w_ffn_down[l])
    proj_p, glog_p, *later_bf16 = _inproj(x_prompt.reshape(-1, D_MODEL), gn_mix, mods_p[1], mods_p[0], *in_wts,
                                          cast_weights=later_weights)
    proj_s, glog_s = _inproj(x_sample.reshape(-1, D_MODEL), gn_mix, mods_s[1], mods_s[0], *in_wts)
    w_pa_b, w_pb_b, w_o_b, w_fg_b, w_fu_b, w_fd_b = later_bf16
    wts = (jnp.tile(g_qnorm[l], 2).reshape(1, LANES), jnp.tile(g_knorm[l], 2).reshape(1, LANES),
           sinks[l], g_gla_out[l].reshape(1, GLA_DV), w_pa_b, w_pb_b, w_o_b,
           g_norm_ffn[l].reshape(1, D_MODEL), w_fg_b, w_fu_b, w_fd_b)

    yp, sp, kp, vp = _layer_after_inproj(x_prompt, proj_p, glog_p, mods_p, pos_p, None, None, None, wts)
    ys, ss, ks, vs = _layer_after_inproj(x_sample, proj_s, glog_s, mods_s, pos_s, state_gla[l], cache_swa_k[l],
                                         cache_swa_v[l], wts)
    return (yp, ys, sp[None], kp[None], vp[None], ss[None], ks[None], vs[None])
```
